```python
import jax
import jax.numpy as jnp
from jax import lax
import numpy as np

D_MODEL = 2048
BATCH = 2
SEQ = 8192
DEPTH = 2

CTX_LEN = 256
GRID_W = 64
NORM_EPS = 1e-6

RW_HEADS = 16
RW_HEAD = 64
RW_WIDTH = RW_HEADS * RW_HEAD
RW_DECAY_LORA = 64
RW_A_LORA = 64
RW_GATE_LORA = 32
N_DIR = 2
RW_GN_EPS = 64e-5

MLA_HEADS = 16
MLA_NOPE = 64
MLA_ROPE = 32
MLA_QK = MLA_NOPE + MLA_ROPE
MLA_V = 64
MLA_WIDTH = MLA_HEADS * MLA_V
MLA_Q_RANK = 512
MLA_KV_RANK = 256
ROPE_BASE = 10000.0
Q_BLOCK = 128

PEER_HEADS = 8
PEER_NKEYS = 128
PEER_EXPERTS = PEER_NKEYS * PEER_NKEYS
PEER_QDIM = 256
PEER_HALF = PEER_QDIM // 2
PEER_TOPK = 16
PEER_BLOCK = 128

N_BRANCH = 2
RW_LORA_COLS = N_DIR * (RW_DECAY_LORA + RW_A_LORA + RW_GATE_LORA)
RW_COLS = 3 * RW_WIDTH + RW_LORA_COLS
MLA_COLS = MLA_Q_RANK + MLA_KV_RANK + MLA_ROPE
GATE_COLS = N_BRANCH * D_MODEL
IN_COLS = RW_COLS + MLA_COLS + GATE_COLS

kernel_name = 'hybrid_rwkv7_mla_peer_dit_block'


def rmsnorm(x, g, eps=NORM_EPS):
    xf = x.astype(jnp.float32)
    y = xf * lax.rsqrt(jnp.mean(xf * xf, -1, keepdims=True) + eps)
    return (y * g.astype(jnp.float32)).astype(x.dtype)


def modulate(h, shift, scale):
    return h * (1.0 + scale) + shift


def axial_rope_tables(n_tok):
    rows = n_tok // GRID_W
    row = jnp.repeat(jnp.arange(rows, dtype=jnp.float32), GRID_W)
    col = jnp.tile(jnp.arange(GRID_W, dtype=jnp.float32), rows)
    n_freq = MLA_ROPE // 4
    freqs = ROPE_BASE ** (-jnp.arange(n_freq, dtype=jnp.float32) / n_freq)
    ang = jnp.concatenate([row[:, None] * freqs, col[:, None] * freqs], -1)
    return jnp.cos(ang), jnp.sin(ang)


def apply_rope(x, cos, sin):
    half = x.shape[-1] // 2
    x1, x2 = x[..., :half], x[..., half:]
    return jnp.concatenate([x1 * cos - x2 * sin, x1 * sin + x2 * cos], -1).astype(x.dtype)


def centred_shift(p):
    prev = jnp.pad(p[:, :-1], ((0, 0), (1, 0), (0, 0)))
    nxt = jnp.pad(p[:, 1:], ((0, 0), (0, 1), (0, 0)))
    return 0.5 * (prev + nxt) - p


def rwkv_features(p, mu, w0, w_up, a0, a_up, g_up, k_k, k_a):
    B, T, _ = p.shape
    f32 = lambda z: z.astype(jnp.float32)
    p = f32(p + centred_shift(p) * mu)
    r = p[..., :RW_WIDTH]
    k = p[..., RW_WIDTH:2 * RW_WIDTH]
    v = p[..., 2 * RW_WIDTH:3 * RW_WIDTH]
    lo = 3 * RW_WIDTH
    wd = p[..., lo:lo + N_DIR * RW_DECAY_LORA].reshape(B, T, N_DIR, RW_DECAY_LORA)
    lo += N_DIR * RW_DECAY_LORA
    ad = p[..., lo:lo + N_DIR * RW_A_LORA].reshape(B, T, N_DIR, RW_A_LORA)
    lo += N_DIR * RW_A_LORA
    gd = p[..., lo:lo + N_DIR * RW_GATE_LORA].reshape(B, T, N_DIR, RW_GATE_LORA)
    w = -jax.nn.softplus(-(f32(w0) + jnp.einsum('btdr,drc->btdc', jnp.tanh(wd), f32(w_up)))) - 0.5
    decay = jnp.exp(-jnp.exp(w))
    a = jax.nn.sigmoid(f32(a0) + jnp.einsum('btdr,drc->btdc', ad, f32(a_up)))
    g = jnp.einsum('btdr,drc->btdc', jax.nn.sigmoid(gd), f32(g_up))
    kk = (k * f32(k_k)).reshape(B, T, RW_HEADS, RW_HEAD)
    kk = kk * lax.rsqrt(jnp.sum(kk * kk, -1, keepdims=True) + 1e-12)
    k_eff = k[:, :, None, :] * (1.0 + (a - 1.0) * f32(k_a))
    heads = lambda z: z.reshape(z.shape[:-1] + (RW_HEADS, RW_HEAD))
    return heads(r), kk, heads(v), heads(decay), heads(a), heads(k_eff), g


def wkv_scan(s0, r, decay, k, v, kk, a, reverse):
    def step(s, xs):
        r_t, w_t, k_t, v_t, kk_t, a_t = xs
        sa = jnp.einsum('bhvk,bhk->bhv', s, -kk_t)
        s = (s * w_t[:, :, None, :]
             + sa[..., None] * (kk_t * a_t)[:, :, None, :]
             + v_t[..., None] * k_t[:, :, None, :])
        return s, jnp.einsum('bhvk,bhk->bhv', s, r_t)
    xs = tuple(jnp.moveaxis(z, 1, 0) for z in (r, decay, k, v, kk, a))
    s_fin, y = lax.scan(step, s0, xs, reverse=reverse)
    return s_fin, jnp.moveaxis(y, 0, 1)


def rwkv_readout(y, r, k_eff, v, g, r_k, ln_w, ln_b, dtype):
    B, T = y.shape[:2]
    mean = jnp.mean(y, -1, keepdims=True)
    var = jnp.mean(jnp.square(y - mean), -1, keepdims=True)
    yn = ((y - mean) * lax.rsqrt(var + RW_GN_EPS)).reshape(B, T, RW_WIDTH)
    yn = yn * ln_w.astype(jnp.float32) + ln_b.astype(jnp.float32)
    bonus = (jnp.sum(r * k_eff * r_k.astype(jnp.float32), -1, keepdims=True) * v).reshape(B, T, RW_WIDTH)
    return ((yn + bonus) * g).astype(dtype)


def rwkv_branch(p_ctx, p_lat, need_ctx, feat_params, readout_params):
    r_c, kk_c, v_c, w_c, a_c, ke_c, g_c = rwkv_features(p_ctx, *feat_params)
    r_l, kk_l, v_l, w_l, a_l, ke_l, g_l = rwkv_features(p_lat, *feat_params)
    B = p_lat.shape[0]
    s0 = jnp.zeros((B, RW_HEADS, RW_HEAD, RW_HEAD), jnp.float32)
    out_ctx, out_lat = [], []
    for d in range(N_DIR):
        rev = d == 1
        s_ctx, y_c = wkv_scan(s0, r_c, w_c[:, :, d], ke_c[:, :, d], v_c, kk_c, a_c[:, :, d], rev)
        _, y_l = wkv_scan(s_ctx, r_l, w_l[:, :, d], ke_l[:, :, d], v_l, kk_l, a_l[:, :, d], rev)
        out_lat.append(rwkv_readout(y_l, r_l, ke_l[:, :, d], v_l, g_l[:, :, d], *readout_params, p_lat.dtype))
        if need_ctx:
            out_ctx.append(rwkv_readout(y_c, r_c, ke_c[:, :, d], v_c, g_c[:, :, d], *readout_params, p_ctx.dtype))
    y_lat = out_lat[0] + out_lat[1]
    y_ctx = out_ctx[0] + out_ctx[1] if need_ctx else None
    return y_ctx, y_lat


def mla_heads(p, q_norm, w_uq, kv_norm, w_ukv):
    B, T, _ = p.shape
    q_c = p[..., :MLA_Q_RANK]
    kv_c = p[..., MLA_Q_RANK:MLA_Q_RANK + MLA_KV_RANK]
    k_rope = p[..., MLA_Q_RANK + MLA_KV_RANK:]
    q = (rmsnorm(q_c, q_norm) @ w_uq).reshape(B, T, MLA_HEADS, MLA_QK)
    kv = (rmsnorm(kv_c, kv_norm) @ w_ukv).reshape(B, T, MLA_HEADS, MLA_NOPE + MLA_V)
    return q[..., :MLA_NOPE], q[..., MLA_NOPE:], kv[..., :MLA_NOPE], kv[..., MLA_NOPE:], k_rope


def join_rope_key(k_nope, k_rope):
    k_rope = jnp.broadcast_to(k_rope[:, :, None, :], k_nope.shape[:-1] + (MLA_ROPE,))
    return jnp.concatenate([k_nope, k_rope], -1)


def attend(q, k, v):
    s = jnp.einsum('bqhd,bkhd->bhqk', q, k).astype(jnp.float32) * (MLA_QK ** -0.5)
    pr = jax.nn.softmax(s, -1).astype(v.dtype)
    return jnp.einsum('bhqk,bkhd->bqhd', pr, v)


def mla_branch(p_ctx, p_lat, need_ctx, cos, sin, q_norm, w_uq, kv_norm, w_ukv):
    qn_c, qr_c, kn_c, v_c, kr_c = mla_heads(p_ctx, q_norm, w_uq, kv_norm, w_ukv)
    qn_l, qr_l, kn_l, v_l, kr_l = mla_heads(p_lat, q_norm, w_uq, kv_norm, w_ukv)
    k_c = join_rope_key(kn_c, kr_c)
    k_l = join_rope_key(kn_l, apply_rope(kr_l, cos, sin))
    q_l = jnp.concatenate([qn_l, apply_rope(qr_l, cos[:, None, :], sin[:, None, :])], -1)
    keys = jnp.concatenate([k_c, k_l], 1)
    vals = jnp.concatenate([v_c, v_l], 1)
    B, L = p_lat.shape[:2]
    nb = L // Q_BLOCK
    qb = jnp.moveaxis(q_l.reshape(B, nb, Q_BLOCK, MLA_HEADS, MLA_QK), 1, 0)
    o = lax.map(lambda qblk: attend(qblk, keys, vals), qb)
    y_lat = jnp.moveaxis(o, 0, 1).reshape(B, L, MLA_WIDTH)
    y_ctx = None
    if need_ctx:
        q_c = jnp.concatenate([qn_c, qr_c], -1)
        y_ctx = attend(q_c, k_c, v_c).reshape(B, p_ctx.shape[1], MLA_WIDTH)
    return y_ctx, y_lat


def gated_merge(y_a, y_b, gate_logits, gate_b, w_branch_a, w_branch_b, w_out):
    B, T, _ = gate_logits.shape
    gl = gate_logits.reshape(B, T, N_BRANCH, D_MODEL) + gate_b
    g_a = jax.nn.sigmoid(gl[:, :, 0])
    g_b = jax.nn.sigmoid(gl[:, :, 1])
    return (g_a * (y_a @ w_branch_a) + g_b * (y_b @ w_branch_b)) @ w_out


def mixer_sublayer(h_ctx, h_lat, need_ctx, cos, sin, w_in, gate_b, rk_mu, rk_w0, rk_w_up, rk_a0, rk_a_up,
                   rk_g_up, rk_k_k, rk_k_a, rk_r_k, rk_ln_w, rk_ln_b, mla_q_norm, mla_w_uq, mla_kv_norm,
                   mla_w_ukv, w_branch_a, w_branch_b, w_out):
    p_lat = h_lat @ w_in
    p_ctx = h_ctx @ w_in
    feat_params = (rk_mu, rk_w0, rk_w_up, rk_a0, rk_a_up, rk_g_up, rk_k_k, rk_k_a)
    readout_params = (rk_r_k, rk_ln_w, rk_ln_b)
    ya_c, ya_l = rwkv_branch(p_ctx[..., :RW_COLS], p_lat[..., :RW_COLS], need_ctx, feat_params, readout_params)
    lo, hi = RW_COLS, RW_COLS + MLA_COLS
    yb_c, yb_l = mla_branch(p_ctx[..., lo:hi], p_lat[..., lo:hi], need_ctx, cos, sin,
                            mla_q_norm, mla_w_uq, mla_kv_norm, mla_w_ukv)
    out_lat = gated_merge(ya_l, yb_l, p_lat[..., hi:], gate_b, w_branch_a, w_branch_b, w_out)
    out_ctx = None
    if need_ctx:
        out_ctx = gated_merge(ya_c, yb_c, p_ctx[..., hi:], gate_b, w_branch_a, w_branch_b, w_out)
    return out_ctx, out_lat


def peer(h, w_q, keys, u_tab, v_tab):
    B, T, D = h.shape
    n = B * T
    hf = h.reshape(n, D)
    q = (hf @ w_q).reshape(n, PEER_HEADS, 2, PEER_HALF)
    s = jnp.einsum('nhpd,hpkd->nhpk', q, keys).astype(jnp.float32)
    v1, i1 = lax.top_k(s[:, :, 0], PEER_TOPK)
    v2, i2 = lax.top_k(s[:, :, 1], PEER_TOPK)
    cand_s = (v1[..., :, None] + v2[..., None, :]).reshape(n, PEER_HEADS, PEER_TOPK * PEER_TOPK)
    cand_i = (i1[..., :, None] * PEER_NKEYS + i2[..., None, :]).reshape(n, PEER_HEADS, PEER_TOPK * PEER_TOPK)
    top_s, pos = lax.top_k(cand_s, PEER_TOPK)
    idx = jnp.take_along_axis(cand_i, pos, -1).reshape(n, PEER_HEADS * PEER_TOPK)
    gates = jax.nn.softmax(top_s, -1).reshape(n, PEER_HEADS * PEER_TOPK).astype(h.dtype)
    nb = n // PEER_BLOCK

    def block(args):
        hb, ib, gb = args
        act = jax.nn.gelu(jnp.einsum('nd,ned->ne', hb, u_tab[ib]), approximate=False)
        return jnp.einsum('ne,ned->nd', gb * act, v_tab[ib])

    out = lax.map(block, (hf.reshape(nb, PEER_BLOCK, D),
                          idx.reshape(nb, PEER_BLOCK, -1),
                          gates.reshape(nb, PEER_BLOCK, -1)))
    return out.reshape(B, T, D)


def setup_inputs(seed: int = 0) -> dict:
    key = jax.random.key(seed)
    ks = iter(jax.random.split(key, 40))
    nrm = lambda shape, scale: scale * jax.random.normal(next(ks), shape, jnp.float32)
    uni = lambda shape, lo, hi: jax.random.uniform(next(ks), shape, jnp.float32, lo, hi)
    D, L = D_MODEL, DEPTH
    return {
        'x': nrm((BATCH, SEQ, D), 1.0),
        'c': nrm((BATCH, D), 1.0),
        'ctx': nrm((BATCH, CTX_LEN, D), 1.0),
        'c_ctx': nrm((D,), 1.0),
        'ada_w': nrm((L, D, 6 * D), 0.5 * D ** -0.5),
        'ada_b': nrm((L, 6 * D), 0.02),
        'norm_mix_g': 1.0 + nrm((L, D), 0.02),
        'w_in': nrm((L, D, IN_COLS), D ** -0.5),
        'gate_b': nrm((L, N_BRANCH, D), 0.02),
        'rk_mu': uni((L, RW_COLS), 0.0, 1.0),
        'rk_w0': uni((L, N_DIR, RW_WIDTH), -6.0, -1.0),
        'rk_w_up': nrm((L, N_DIR, RW_DECAY_LORA, RW_WIDTH), 0.5 * RW_DECAY_LORA ** -0.5),
        'rk_a0': nrm((L, N_DIR, RW_WIDTH), 0.1),
        'rk_a_up': nrm((L, N_DIR, RW_A_LORA, RW_WIDTH), 0.5 * RW_A_LORA ** -0.5),
        'rk_g_up': nrm((L, N_DIR, RW_GATE_LORA, RW_WIDTH), RW_GATE_LORA ** -0.5),
        'rk_k_k': 0.85 + nrm((L, RW_WIDTH), 0.02),
        'rk_k_a': 1.0 + nrm((L, RW_WIDTH), 0.02),
        'rk_r_k': nrm((L, RW_HEADS, RW_HEAD), 0.1),
        'rk_ln_w': 1.0 + nrm((L, RW_WIDTH), 0.02),
        'rk_ln_b': nrm((L, RW_WIDTH), 0.02),
        'mla_q_norm': 1.0 + nrm((L, MLA_Q_RANK), 0.02),
        'mla_w_uq': nrm((L, MLA_Q_RANK, MLA_HEADS * MLA_QK), MLA_Q_RANK ** -0.5),
        'mla_kv_norm': 1.0 + nrm((L, MLA_KV_RANK), 0.02),
        'mla_w_ukv': nrm((L, MLA_KV_RANK, MLA_HEADS * (MLA_NOPE + MLA_V)), MLA_KV_RANK ** -0.5),
        'w_branch_a': nrm((L, RW_WIDTH, D), RW_WIDTH ** -0.5),
        'w_branch_b': nrm((L, MLA_WIDTH, D), MLA_WIDTH ** -0.5),
        'w_out': nrm((L, D, D), D ** -0.5),
        'norm_ffn_g': 1.0 + nrm((L, D), 0.02),
        'peer_w_q': nrm((L, D, PEER_HEADS * PEER_QDIM), D ** -0.5),
        'peer_keys': nrm((L, PEER_HEADS, 2, PEER_NKEYS, PEER_HALF), PEER_HALF ** -0.5),
        'peer_u': nrm((L, PEER_EXPERTS, D), D ** -0.5),
        'peer_v': nrm((L, PEER_EXPERTS, D), 0.25),
        'final_norm_g': 1.0 + nrm((D,), 0.02),
    }


def reference(x, c, ctx, c_ctx, ada_w, ada_b, norm_mix_g, w_in, gate_b, rk_mu, rk_w0, rk_w_up, rk_a0,
              rk_a_up, rk_g_up, rk_k_k, rk_k_a, rk_r_k, rk_ln_w, rk_ln_b, mla_q_norm, mla_w_uq, mla_kv_norm,
              mla_w_ukv, w_branch_a, w_branch_b, w_out, norm_ffn_g, peer_w_q, peer_keys, peer_u, peer_v,
              final_norm_g):
    cos, sin = axial_rope_tables(x.shape[1])
    xc = ctx
    for l in range(DEPTH):
        need_ctx = l < DEPTH - 1
        mod_l = (jax.nn.silu(c) @ ada_w[l] + ada_b[l])[:, None, :]
        mod_c = (jax.nn.silu(c_ctx) @ ada_w[l] + ada_b[l])[None, None, :]
        sh1_l, sc1_l, gt1_l, sh2_l, sc2_l, gt2_l = jnp.split(mod_l, 6, -1)
        sh1_c, sc1_c, gt1_c, sh2_c, sc2_c, gt2_c = jnp.split(mod_c, 6, -1)
        h_l = modulate(rmsnorm(x, norm_mix_g[l]), sh1_l, sc1_l)
        h_c = modulate(rmsnorm(xc, norm_mix_g[l]), sh1_c, sc1_c)
        m_c, m_l = mixer_sublayer(h_c, h_l, need_ctx, cos, sin, w_in[l], gate_b[l], rk_mu[l], rk_w0[l],
                                  rk_w_up[l], rk_a0[l], rk_a_up[l], rk_g_up[l], rk_k_k[l], rk_k_a[l],
                                  rk_r_k[l], rk_ln_w[l], rk_ln_b[l], mla_q_norm[l], mla_w_uq[l],
                                  mla_kv_norm[l], mla_w_ukv[l], w_branch_a[l], w_branch_b[l], w_out[l])
        x = x + gt1_l * m_l
        h_l = modulate(rmsnorm(x, norm_ffn_g[l]), sh2_l, sc2_l)
        x = x + gt2_l * peer(h_l, peer_w_q[l], peer_keys[l], peer_u[l], peer_v[l])
        if need_ctx:
            xc = xc + gt1_c * m_c
            h_c = modulate(rmsnorm(xc, norm_ffn_g[l]), sh2_c, sc2_c)
            xc = xc + gt2_c * peer(h_c, peer_w_q[l], peer_keys[l], peer_u[l], peer_v[l])
    return rmsnorm(x, final_norm_g)
```

```python
import functools
import math

import jax
import jax.numpy as jnp
from jax import lax
from jax.experimental import pallas as pl
from jax.experimental.pallas import tpu as pltpu

F32 = jnp.float32
BF16 = jnp.bfloat16
I32 = jnp.int32

NORM_EPS = 1e-6
GRID_W = 64
ROPE_BASE = 10000.0

RW_HEADS = 16
RW_HEAD = 64
RW_WIDTH = RW_HEADS * RW_HEAD
RW_DECAY_LORA = 64
RW_A_LORA = 64
RW_GATE_LORA = 32
RW_GN_EPS = 64e-5
RW_COLS = 3 * RW_WIDTH + 2 * (RW_DECAY_LORA + RW_A_LORA + RW_GATE_LORA)
RW_COLS_PAD = 3456

MLA_HEADS = 16
MLA_NOPE = 64
MLA_ROPE = 32
MLA_QK = MLA_NOPE + MLA_ROPE
MLA_V = 64
MLA_Q_RANK = 512
MLA_KV_RANK = 256
MLA_HEAD_PAD = 128
MLA_P_COLS = MLA_Q_RANK + MLA_KV_RANK + 2 * MLA_HEAD_PAD

PEER_HEADS = 8
PEER_NKEYS = 128
PEER_HALF = 128
PEER_TOPK = 16
PEER_SEL = PEER_HEADS * PEER_TOPK

LANES = 128
SUBLANES = 8
VMEM_LIMIT = 56 * 1024 * 1024


def _cparams(sem):
    return pltpu.CompilerParams(dimension_semantics=sem, vmem_limit_bytes=VMEM_LIMIT)


def _pick_tile(n, cap, mult=LANES):
    best = None
    t = mult
    while t <= min(n, cap):
        if n % t == 0:
            best = t
        t += mult
    assert best is not None, (n, cap)
    return best


def _mm_kernel(*refs, n_extra, epilogue):
    a_ref, w_ref = refs[0], refs[1]
    extra = refs[2:2 + n_extra]
    o_ref = refs[2 + n_extra]
    acc = jnp.dot(a_ref[...].astype(BF16), w_ref[...].astype(BF16), preferred_element_type=F32)
    if epilogue is not None:
        acc = epilogue(acc, *[e[...] for e in extra])
    o_ref[...] = acc.astype(o_ref.dtype)


def _mm(a, w, *, tm, tn, out_dtype=F32, epilogue=None, extras=()):
    m, k = a.shape
    k2, n = w.shape
    assert k == k2 and m % tm == 0 and n % tn == 0, (a.shape, w.shape, tm, tn)
    in_specs = [pl.BlockSpec((tm, k), lambda j, i: (i, 0)),
                pl.BlockSpec((k, tn), lambda j, i: (0, j))] + [s for _, s in extras]
    return pl.pallas_call(
        functools.partial(_mm_kernel, n_extra=len(extras), epilogue=epilogue),
        grid=(n // tn, m // tm),
        in_specs=in_specs,
        out_specs=pl.BlockSpec((tm, tn), lambda j, i: (i, j)),
        out_shape=jax.ShapeDtypeStruct((m, n), out_dtype),
        compiler_params=_cparams(("parallel", "parallel")),
    )(a, w, *[x for x, _ in extras])


def _sigmoid(x):
    return 1.0 / (1.0 + jnp.exp(-x))


def _mod_row(i, tiles_per_batch, lat_tiles, part, n_batch):
    which = jnp.where(i % tiles_per_batch >= lat_tiles, n_batch, i // tiles_per_batch)
    return which * 6 + part


def _rms(x, g):
    return x * lax.rsqrt(jnp.mean(x * x, -1, keepdims=True) + NORM_EPS) * g


def _norm_mod_kernel(x_ref, g_ref, sh_ref, sc_ref, h_ref):
    y = _rms(x_ref[...], g_ref[...])
    h_ref[...] = (y * (1.0 + sc_ref[...]) + sh_ref[...]).astype(h_ref.dtype)


def _resid_norm_mod_kernel(x_ref, d_ref, gt_ref, g_ref, sh_ref, sc_ref, xo_ref, h_ref):
    x = x_ref[...] + gt_ref[...] * d_ref[...]
    xo_ref[...] = x
    y = _rms(x, g_ref[...])
    h_ref[...] = (y * (1.0 + sc_ref[...]) + sh_ref[...]).astype(h_ref.dtype)


def _resid_final_norm_kernel(x_ref, d_ref, gt_ref, g_ref, o_ref):
    x = x_ref[...] + gt_ref[...] * d_ref[...]
    o_ref[...] = _rms(x, g_ref[...])


def _mod_spec(d, part, dims, tm):
    tpb, lt, nb = dims["T"] // tm, dims["L"] // tm, dims["B"]
    return pl.BlockSpec((None, 1, d), lambda i: (_mod_row(i, tpb, lt, part, nb), 0, 0))


def _norm_mod(x, g, mods, part_sh, part_sc, dims, tm=256):
    m, d = x.shape
    row = pl.BlockSpec((tm, d), lambda i: (i, 0))
    return pl.pallas_call(
        _norm_mod_kernel, grid=(m // tm,),
        in_specs=[row, pl.BlockSpec((1, d), lambda i: (0, 0)),
                  _mod_spec(d, part_sh, dims, tm), _mod_spec(d, part_sc, dims, tm)],
        out_specs=row, out_shape=jax.ShapeDtypeStruct((m, d), BF16),
        compiler_params=_cparams(("parallel",)),
    )(x, g, mods, mods)


def _resid_norm_mod(x, delta, mods_gt, part_gt, g, mods, part_sh, part_sc, dims, tm=256):
    m, d = x.shape
    row = pl.BlockSpec((tm, d), lambda i: (i, 0))
    return pl.pallas_call(
        _resid_norm_mod_kernel, grid=(m // tm,),
        in_specs=[row, row, _mod_spec(d, part_gt, dims, tm), pl.BlockSpec((1, d), lambda i: (0, 0)),
                  _mod_spec(d, part_sh, dims, tm), _mod_spec(d, part_sc, dims, tm)],
        out_specs=[row, row],
        out_shape=[jax.ShapeDtypeStruct((m, d), F32), jax.ShapeDtypeStruct((m, d), BF16)],
        compiler_params=_cparams(("parallel",)),
    )(x, delta, mods_gt, g, mods, mods)


def _resid_final_norm(x3, delta3, mods, part_gt, g, dims, tm=256):
    b, t, d = x3.shape
    lt = dims["L"] // tm
    blk = pl.BlockSpec((None, tm, d), lambda bi, i: (bi, i, 0))
    return pl.pallas_call(
        _resid_final_norm_kernel, grid=(b, lt),
        in_specs=[blk, blk, pl.BlockSpec((None, 1, d), lambda bi, i: (bi * 6 + part_gt, 0, 0)),
                  pl.BlockSpec((1, d), lambda bi, i: (0, 0))],
        out_specs=blk, out_shape=jax.ShapeDtypeStruct((b, dims["L"], d), F32),
        compiler_params=_cparams(("parallel", "parallel")),
    )(x3, delta3, mods, g)


def _segsum(x, bd):
    hi = x.astype(BF16)
    lo = (x - hi.astype(F32)).astype(BF16)
    return (jnp.dot(hi, bd, preferred_element_type=F32) + jnp.dot(lo, bd, preferred_element_type=F32))


def _rwfeat_kernel(p_ref, pp_ref, pn_ref, mu_ref, w0_ref, wup_ref, a0_ref, aup_ref, gup_ref, kk_ref, ka_ref,
                   rk_ref, bd_ref,
                   r_ref, nkk_ref, v_ref, dec0_ref, dec1_ref, b0_ref, b1_ref, ke0_ref, ke1_ref,
                   bon0_ref, bon1_ref, g0_ref, g1_ref, *, tm, lat_tiles, all_tiles):
    i = pl.program_id(1)
    p = p_ref[...]
    prev_ok = jnp.logical_and(i != 0, i != lat_tiles)
    next_ok = jnp.logical_and(i != lat_tiles - 1, i != all_tiles - 1)
    prow = jnp.where(prev_ok, pp_ref[SUBLANES - 1:SUBLANES, :], 0.0)
    nrow = jnp.where(next_ok, pn_ref[0:1, :], 0.0)
    rid = lax.broadcasted_iota(I32, (tm, 1), 0)
    prev = jnp.where(rid == 0, prow, pltpu.roll(p, 1, 0))
    nxt = jnp.where(rid == tm - 1, nrow, pltpu.roll(p, tm - 1, 0))
    ps = p + (0.5 * (prev + nxt) - p) * mu_ref[...]

    w_ = RW_WIDTH
    r = ps[:, 0:w_]
    k = ps[:, w_:2 * w_]
    v = ps[:, 2 * w_:3 * w_]
    wd = ps[:, 3 * w_:3 * w_ + LANES]
    ad = ps[:, 3 * w_ + LANES:3 * w_ + 2 * LANES]
    gd = ps[:, 3 * w_ + 2 * LANES:3 * w_ + 3 * LANES]
    bd = bd_ref[...]

    lw = jnp.dot(jnp.tanh(wd).astype(BF16), wup_ref[...], preferred_element_type=F32)
    la = jnp.dot(ad.astype(BF16), aup_ref[...], preferred_element_type=F32)
    lg = jnp.dot(_sigmoid(gd).astype(BF16), gup_ref[...], preferred_element_type=F32)

    kk = k * kk_ref[...]
    kk = kk * lax.rsqrt(_segsum(kk * kk, bd) + 1e-12)
    r_ref[...] = r
    nkk_ref[...] = -kk
    v_ref[...] = v
    rk = rk_ref[...]
    ka = ka_ref[...]
    outs = ((dec0_ref, b0_ref, ke0_ref, bon0_ref, g0_ref), (dec1_ref, b1_ref, ke1_ref, bon1_ref, g1_ref))
    for d in range(2):
        dec_ref, b_ref, ke_ref, bon_ref, g_ref = outs[d]
        z = w0_ref[d:d + 1, :] + lw[:, d * w_:(d + 1) * w_]
        nz = -z
        softplus = jnp.maximum(nz, 0.0) + jnp.log(1.0 + jnp.exp(-jnp.abs(nz)))
        wlog = -softplus - 0.5
        dec_ref[...] = jnp.exp(-jnp.exp(wlog))
        a = _sigmoid(a0_ref[d:d + 1, :] + la[:, d * w_:(d + 1) * w_])
        b_ref[...] = kk * a
        ke = k * (1.0 + (a - 1.0) * ka)
        ke_ref[...] = ke
        bon_ref[...] = _segsum(r * ke * rk, bd) * v
        g_ref[...] = lg[:, d * w_:(d + 1) * w_]


def _rwfeat(p3, mu, w0, wup, a0, aup, gup, k_k, k_a, r_k, bd, dims, tm=128):
    b, t, wc = p3.shape
    lt, at = dims["L"] // tm, t // tm
    hb = tm // SUBLANES
    nblk8 = t // SUBLANES
    w_ = RW_WIDTH
    full = lambda arr: pl.BlockSpec(arr.shape, lambda bi, i: (0,) * arr.ndim)
    out_blk = pl.BlockSpec((None, tm, w_), lambda bi, i: (bi, i, 0))
    n_out = 13
    return pl.pallas_call(
        functools.partial(_rwfeat_kernel, tm=tm, lat_tiles=lt, all_tiles=at),
        grid=(b, at),
        in_specs=[pl.BlockSpec((None, tm, wc), lambda bi, i: (bi, i, 0)),
                  pl.BlockSpec((None, SUBLANES, wc), lambda bi, i: (bi, jnp.maximum(i * hb - 1, 0), 0)),
                  pl.BlockSpec((None, SUBLANES, wc), lambda bi, i: (bi, jnp.minimum((i + 1) * hb, nblk8 - 1), 0)),
                  full(mu), full(w0), full(wup), full(a0), full(aup), full(gup), full(k_k), full(k_a),
                  full(r_k), full(bd)],
        out_specs=[out_blk] * n_out,
        out_shape=[jax.ShapeDtypeStruct((b, t, w_), F32)] * n_out,
        compiler_params=_cparams(("parallel", "parallel")),
    )(p3, p3, p3, mu, w0, wup, a0, aup, gup, k_k, k_a, r_k, bd)


SCAN_KH = RW_HEAD // 2


def _scan_kernel(kv_ref, v_ref, y_ref, s_ref, sa_ref, *, tt):
    @pl.when(pl.program_id(0) == 0)
    def _():
        s_ref[...] = jnp.zeros_like(s_ref)
        sa_ref[...] = jnp.zeros_like(sa_ref)

    lane = lax.broadcasted_iota(I32, (RW_HEAD, LANES), 1)

    def one_step(t, sa):
        vt = v_ref[t]
        yacc = jnp.zeros((RW_HEAD, LANES), F32)
        sn = jnp.zeros((RW_HEAD, LANES), F32)
        for k in range(SCAN_KH):
            w = kv_ref[t, 0, pl.ds(k, 1), :]
            b = kv_ref[t, 1, pl.ds(k, 1), :]
            ke = kv_ref[t, 2, pl.ds(k, 1), :]
            r = kv_ref[t, 3, pl.ds(k, 1), :]
            an = kv_ref[t, 4, pl.ds(k, 1), :]
            s = s_ref[k] * w + (sa * b + vt * ke)
            s_ref[k] = s
            yacc = yacc + s * r
            sn = sn + s * an
        y = yacc + pltpu.roll(yacc, LANES // 2, 1)
        sa_new = sn + pltpu.roll(sn, LANES // 2, 1)
        mean = jnp.mean(y, axis=0, keepdims=True)
        dlt = y - mean
        var = jnp.mean(dlt * dlt, axis=0, keepdims=True)
        return sa_new, dlt * lax.rsqrt(var + RW_GN_EPS)

    def body(j, sa):
        sa, y0 = one_step(2 * j, sa)
        sa, y1 = one_step(2 * j + 1, sa)
        y_ref[j] = jnp.where(lane < LANES // 2, y0, y1)
        return sa

    sa_ref[...] = lax.fori_loop(0, tt // 2, body, sa_ref[...])


def _scan(kv, vv, tt=32):
    t = kv.shape[0]
    assert t % tt == 0 and tt % 2 == 0
    return pl.pallas_call(
        functools.partial(_scan_kernel, tt=tt),
        grid=(t // tt,),
        in_specs=[pl.BlockSpec((tt, 5, SCAN_KH, LANES), lambda i: (i, 0, 0, 0)),
                  pl.BlockSpec((tt, RW_HEAD, LANES), lambda i: (i, 0, 0))],
        out_specs=pl.BlockSpec((tt // 2, RW_HEAD, LANES), lambda i: (i, 0, 0)),
        out_shape=jax.ShapeDtypeStruct((t // 2, RW_HEAD, LANES), F32),
        scratch_shapes=[pltpu.VMEM((SCAN_KH, RW_HEAD, LANES), F32), pltpu.VMEM((RW_HEAD, LANES), F32)],
        compiler_params=_cparams(("arbitrary",)),
    )(kv, vv)


def _seq_order(x, l, reverse):
    lat, ctx = x[:, :l], x[:, l:]
    if reverse:
        lat, ctx = jnp.flip(lat, 1), jnp.flip(ctx, 1)
    return jnp.concatenate([ctx, lat], 1)


def _seq_unorder(y, c, reverse):
    ctx, lat = y[:, :c], y[:, c:]
    if reverse:
        lat, ctx = jnp.flip(lat, 1), jnp.flip(ctx, 1)
    return jnp.concatenate([lat, ctx], 1)


def _to_scan_k(x0, x1, l):
    b, t, _ = x0.shape
    s = jnp.stack([_seq_order(x0, l, False), _seq_order(x1, l, True)], 0)
    s = s.reshape(2, b, t, RW_HEADS, 2, SCAN_KH)
    return s.transpose(2, 5, 4, 0, 1, 3).reshape(t, SCAN_KH, 4 * b * RW_HEADS)


def _to_scan_v(x, l):
    b, t, _ = x.shape
    s = jnp.stack([_seq_order(x, l, False), _seq_order(x, l, True)], 0)
    s = s.reshape(2, b, t, RW_HEADS, RW_HEAD).transpose(2, 4, 0, 1, 3).reshape(t, RW_HEAD, 2 * b * RW_HEADS)
    return jnp.concatenate([s, s], -1)


def _from_scan_y(y, b, c):
    t2 = y.shape[0]
    s = y.reshape(t2, RW_HEAD, 2, 2, b, RW_HEADS).transpose(3, 4, 0, 2, 5, 1).reshape(2, b, 2 * t2, RW_WIDTH)
    return _seq_unorder(s[0], c, False), _seq_unorder(s[1], c, True)


def _readout_kernel(y0_ref, y1_ref, bon0_ref, bon1_ref, g0_ref, g1_ref, lw_ref, lb_ref, o_ref):
    lw, lb = lw_ref[...], lb_ref[...]
    o = (y0_ref[...] * lw + lb + bon0_ref[...]) * g0_ref[...]
    o = o + (y1_ref[...] * lw + lb + bon1_ref[...]) * g1_ref[...]
    o_ref[...] = o.astype(o_ref.dtype)


def _readout(y0, y1, bon0, bon1, g0, g1, ln_w, ln_b, tm=256):
    m, w_ = y0.shape
    row = pl.BlockSpec((tm, w_), lambda i: (i, 0))
    vec = pl.BlockSpec((1, w_), lambda i: (0, 0))
    return pl.pallas_call(
        _readout_kernel, grid=(m // tm,), in_specs=[row] * 6 + [vec, vec], out_specs=row,
        out_shape=jax.ShapeDtypeStruct((m, w_), BF16), compiler_params=_cparams(("parallel",)),
    )(y0, y1, bon0, bon1, g0, g1, ln_w, ln_b)


def _mla_prep_kernel(p_ref, c_ref, s_ref, qn_ref, kn_ref, wqa_ref, wqs_ref, wk_ref, wv_ref, q_ref, k_ref, v_ref):
    p = p_ref[...]
    cos, sin = c_ref[...], s_ref[...]
    qc = _rms(p[:, :MLA_Q_RANK], qn_ref[...]).astype(BF16)
    kvc = _rms(p[:, MLA_Q_RANK:MLA_Q_RANK + MLA_KV_RANK], kn_ref[...]).astype(BF16)
    kr = p[:, MLA_Q_RANK + MLA_KV_RANK:MLA_Q_RANK + MLA_KV_RANK + LANES]
    krs = p[:, MLA_Q_RANK + MLA_KV_RANK + LANES:MLA_Q_RANK + MLA_KV_RANK + 2 * LANES]
    krope = kr * cos + krs * sin
    qa = jnp.dot(qc, wqa_ref[...], preferred_element_type=F32)
    qs = jnp.dot(qc, wqs_ref[...], preferred_element_type=F32)
    kk = jnp.dot(kvc, wk_ref[...], preferred_element_type=F32)
    v_ref[...] = jnp.dot(kvc, wv_ref[...], preferred_element_type=F32).astype(v_ref.dtype)
    scale = MLA_QK ** -0.5
    for h in range(MLA_HEADS):
        sl = slice(h * LANES, (h + 1) * LANES)
        q_ref[:, sl] = ((qa[:, sl] * cos + qs[:, sl] * sin) * scale).astype(q_ref.dtype)
        k_ref[:, sl] = (kk[:, sl] + krope).astype(k_ref.dtype)


def _mla_prep(p, cos_t, sin_t, q_norm, kv_norm, wqa, wqs, wk, wv, dims, tm=256):
    m, pc = p.shape
    tpb = dims["T"] // tm
    full = lambda arr: pl.BlockSpec(arr.shape, lambda i: (0,) * arr.ndim)
    tab = pl.BlockSpec((tm, LANES), lambda i: (i % tpb, 0))
    hw = MLA_HEADS * LANES
    return pl.pallas_call(
        _mla_prep_kernel, grid=(m // tm,),
        in_specs=[pl.BlockSpec((tm, pc), lambda i: (i, 0)), tab, tab, full(q_norm), full(kv_norm),
                  full(wqa), full(wqs), full(wk), full(wv)],
        out_specs=[pl.BlockSpec((tm, hw), lambda i: (i, 0)), pl.BlockSpec((tm, hw), lambda i: (i, 0)),
                   pl.BlockSpec((tm, MLA_HEADS * MLA_V), lambda i: (i, 0))],
        out_shape=[jax.ShapeDtypeStruct((m, hw), BF16), jax.ShapeDtypeStruct((m, hw), BF16),
                   jax.ShapeDtypeStruct((m, MLA_HEADS * MLA_V), BF16)],
        compiler_params=_cparams(("parallel",)),
    )(p, cos_t, sin_t, q_norm, kv_norm, wqa, wqs, wk, wv)


def _flash_kernel(q_ref, k_ref, v_ref, *rest, nk, aliased):
    o_ref, m_ref, l_ref, acc_ref = rest[-4:]
    ki = pl.program_id(3)

    @pl.when(ki == 0)
    def _():
        m_ref[...] = jnp.full_like(m_ref, -jnp.inf)
        l_ref[...] = jnp.zeros_like(l_ref)
        acc_ref[...] = jnp.zeros_like(acc_ref)

    v = v_ref[...]
    for hh in range(2):
        q = q_ref[:, hh * LANES:(hh + 1) * LANES]
        k = k_ref[:, hh * LANES:(hh + 1) * LANES]
        s = lax.dot_general(q, k, (((1,), (1,)), ((), ())), preferred_element_type=F32)
        m_prev = m_ref[hh]
        m_new = jnp.maximum(m_prev, jnp.max(s, -1, keepdims=True))
        alpha = jnp.exp(m_prev - m_new)
        pr = jnp.exp(s - m_new)
        l_ref[hh] = alpha * l_ref[hh] + jnp.sum(pr, -1, keepdims=True)
        acc_ref[hh] = alpha * acc_ref[hh] + jnp.dot(pr.astype(BF16), v, preferred_element_type=F32)
        m_ref[hh] = m_new

    @pl.when(ki == nk - 1)
    def _():
        lane = lax.broadcasted_iota(I32, acc_ref.shape[1:], 1)
        o0 = acc_ref[0] / l_ref[0]
        o1 = acc_ref[1] / l_ref[1]
        o_ref[...] = jnp.where(lane < MLA_V, o0, o1).astype(o_ref.dtype)


def _flash(q3, k3, v3, *, tq, tk, q_off, nq, k_off, nk, prev_out=None):
    b, t, _ = q3.shape
    hp = MLA_HEADS // 2
    in_specs = [pl.BlockSpec((None, tq, 2 * LANES), lambda bi, h, qi, ki: (bi, q_off + qi, h)),
                pl.BlockSpec((None, tk, 2 * LANES), lambda bi, h, qi, ki: (bi, k_off + ki, h)),
                pl.BlockSpec((None, tk, LANES), lambda bi, h, qi, ki: (bi, k_off + ki, h))]
    args = [q3, k3, v3]
    aliases = {}
    if prev_out is not None:
        in_specs.append(pl.BlockSpec(memory_space=pl.ANY))
        args.append(prev_out)
        aliases = {3: 0}
    return pl.pallas_call(
        functools.partial(_flash_kernel, nk=nk, aliased=prev_out is not None),
        grid=(b, hp, nq, nk),
        in_specs=in_specs,
        out_specs=pl.BlockSpec((None, tq, LANES), lambda bi, h, qi, ki: (bi, q_off + qi, h)),
        out_shape=jax.ShapeDtypeStruct((b, t, MLA_HEADS * MLA_V), BF16),
        scratch_shapes=[pltpu.VMEM((2, tq, 1), F32), pltpu.VMEM((2, tq, 1), F32), pltpu.VMEM((2, tq, LANES), F32)],
        input_output_aliases=aliases,
        compiler_params=_cparams(("parallel", "parallel", "parallel", "arbitrary")),
    )(*args)


def _topk_rows(s, kk, payload=None):
    nrow = s.shape[0]
    rid = lax.broadcasted_iota(I32, s.shape, 0)
    vals, sel = [], []
    for _ in range(kk):
        m = jnp.max(s, axis=0, keepdims=True)
        pos = jnp.min(jnp.where(s == m, rid, nrow), axis=0, keepdims=True)
        hit = rid == pos
        vals.append(m)
        sel.append(pos if payload is None else jnp.max(jnp.where(hit, payload, -1), axis=0, keepdims=True))
        s = jnp.where(hit, -jnp.inf, s)
    return jnp.concatenate(vals, 0), jnp.concatenate(sel, 0)


def _peer_topk_kernel(q0_ref, q1_ref, keys_ref, idx_ref, gate_ref):
    nt = (((1,), (1,)), ((), ()))
    s0 = lax.dot_general(keys_ref[0].astype(BF16), q0_ref[...].astype(BF16), nt, preferred_element_type=F32)
    s1 = lax.dot_general(keys_ref[1].astype(BF16), q1_ref[...].astype(BF16), nt, preferred_element_type=F32)
    v1, i1 = _topk_rows(s0, PEER_TOPK)
    v2, i2 = _topk_rows(s1, PEER_TOPK)
    cand_s = jnp.concatenate([v1[a:a + 1, :] + v2 for a in range(PEER_TOPK)], 0)
    cand_i = jnp.concatenate([i1[a:a + 1, :] * PEER_NKEYS + i2 for a in range(PEER_TOPK)], 0)
    top_s, top_i = _topk_rows(cand_s, PEER_TOPK, payload=cand_i)
    e = jnp.exp(top_s - top_s[0:1, :])
    idx_ref[...] = top_i
    gate_ref[...] = e / jnp.sum(e, axis=0, keepdims=True)


def _peer_topk(q, keys, tm=256):
    m = q.shape[0]
    out_blk = pl.BlockSpec((None, PEER_TOPK, tm), lambda i, h: (h, 0, i))
    return pl.pallas_call(
        _peer_topk_kernel, grid=(m // tm, PEER_HEADS),
        in_specs=[pl.BlockSpec((tm, PEER_HALF), lambda i, h: (i, 2 * h)),
                  pl.BlockSpec((tm, PEER_HALF), lambda i, h: (i, 2 * h + 1)),
                  pl.BlockSpec((None, 2, PEER_NKEYS, PEER_HALF), lambda i, h: (h, 0, 0, 0))],
        out_specs=[out_blk, out_blk],
        out_shape=[jax.ShapeDtypeStruct((PEER_HEADS, PEER_TOPK, m), I32),
                   jax.ShapeDtypeStruct((PEER_HEADS, PEER_TOPK, m), F32)],
        compiler_params=_cparams(("parallel", "parallel")),
    )(q, q, keys)


PEER_GROUP = 8
ROW_SUB = 16


def _gelu(x):
    return 0.5 * x * (1.0 + lax.erf(x * (2.0 ** -0.5)))


def _expert_kernel(sx_ref, off_ref, gs_ref, h_ref, u_ref, v_ref, oin_ref, o_ref, part_ref, *, tb, ch):
    c = pl.program_id(0)
    base = c * ch
    g8 = PEER_GROUP
    sub = lax.broadcasted_iota(I32, (g8, LANES), 0)

    @pl.when(jnp.logical_and(c == 0, pl.program_id(1) == 0))
    def _():
        part_ref[...] = jnp.zeros_like(part_ref)

    def seg(n):
        s = off_ref[n, c]
        e = off_ref[n, c + 1]
        return s, e, (e - s + g8 - 1) // g8

    def tok_a(n, slot):
        s, e, ng = seg(n)
        h = h_ref[n].astype(F32)

        def grp(g, slot):
            acc = jnp.zeros((g8, LANES), F32)
            for i in range(g8):
                p = jnp.minimum(s + g * g8 + i, e - 1)
                u = u_ref[sx_ref[n, p] - base]
                prod = h * u
                row = jnp.sum(prod[0:SUBLANES] + prod[SUBLANES:ROW_SUB], axis=0, keepdims=True)
                acc = jnp.where(sub == i, row, acc)
            part_ref[slot] = acc
            return slot + 1

        return lax.fori_loop(0, ng, grp, slot)

    nslot = lax.fori_loop(0, tb, tok_a, 0)

    def slot_b(j, carry):
        act = jnp.sum(part_ref[j], axis=-1, keepdims=True)
        part_ref[j] = jnp.broadcast_to(_gelu(act), (g8, LANES))
        return carry

    lax.fori_loop(0, nslot, slot_b, 0)

    def tok_c(n, slot):
        s, e, ng = seg(n)

        def grp(g, carry):
            slot, acc = carry
            w8 = part_ref[slot]
            for i in range(g8):
                p0 = s + g * g8 + i
                p = jnp.minimum(p0, e - 1)
                gate = jnp.where(p0 < e, gs_ref[n, p], 0.0)
                w = w8[i:i + 1, :] * gate
                acc = acc + w * v_ref[sx_ref[n, p] - base]
            return slot + 1, acc

        slot, acc = lax.fori_loop(0, ng, grp, (slot, jnp.zeros((ROW_SUB, LANES), F32)))
        o_ref[n] = oin_ref[n] + acc
        return slot

    lax.fori_loop(0, tb, tok_c, 0)


def _experts(sx, off, gs, h3, u3, v3, *, tb=64, ch=2048):
    m = sx.shape[0]
    e = u3.shape[0]
    nch = e // ch
    lanes = h3.shape[-1]
    assert lanes == LANES and e % ch == 0 and m % tb == 0
    smem = lambda w: pl.BlockSpec((tb, w), lambda c, t: (t, 0), memory_space=pltpu.SMEM)
    tok_blk = pl.BlockSpec((tb, ROW_SUB, LANES), lambda c, t: (t, 0, 0))
    tab_blk = pl.BlockSpec((ch, ROW_SUB, LANES), lambda c, t: (c, 0, 0), pipeline_mode=pl.Buffered(1))
    out0 = jnp.zeros((m, ROW_SUB, LANES), F32)
    return pl.pallas_call(
        functools.partial(_expert_kernel, tb=tb, ch=ch),
        grid=(nch, m // tb),
        in_specs=[smem(sx.shape[1]), smem(off.shape[1]), smem(gs.shape[1]), tok_blk, tab_blk, tab_blk, tok_blk],
        out_specs=tok_blk,
        out_shape=jax.ShapeDtypeStruct((m, ROW_SUB, LANES), F32),
        scratch_shapes=[pltpu.VMEM((tb * PEER_SEL // PEER_GROUP, PEER_GROUP, LANES), F32)],
        input_output_aliases={6: 0},
        compiler_params=_cparams(("arbitrary", "arbitrary")),
    )(sx, off, gs, h3, u3, v3, out0)


def _block_diag2(w):
    z = jnp.zeros_like(w[0])
    return jnp.concatenate([jnp.concatenate([w[0], z], 1), jnp.concatenate([z, w[1]], 1)], 0)


def _per_head_pad(w, width, pad_to):
    k = w.shape[0]
    w = w.reshape(k, -1, width)
    return jnp.pad(w, ((0, 0), (0, 0), (0, pad_to - width))).reshape(k, -1)


def _rope_swap_cols(w_rope):
    half = MLA_ROPE // 2
    return jnp.concatenate([-w_rope[..., half:], w_rope[..., :half]], -1)


def _prep_layer_weights(l, w_in, mla_w_uq, mla_w_ukv, rk_w_up, rk_a_up, rk_g_up):
    d = w_in.shape[1]
    wi = w_in[l]
    lo, hi = RW_COLS, RW_COLS + MLA_Q_RANK + MLA_KV_RANK + MLA_ROPE
    w_rw = jnp.pad(wi[:, :lo], ((0, 0), (0, RW_COLS_PAD - RW_COLS))).astype(BF16)
    w_kr = wi[:, hi - MLA_ROPE:hi]
    place = lambda w: jnp.pad(w, ((0, 0), (MLA_NOPE, LANES - MLA_NOPE - MLA_ROPE)))
    w_mla = jnp.concatenate([wi[:, lo:hi - MLA_ROPE], place(w_kr), place(_rope_swap_cols(w_kr))], 1).astype(BF16)
    w_gate = wi[:, hi:].astype(BF16)

    uq = mla_w_uq[l].reshape(MLA_Q_RANK, MLA_HEADS, MLA_QK)
    zeros = jnp.zeros((MLA_Q_RANK, MLA_HEADS, LANES - MLA_QK), F32)
    wqa = jnp.concatenate([uq, zeros], -1).reshape(MLA_Q_RANK, -1).astype(BF16)
    zn = jnp.zeros((MLA_Q_RANK, MLA_HEADS, MLA_NOPE), F32)
    wqs = jnp.concatenate([zn, _rope_swap_cols(uq[..., MLA_NOPE:]), zeros], -1).reshape(MLA_Q_RANK, -1).astype(BF16)
    ukv = mla_w_ukv[l].reshape(MLA_KV_RANK, MLA_HEADS, MLA_NOPE + MLA_V)
    wk = jnp.pad(ukv[..., :MLA_NOPE], ((0, 0), (0, 0), (0, LANES - MLA_NOPE))).reshape(MLA_KV_RANK, -1).astype(BF16)
    wv = ukv[..., MLA_NOPE:].reshape(MLA_KV_RANK, -1).astype(BF16)

    wup = _block_diag2(rk_w_up[l]).astype(BF16)
    aup = _block_diag2(rk_a_up[l]).astype(BF16)
    gup = jnp.pad(_block_diag2(rk_g_up[l]), ((0, LANES - 2 * RW_GATE_LORA), (0, 0))).astype(BF16)
    return dict(w_rw=w_rw, w_mla=w_mla, w_gate=w_gate, wqa=wqa, wqs=wqs, wk=wk, wv=wv, wup=wup, aup=aup, gup=gup)


def _rope_lane_tables(l, c):
    rows = l // GRID_W
    row = jnp.repeat(jnp.arange(rows, dtype=F32), GRID_W)
    col = jnp.tile(jnp.arange(GRID_W, dtype=F32), rows)
    n_freq = MLA_ROPE // 4
    freqs = ROPE_BASE ** (-jnp.arange(n_freq, dtype=F32) / n_freq)
    ang = jnp.concatenate([row[:, None] * freqs, col[:, None] * freqs], -1)
    cos, sin = jnp.cos(ang), jnp.sin(ang)
    tail = LANES - MLA_NOPE - MLA_ROPE
    cos_t = jnp.concatenate([jnp.ones((l, MLA_NOPE), F32), cos, cos, jnp.ones((l, tail), F32)], -1)
    sin_t = jnp.concatenate([jnp.zeros((l, MLA_NOPE), F32), sin, sin, jnp.zeros((l, tail), F32)], -1)
    cos_t = jnp.concatenate([cos_t, jnp.ones((c, LANES), F32)], 0)
    sin_t = jnp.concatenate([sin_t, jnp.zeros((c, LANES), F32)], 0)
    return cos_t, sin_t


def _peer(h2, w_q, keys, u_tab, v_tab, *, ch, tb):
    m, d = h2.shape
    q = _mm(h2, w_q, tm=256, tn=_pick_tile(w_q.shape[1], 1024))
    idx_t, gate_t = _peer_topk(q, keys)
    idx = idx_t.transpose(2, 0, 1).reshape(m, PEER_SEL)
    gate = gate_t.transpose(2, 0, 1).reshape(m, PEER_SEL)
    sx, gs = lax.sort((idx, gate), dimension=1, num_keys=1)
    n_exp = u_tab.shape[0]
    bounds = jnp.arange(n_exp // ch + 1, dtype=I32) * ch
    off = jnp.sum((sx[:, :, None] < bounds[None, None, :]).astype(I32), axis=1)
    lanes = d // ROW_SUB
    out = _experts(sx, off, gs, h2.reshape(m, ROW_SUB, lanes), u_tab.reshape(n_exp, ROW_SUB, lanes),
                   v_tab.reshape(n_exp, ROW_SUB, lanes), tb=tb, ch=ch)
    return out.reshape(m, d)


def _forward(x, c, ctx, c_ctx, ada_w, ada_b, norm_mix_g, w_in, gate_b, rk_mu, rk_w0, rk_w_up, rk_a0,
             rk_a_up, rk_g_up, rk_k_k, rk_k_a, rk_r_k, rk_ln_w, rk_ln_b, mla_q_norm, mla_w_uq, mla_kv_norm,
             mla_w_ukv, w_branch_a, w_branch_b, w_out, norm_ffn_g, peer_w_q, peer_keys, peer_u, peer_v,
             final_norm_g, *, flash_tq=512, flash_tk=768, scan_tt=32, peer_ch=2048, peer_tb=64):
    nb, l, d = x.shape
    cl = ctx.shape[1]
    t = l + cl
    m = nb * t
    depth = w_in.shape[0]
    dims = dict(B=nb, L=l, C=cl, T=t)
    assert nb == 2 and 4 * nb * RW_HEADS == LANES and l % 256 == 0 and cl % 256 == 0

    xa = jnp.concatenate([x, ctx], 1).reshape(m, d)
    cos_t, sin_t = _rope_lane_tables(l, cl)
    seg = jnp.arange(RW_WIDTH, dtype=I32) // RW_HEAD
    bd = (seg[:, None] == seg[None, :]).astype(BF16)
    cvec = jnp.zeros((SUBLANES, d), F32).at[:nb].set(jax.nn.silu(c)).at[nb].set(jax.nn.silu(c_ctx))

    delta, mods_prev = None, None
    out = None
    for li in range(depth):
        wl = _prep_layer_weights(li, w_in, mla_w_uq, mla_w_ukv, rk_w_up, rk_a_up, rk_g_up)
        tn_ada = _pick_tile(6 * d, 768)
        mods = _mm(cvec, ada_w[li], tm=SUBLANES, tn=tn_ada, epilogue=lambda acc, bias: acc + bias,
                   extras=[(ada_b[li][None, :], pl.BlockSpec((1, tn_ada), lambda j, i: (0, j)))])
        mods = mods[:nb + 1].reshape((nb + 1) * 6, 1, d)
        g_mix = norm_mix_g[li][None, :]
        if li == 0:
            h = _norm_mod(xa, g_mix, mods, 0, 1, dims)
        else:
            xa, h = _resid_norm_mod(xa, delta, mods_prev, 5, g_mix, mods, 0, 1, dims)

        p_rw = _mm(h, wl["w_rw"], tm=256, tn=_pick_tile(RW_COLS_PAD, 1152))
        p_mla = _mm(h, wl["w_mla"], tm=256, tn=MLA_P_COLS)
        p_gate = _mm(h, wl["w_gate"], tm=256, tn=1024)

        mu = jnp.pad(rk_mu[li], (0, RW_COLS_PAD - RW_COLS))[None, :]
        feats = _rwfeat(p_rw.reshape(nb, t, RW_COLS_PAD), mu, rk_w0[li], wl["wup"], rk_a0[li], wl["aup"], wl["gup"],
                        rk_k_k[li][None, :], rk_k_a[li][None, :], rk_r_k[li].reshape(1, RW_WIDTH), bd, dims)
        r, nkk, v, dec0, dec1, b0, b1, ke0, ke1, bon0, bon1, g0, g1 = feats
        a_next = _to_scan_k(nkk, nkk, l)
        a_next = jnp.concatenate([a_next[1:], jnp.zeros_like(a_next[:1])], 0)
        kv = jnp.stack([_to_scan_k(dec0, dec1, l), _to_scan_k(b0, b1, l), _to_scan_k(ke0, ke1, l),
                        _to_scan_k(r, r, l), a_next], 1)
        yn = _scan(kv, _to_scan_v(v, l), tt=scan_tt)
        yn0, yn1 = _from_scan_y(yn, nb, cl)
        flat = lambda z: z.reshape(m, RW_WIDTH)
        ya = _readout(flat(yn0), flat(yn1), flat(bon0), flat(bon1), flat(g0), flat(g1),
                      rk_ln_w[li][None, :], rk_ln_b[li][None, :])

        q, k, vv = _mla_prep(p_mla, cos_t, sin_t, mla_q_norm[li][None, :], mla_kv_norm[li][None, :],
                             wl["wqa"], wl["wqs"], wl["wk"], wl["wv"], dims)
        q3, k3, v3 = (z.reshape(nb, t, -1) for z in (q, k, vv))
        yb = _flash(q3, k3, v3, tq=flash_tq, tk=flash_tk, q_off=0, nq=l // flash_tq, k_off=0, nk=t // flash_tk)
        yb = _flash(q3, k3, v3, tq=cl, tk=cl, q_off=l // cl, nq=1, k_off=l // cl, nk=1, prev_out=yb)
        yb = yb.reshape(m, MLA_HEADS * MLA_V)

        tpb, lt = t // 256, l // 256
        br_a = _mm(ya, w_branch_a[li].astype(BF16), tm=256, tn=1024)
        gb = gate_b[li]
        merged = _mm(
            yb, w_branch_b[li].astype(BF16), tm=256, tn=1024, out_dtype=BF16,
            epilogue=lambda acc, ga, gbb, ba, bb, a: _sigmoid(ga + ba) * a + _sigmoid(gbb + bb) * acc,
            extras=[(p_gate, pl.BlockSpec((256, 1024), lambda j, i: (i, j))),
                    (p_gate, pl.BlockSpec((256, 1024), lambda j, i: (i, j + d // 1024))),
                    (gb[0][None, :], pl.BlockSpec((1, 1024), lambda j, i: (0, j))),
                    (gb[1][None, :], pl.BlockSpec((1, 1024), lambda j, i: (0, j))),
                    (br_a, pl.BlockSpec((256, 1024), lambda j, i: (i, j)))])
        x1 = _mm(merged, w_out[li].astype(BF16), tm=256, tn=1024,
                 epilogue=lambda acc, xr, gt: xr + gt * acc,
                 extras=[(xa, pl.BlockSpec((256, 1024), lambda j, i: (i, j))),
                         (mods, pl.BlockSpec((None, 1, 1024), lambda j, i: (_mod_row(i, tpb, lt, 2, nb), 0, j)))])

        h2 = _norm_mod(x1, norm_ffn_g[li][None, :], mods, 3, 4, dims)
        delta = _peer(h2, peer_w_q[li].astype(BF16), peer_keys[li], peer_u[li], peer_v[li], ch=peer_ch, tb=peer_tb)
        xa, mods_prev = x1, mods

    out = _resid_final_norm(xa.reshape(nb, t, d), delta.reshape(nb, t, d), mods_prev, 5,
                            final_norm_g[None, :], dims)
    return out


def kernel(x, c, ctx, c_ctx, ada_w, ada_b, norm_mix_g, w_in, gate_b, rk_mu, rk_w0, rk_w_up, rk_a0, rk_a_up, rk_g_up, rk_k_k, rk_k_a, rk_r_k, rk_ln_w, rk_ln_b, mla_q_norm, mla_w_uq, mla_kv_norm, mla_w_ukv, w_branch_a, w_branch_b, w_out, norm_ffn_g, peer_w_q, peer_keys, peer_u, peer_v, final_norm_g):
    return _forward(x, c, ctx, c_ctx, ada_w, ada_b, norm_mix_g, w_in, gate_b, rk_mu, rk_w0, rk_w_up, rk_a0,
                    rk_a_up, rk_g_up, rk_k_k, rk_k_a, rk_r_k, rk_ln_w, rk_ln_b, mla_q_norm, mla_w_uq,
                    mla_kv_norm, mla_w_ukv, w_branch_a, w_branch_b, w_out, norm_ffn_g, peer_w_q, peer_keys,
                    peer_u, peer_v, final_norm_g)
```

```python
import functools
import math

import jax
import jax.numpy as jnp
from jax import lax
from jax.experimental import pallas as pl
from jax.experimental.pallas import tpu as pltpu

F32 = jnp.float32
BF16 = jnp.bfloat16
I32 = jnp.int32

NORM_EPS = 1e-6
GRID_W = 64
ROPE_BASE = 10000.0

RW_HEADS = 16
RW_HEAD = 64
RW_WIDTH = RW_HEADS * RW_HEAD
RW_DECAY_LORA = 64
RW_A_LORA = 64
RW_GATE_LORA = 32
RW_GN_EPS = 64e-5
RW_COLS = 3 * RW_WIDTH + 2 * (RW_DECAY_LORA + RW_A_LORA + RW_GATE_LORA)
RW_COLS_PAD = 3456

MLA_HEADS = 16
MLA_NOPE = 64
MLA_ROPE = 32
MLA_QK = MLA_NOPE + MLA_ROPE
MLA_V = 64
MLA_Q_RANK = 512
MLA_KV_RANK = 256
MLA_HEAD_PAD = 128
MLA_P_COLS = MLA_Q_RANK + MLA_KV_RANK + 2 * MLA_HEAD_PAD

PEER_HEADS = 8
PEER_NKEYS = 128
PEER_HALF = 128
PEER_TOPK = 16
PEER_SEL = PEER_HEADS * PEER_TOPK

LANES = 128
SUBLANES = 8
VMEM_LIMIT = 56 * 1024 * 1024


def _cparams(sem):
    return pltpu.CompilerParams(dimension_semantics=sem, vmem_limit_bytes=VMEM_LIMIT)


def _pick_tile(n, cap, mult=LANES):
    best = None
    t = mult
    while t <= min(n, cap):
        if n % t == 0:
            best = t
        t += mult
    assert best is not None, (n, cap)
    return best


def _mm_kernel(*refs, n_extra, epilogue):
    a_ref, w_ref = refs[0], refs[1]
    extra = refs[2:2 + n_extra]
    o_ref = refs[2 + n_extra]
    acc = jnp.dot(a_ref[...].astype(BF16), w_ref[...].astype(BF16), preferred_element_type=F32)
    if epilogue is not None:
        acc = epilogue(acc, *[e[...] for e in extra])
    o_ref[...] = acc.astype(o_ref.dtype)


def _mm(a, w, *, tm, tn, out_dtype=F32, epilogue=None, extras=()):
    m, k = a.shape
    k2, n = w.shape
    assert k == k2 and m % tm == 0 and n % tn == 0, (a.shape, w.shape, tm, tn)
    in_specs = [pl.BlockSpec((tm, k), lambda j, i: (i, 0)),
                pl.BlockSpec((k, tn), lambda j, i: (0, j))] + [s for _, s in extras]
    return pl.pallas_call(
        functools.partial(_mm_kernel, n_extra=len(extras), epilogue=epilogue),
        grid=(n // tn, m // tm),
        in_specs=in_specs,
        out_specs=pl.BlockSpec((tm, tn), lambda j, i: (i, j)),
        out_shape=jax.ShapeDtypeStruct((m, n), out_dtype),
        compiler_params=_cparams(("parallel", "parallel")),
    )(a, w, *[x for x, _ in extras])


def _sigmoid(x):
    return 1.0 / (1.0 + jnp.exp(-x))


def _mod_row(i, tiles_per_batch, lat_tiles, part, n_batch):
    which = jnp.where(i % tiles_per_batch >= lat_tiles, n_batch, i // tiles_per_batch)
    return which * 6 + part


def _rms(x, g):
    return x * lax.rsqrt(jnp.mean(x * x, -1, keepdims=True) + NORM_EPS) * g


def _norm_mod_kernel(x_ref, g_ref, sh_ref, sc_ref, h_ref):
    y = _rms(x_ref[...], g_ref[...])
    h_ref[...] = (y * (1.0 + sc_ref[...]) + sh_ref[...]).astype(h_ref.dtype)


def _resid_norm_mod_kernel(x_ref, d_ref, gt_ref, g_ref, sh_ref, sc_ref, xo_ref, h_ref):
    x = x_ref[...] + gt_ref[...] * d_ref[...]
    xo_ref[...] = x
    y = _rms(x, g_ref[...])
    h_ref[...] = (y * (1.0 + sc_ref[...]) + sh_ref[...]).astype(h_ref.dtype)


def _resid_final_norm_kernel(x_ref, d_ref, gt_ref, g_ref, o_ref):
    x = x_ref[...] + gt_ref[...] * d_ref[...]
    o_ref[...] = _rms(x, g_ref[...])


def _mod_spec(d, part, dims, tm):
    tpb, lt, nb = dims["T"] // tm, dims["L"] // tm, dims["B"]
    return pl.BlockSpec((None, 1, d), lambda i: (_mod_row(i, tpb, lt, part, nb), 0, 0))


def _norm_mod(x, g, mods, part_sh, part_sc, dims, tm=256):
    m, d = x.shape
    row = pl.BlockSpec((tm, d), lambda i: (i, 0))
    return pl.pallas_call(
        _norm_mod_kernel, grid=(m // tm,),
        in_specs=[row, pl.BlockSpec((1, d), lambda i: (0, 0)),
                  _mod_spec(d, part_sh, dims, tm), _mod_spec(d, part_sc, dims, tm)],
        out_specs=row, out_shape=jax.ShapeDtypeStruct((m, d), BF16),
        compiler_params=_cparams(("parallel",)),
    )(x, g, mods, mods)


def _resid_norm_mod(x, delta, mods_gt, part_gt, g, mods, part_sh, part_sc, dims, tm=256):
    m, d = x.shape
    row = pl.BlockSpec((tm, d), lambda i: (i, 0))
    return pl.pallas_call(
        _resid_norm_mod_kernel, grid=(m // tm,),
        in_specs=[row, row, _mod_spec(d, part_gt, dims, tm), pl.BlockSpec((1, d), lambda i: (0, 0)),
                  _mod_spec(d, part_sh, dims, tm), _mod_spec(d, part_sc, dims, tm)],
        out_specs=[row, row],
        out_shape=[jax.ShapeDtypeStruct((m, d), F32), jax.ShapeDtypeStruct((m, d), BF16)],
        compiler_params=_cparams(("parallel",)),
    )(x, delta, mods_gt, g, mods, mods)


def _resid_final_norm(x3, delta3, mods, part_gt, g, dims, tm=256):
    b, t, d = x3.shape
    lt = dims["L"] // tm
    blk = pl.BlockSpec((None, tm, d), lambda bi, i: (bi, i, 0))
    return pl.pallas_call(
        _resid_final_norm_kernel, grid=(b, lt),
        in_specs=[blk, blk, pl.BlockSpec((None, 1, d), lambda bi, i: (bi * 6 + part_gt, 0, 0)),
                  pl.BlockSpec((1, d), lambda bi, i: (0, 0))],
        out_specs=blk, out_shape=jax.ShapeDtypeStruct((b, dims["L"], d), F32),
        compiler_params=_cparams(("parallel", "parallel")),
    )(x3, delta3, mods, g)


def _segsum(x, bd):
    hi = x.astype(BF16)
    lo = (x - hi.astype(F32)).astype(BF16)
    return (jnp.dot(hi, bd, preferred_element_type=F32) + jnp.dot(lo, bd, preferred_element_type=F32))


def _rwfeat_kernel(p_ref, pp_ref, pn_ref, mu_ref, w0_ref, wup_ref, a0_ref, aup_ref, gup_ref, kk_ref, ka_ref,
                   rk_ref, bd_ref,
                   r_ref, nkk_ref, v_ref, dec0_ref, dec1_ref, b0_ref, b1_ref, ke0_ref, ke1_ref,
                   bon0_ref, bon1_ref, g0_ref, g1_ref, *, tm, lat_tiles, all_tiles):
    i = pl.program_id(1)
    p = p_ref[...]
    prev_ok = jnp.logical_and(i != 0, i != lat_tiles)
    next_ok = jnp.logical_and(i != lat_tiles - 1, i != all_tiles - 1)
    prow = jnp.where(prev_ok, pp_ref[SUBLANES - 1:SUBLANES, :], 0.0)
    nrow = jnp.where(next_ok, pn_ref[0:1, :], 0.0)
    rid = lax.broadcasted_iota(I32, (tm, 1), 0)
    prev = jnp.where(rid == 0, prow, pltpu.roll(p, 1, 0))
    nxt = jnp.where(rid == tm - 1, nrow, pltpu.roll(p, tm - 1, 0))
    ps = p + (0.5 * (prev + nxt) - p) * mu_ref[...]

    w_ = RW_WIDTH
    r = ps[:, 0:w_]
    k = ps[:, w_:2 * w_]
    v = ps[:, 2 * w_:3 * w_]
    wd = ps[:, 3 * w_:3 * w_ + LANES]
    ad = ps[:, 3 * w_ + LANES:3 * w_ + 2 * LANES]
    gd = ps[:, 3 * w_ + 2 * LANES:3 * w_ + 3 * LANES]
    bd = bd_ref[...]

    lw = jnp.dot(jnp.tanh(wd).astype(BF16), wup_ref[...], preferred_element_type=F32)
    la = jnp.dot(ad.astype(BF16), aup_ref[...], preferred_element_type=F32)
    lg = jnp.dot(_sigmoid(gd).astype(BF16), gup_ref[...], preferred_element_type=F32)

    kk = k * kk_ref[...]
    kk = kk * lax.rsqrt(_segsum(kk * kk, bd) + 1e-12)
    r_ref[...] = r
    nkk_ref[...] = -kk
    v_ref[...] = v
    rk = rk_ref[...]
    ka = ka_ref[...]
    outs = ((dec0_ref, b0_ref, ke0_ref, bon0_ref, g0_ref), (dec1_ref, b1_ref, ke1_ref, bon1_ref, g1_ref))
    for d in range(2):
        dec_ref, b_ref, ke_ref, bon_ref, g_ref = outs[d]
        z = w0_ref[d:d + 1, :] + lw[:, d * w_:(d + 1) * w_]
        nz = -z
        softplus = jnp.maximum(nz, 0.0) + jnp.log(1.0 + jnp.exp(-jnp.abs(nz)))
        wlog = -softplus - 0.5
        dec_ref[...] = jnp.exp(-jnp.exp(wlog))
        a = _sigmoid(a0_ref[d:d + 1, :] + la[:, d * w_:(d + 1) * w_])
        b_ref[...] = kk * a
        ke = k * (1.0 + (a - 1.0) * ka)
        ke_ref[...] = ke
        bon_ref[...] = _segsum(r * ke * rk, bd) * v
        g_ref[...] = lg[:, d * w_:(d + 1) * w_]


def _rwfeat(p3, mu, w0, wup, a0, aup, gup, k_k, k_a, r_k, bd, dims, tm=128):
    b, t, wc = p3.shape
    lt, at = dims["L"] // tm, t // tm
    hb = tm // SUBLANES
    nblk8 = t // SUBLANES
    w_ = RW_WIDTH
    full = lambda arr: pl.BlockSpec(arr.shape, lambda bi, i: (0,) * arr.ndim)
    out_blk = pl.BlockSpec((None, tm, w_), lambda bi, i: (bi, i, 0))
    n_out = 13
    return pl.pallas_call(
        functools.partial(_rwfeat_kernel, tm=tm, lat_tiles=lt, all_tiles=at),
        grid=(b, at),
        in_specs=[pl.BlockSpec((None, tm, wc), lambda bi, i: (bi, i, 0)),
                  pl.BlockSpec((None, SUBLANES, wc), lambda bi, i: (bi, jnp.maximum(i * hb - 1, 0), 0)),
                  pl.BlockSpec((None, SUBLANES, wc), lambda bi, i: (bi, jnp.minimum((i + 1) * hb, nblk8 - 1), 0)),
                  full(mu), full(w0), full(wup), full(a0), full(aup), full(gup), full(k_k), full(k_a),
                  full(r_k), full(bd)],
        out_specs=[out_blk] * n_out,
        out_shape=[jax.ShapeDtypeStruct((b, t, w_), F32)] * n_out,
        compiler_params=_cparams(("parallel", "parallel")),
    )(p3, p3, p3, mu, w0, wup, a0, aup, gup, k_k, k_a, r_k, bd)


SCAN_KH = RW_HEAD // 2


def _scan_kernel(kv_ref, v_ref, y_ref, s_ref, sa_ref, *, tt):
    @pl.when(pl.program_id(0) == 0)
    def _():
        s_ref[...] = jnp.zeros_like(s_ref)
        sa_ref[...] = jnp.zeros_like(sa_ref)

    lane = lax.broadcasted_iota(I32, (RW_HEAD, LANES), 1)

    def one_step(t, sa):
        vt = v_ref[t]
        yacc = jnp.zeros((RW_HEAD, LANES), F32)
        sn = jnp.zeros((RW_HEAD, LANES), F32)
        for k in range(SCAN_KH):
            w = kv_ref[t, 0, pl.ds(k, 1), :]
            b = kv_ref[t, 1, pl.ds(k, 1), :]
            ke = kv_ref[t, 2, pl.ds(k, 1), :]
            r = kv_ref[t, 3, pl.ds(k, 1), :]
            an = kv_ref[t, 4, pl.ds(k, 1), :]
            s = s_ref[k] * w + (sa * b + vt * ke)
            s_ref[k] = s
            yacc = yacc + s * r
            sn = sn + s * an
        y = yacc + pltpu.roll(yacc, LANES // 2, 1)
        sa_new = sn + pltpu.roll(sn, LANES // 2, 1)
        mean = jnp.mean(y, axis=0, keepdims=True)
        dlt = y - mean
        var = jnp.mean(dlt * dlt, axis=0, keepdims=True)
        return sa_new, dlt * lax.rsqrt(var + RW_GN_EPS)

    def body(j, sa):
        sa, y0 = one_step(2 * j, sa)
        sa, y1 = one_step(2 * j + 1, sa)
        y_ref[j] = jnp.where(lane < LANES // 2, y0, y1)
        return sa

    sa_ref[...] = lax.fori_loop(0, tt // 2, body, sa_ref[...])


def _scan(kv, vv, tt=32):
    t = kv.shape[0]
    assert t % tt == 0 and tt % 2 == 0
    return pl.pallas_call(
        functools.partial(_scan_kernel, tt=tt),
        grid=(t // tt,),
        in_specs=[pl.BlockSpec((tt, 5, SCAN_KH, LANES), lambda i: (i, 0, 0, 0)),
                  pl.BlockSpec((tt, RW_HEAD, LANES), lambda i: (i, 0, 0))],
        out_specs=pl.BlockSpec((tt // 2, RW_HEAD, LANES), lambda i: (i, 0, 0)),
        out_shape=jax.ShapeDtypeStruct((t // 2, RW_HEAD, LANES), F32),
        scratch_shapes=[pltpu.VMEM((SCAN_KH, RW_HEAD, LANES), F32), pltpu.VMEM((RW_HEAD, LANES), F32)],
        compiler_params=_cparams(("arbitrary",)),
    )(kv, vv)


def _seq_order(x, l, reverse):
    lat, ctx = x[:, :l], x[:, l:]
    if reverse:
        lat, ctx = jnp.flip(lat, 1), jnp.flip(ctx, 1)
    return jnp.concatenate([ctx, lat], 1)


def _seq_unorder(y, c, reverse):
    ctx, lat = y[:, :c], y[:, c:]
    if reverse:
        lat, ctx = jnp.flip(lat, 1), jnp.flip(ctx, 1)
    return jnp.concatenate([lat, ctx], 1)


def _to_scan_k(x0, x1, l):
    b, t, _ = x0.shape
    s = jnp.stack([_seq_order(x0, l, False), _seq_order(x1, l, True)], 0)
    s = s.reshape(2, b, t, RW_HEADS, 2, SCAN_KH)
    return s.transpose(2, 5, 4, 0, 1, 3).reshape(t, SCAN_KH, 4 * b * RW_HEADS)


def _to_scan_v(x, l):
    b, t, _ = x.shape
    s = jnp.stack([_seq_order(x, l, False), _seq_order(x, l, True)], 0)
    s = s.reshape(2, b, t, RW_HEADS, RW_HEAD).transpose(2, 4, 0, 1, 3).reshape(t, RW_HEAD, 2 * b * RW_HEADS)
    return jnp.concatenate([s, s], -1)


def _from_scan_y(y, b, c):
    t2 = y.shape[0]
    s = y.reshape(t2, RW_HEAD, 2, 2, b, RW_HEADS).transpose(3, 4, 0, 2, 5, 1).reshape(2, b, 2 * t2, RW_WIDTH)
    return _seq_unorder(s[0], c, False), _seq_unorder(s[1], c, True)


def _readout_kernel(y0_ref, y1_ref, bon0_ref, bon1_ref, g0_ref, g1_ref, lw_ref, lb_ref, o_ref):
    lw, lb = lw_ref[...], lb_ref[...]
    o = (y0_ref[...] * lw + lb + bon0_ref[...]) * g0_ref[...]
    o = o + (y1_ref[...] * lw + lb + bon1_ref[...]) * g1_ref[...]
    o_ref[...] = o.astype(o_ref.dtype)


def _readout(y0, y1, bon0, bon1, g0, g1, ln_w, ln_b, tm=256):
    m, w_ = y0.shape
    row = pl.BlockSpec((tm, w_), lambda i: (i, 0))
    vec = pl.BlockSpec((1, w_), lambda i: (0, 0))
    return pl.pallas_call(
        _readout_kernel, grid=(m // tm,), in_specs=[row] * 6 + [vec, vec], out_specs=row,
        out_shape=jax.ShapeDtypeStruct((m, w_), BF16), compiler_params=_cparams(("parallel",)),
    )(y0, y1, bon0, bon1, g0, g1, ln_w, ln_b)


def _mla_prep_kernel(p_ref, c_ref, s_ref, qn_ref, kn_ref, wqa_ref, wqs_ref, wk_ref, wv_ref, q_ref, k_ref, v_ref):
    p = p_ref[...]
    cos, sin = c_ref[...], s_ref[...]
    qc = _rms(p[:, :MLA_Q_RANK], qn_ref[...]).astype(BF16)
    kvc = _rms(p[:, MLA_Q_RANK:MLA_Q_RANK + MLA_KV_RANK], kn_ref[...]).astype(BF16)
    kr = p[:, MLA_Q_RANK + MLA_KV_RANK:MLA_Q_RANK + MLA_KV_RANK + LANES]
    krs = p[:, MLA_Q_RANK + MLA_KV_RANK + LANES:MLA_Q_RANK + MLA_KV_RANK + 2 * LANES]
    krope = kr * cos + krs * sin
    qa = jnp.dot(qc, wqa_ref[...], preferred_element_type=F32)
    qs = jnp.dot(qc, wqs_ref[...], preferred_element_type=F32)
    kk = jnp.dot(kvc, wk_ref[...], preferred_element_type=F32)
    v_ref[...] = jnp.dot(kvc, wv_ref[...], preferred_element_type=F32).astype(v_ref.dtype)
    scale = MLA_QK ** -0.5
    for h in range(MLA_HEADS):
        sl = slice(h * LANES, (h + 1) * LANES)
        q_ref[:, sl] = ((qa[:, sl] * cos + qs[:, sl] * sin) * scale).astype(q_ref.dtype)
        k_ref[:, sl] = (kk[:, sl] + krope).astype(k_ref.dtype)


def _mla_prep(p, cos_t, sin_t, q_norm, kv_norm, wqa, wqs, wk, wv, dims, tm=256):
    m, pc = p.shape
    tpb = dims["T"] // tm
    full = lambda arr: pl.BlockSpec(arr.shape, lambda i: (0,) * arr.ndim)
    tab = pl.BlockSpec((tm, LANES), lambda i: (i % tpb, 0))
    hw = MLA_HEADS * LANES
    return pl.pallas_call(
        _mla_prep_kernel, grid=(m // tm,),
        in_specs=[pl.BlockSpec((tm, pc), lambda i: (i, 0)), tab, tab, full(q_norm), full(kv_norm),
                  full(wqa), full(wqs), full(wk), full(wv)],
        out_specs=[pl.BlockSpec((tm, hw), lambda i: (i, 0)), pl.BlockSpec((tm, hw), lambda i: (i, 0)),
                   pl.BlockSpec((tm, MLA_HEADS * MLA_V), lambda i: (i, 0))],
        out_shape=[jax.ShapeDtypeStruct((m, hw), BF16), jax.ShapeDtypeStruct((m, hw), BF16),
                   jax.ShapeDtypeStruct((m, MLA_HEADS * MLA_V), BF16)],
        compiler_params=_cparams(("parallel",)),
    )(p, cos_t, sin_t, q_norm, kv_norm, wqa, wqs, wk, wv)


def _flash_kernel(q_ref, k_ref, v_ref, *rest, nk, aliased):
    o_ref, m_ref, l_ref, acc_ref = rest[-4:]
    ki = pl.program_id(3)

    @pl.when(ki == 0)
    def _():
        m_ref[...] = jnp.full_like(m_ref, -jnp.inf)
        l_ref[...] = jnp.zeros_like(l_ref)
        acc_ref[...] = jnp.zeros_like(acc_ref)

    v = v_ref[...]
    for hh in range(2):
        q = q_ref[:, hh * LANES:(hh + 1) * LANES]
        k = k_ref[:, hh * LANES:(hh + 1) * LANES]
        s = lax.dot_general(q, k, (((1,), (1,)), ((), ())), preferred_element_type=F32)
        m_prev = m_ref[hh]
        m_new = jnp.maximum(m_prev, jnp.max(s, -1, keepdims=True))
        alpha = jnp.exp(m_prev - m_new)
        pr = jnp.exp(s - m_new)
        l_ref[hh] = alpha * l_ref[hh] + jnp.sum(pr, -1, keepdims=True)
        acc_ref[hh] = alpha * acc_ref[hh] + jnp.dot(pr.astype(BF16), v, preferred_element_type=F32)
        m_ref[hh] = m_new

    @pl.when(ki == nk - 1)
    def _():
        lane = lax.broadcasted_iota(I32, acc_ref.shape[1:], 1)
        o0 = acc_ref[0] / l_ref[0]
        o1 = acc_ref[1] / l_ref[1]
        o_ref[...] = jnp.where(lane < MLA_V, o0, o1).astype(o_ref.dtype)


def _flash(q3, k3, v3, *, tq, tk, q_off, nq, k_off, nk, prev_out=None):
    b, t, _ = q3.shape
    hp = MLA_HEADS // 2
    in_specs = [pl.BlockSpec((None, tq, 2 * LANES), lambda bi, h, qi, ki: (bi, q_off + qi, h)),
                pl.BlockSpec((None, tk, 2 * LANES), lambda bi, h, qi, ki: (bi, k_off + ki, h)),
                pl.BlockSpec((None, tk, LANES), lambda bi, h, qi, ki: (bi, k_off + ki, h))]
    args = [q3, k3, v3]
    aliases = {}
    if prev_out is not None:
        in_specs.append(pl.BlockSpec(memory_space=pl.ANY))
        args.append(prev_out)
        aliases = {3: 0}
    return pl.pallas_call(
        functools.partial(_flash_kernel, nk=nk, aliased=prev_out is not None),
        grid=(b, hp, nq, nk),
        in_specs=in_specs,
        out_specs=pl.BlockSpec((None, tq, LANES), lambda bi, h, qi, ki: (bi, q_off + qi, h)),
        out_shape=jax.ShapeDtypeStruct((b, t, MLA_HEADS * MLA_V), BF16),
        scratch_shapes=[pltpu.VMEM((2, tq, 1), F32), pltpu.VMEM((2, tq, 1), F32), pltpu.VMEM((2, tq, LANES), F32)],
        input_output_aliases=aliases,
        compiler_params=_cparams(("parallel", "parallel", "parallel", "arbitrary")),
    )(*args)


def _topk_rows(s, kk, payload=None):
    nrow = s.shape[0]
    rid = lax.broadcasted_iota(I32, s.shape, 0)
    vals, sel = [], []
    for _ in range(kk):
        m = jnp.max(s, axis=0, keepdims=True)
        pos = jnp.min(jnp.where(s == m, rid, nrow), axis=0, keepdims=True)
        hit = rid == pos
        vals.append(m)
        sel.append(pos if payload is None else jnp.max(jnp.where(hit, payload, -1), axis=0, keepdims=True))
        s = jnp.where(hit, -jnp.inf, s)
    return jnp.concatenate(vals, 0), jnp.concatenate(sel, 0)


def _peer_topk_kernel(q0_ref, q1_ref, keys_ref, idx_ref, gate_ref):
    nt = (((1,), (1,)), ((), ()))
    s0 = lax.dot_general(keys_ref[0].astype(BF16), q0_ref[...].astype(BF16), nt, preferred_element_type=F32)
    s1 = lax.dot_general(keys_ref[1].astype(BF16), q1_ref[...].astype(BF16), nt, preferred_element_type=F32)
    v1, i1 = _topk_rows(s0, PEER_TOPK)
    v2, i2 = _topk_rows(s1, PEER_TOPK)
    cand_s = jnp.concatenate([v1[a:a + 1, :] + v2 for a in range(PEER_TOPK)], 0)
    cand_i = jnp.concatenate([i1[a:a + 1, :] * PEER_NKEYS + i2 for a in range(PEER_TOPK)], 0)
    top_s, top_i = _topk_rows(cand_s, PEER_TOPK, payload=cand_i)
    e = jnp.exp(top_s - top_s[0:1, :])
    idx_ref[...] = top_i
    gate_ref[...] = e / jnp.sum(e, axis=0, keepdims=True)


def _peer_topk(q, keys, tm=256):
    m = q.shape[0]
    out_blk = pl.BlockSpec((None, PEER_TOPK, tm), lambda i, h: (h, 0, i))
    return pl.pallas_call(
        _peer_topk_kernel, grid=(m // tm, PEER_HEADS),
        in_specs=[pl.BlockSpec((tm, PEER_HALF), lambda i, h: (i, 2 * h)),
                  pl.BlockSpec((tm, PEER_HALF), lambda i, h: (i, 2 * h + 1)),
                  pl.BlockSpec((None, 2, PEER_NKEYS, PEER_HALF), lambda i, h: (h, 0, 0, 0))],
        out_specs=[out_blk, out_blk],
        out_shape=[jax.ShapeDtypeStruct((PEER_HEADS, PEER_TOPK, m), I32),
                   jax.ShapeDtypeStruct((PEER_HEADS, PEER_TOPK, m), F32)],
        compiler_params=_cparams(("parallel", "parallel")),
    )(q, q, keys)


PEER_GROUP = 8
ROW_SUB = 16
PLAN_OFF_W = 16
PLAN_BATCH = 16
BITREV8 = (0, 4, 2, 6, 1, 5, 3, 7)


def _gelu(x):
    return 0.5 * x * (1.0 + lax.erf(x * (2.0 ** -0.5)))


def _fold_pair(xa, xb, k, sub):
    mask = (sub & (2 * k - 1)) < k
    a = jnp.where(mask, xa, xb)
    if 2 * k == SUBLANES:
        return a + pltpu.roll(jnp.where(mask, xb, xa), k, 0)
    return a + jnp.where(mask, pltpu.roll(xa, SUBLANES - k, 0), pltpu.roll(xb, k, 0))


def _expert_kernel(rows_ref, gates_ref, goff_ref, h_ref, u_ref, v_ref, oin_ref, o_ref, part_ref, *, tb, gpt):
    c = pl.program_id(0)
    g8 = PEER_GROUP
    sub = lax.broadcasted_iota(I32, (g8, LANES), 0)

    @pl.when(jnp.logical_and(c == 0, pl.program_id(1) == 0))
    def _():
        part_ref[...] = jnp.zeros_like(part_ref)

    def groups(n):
        return goff_ref[n * PLAN_OFF_W + c], goff_ref[n * PLAN_OFF_W + c + 1]

    def tok_a(n, slot):
        g0, g1 = groups(n)
        h = h_ref[n].astype(F32)

        def grp(g, slot):
            base = (n * gpt + g) * g8
            xs = []
            for i in range(g8):
                prod = h * u_ref[rows_ref[base + i]].astype(F32)
                xs.append(prod[0:SUBLANES] + prod[SUBLANES:ROW_SUB])
            xs = [xs[BITREV8[j]] for j in range(g8)]
            k = g8 // 2
            while k >= 1:
                xs = [_fold_pair(xs[2 * j], xs[2 * j + 1], k, sub) for j in range(len(xs) // 2)]
                k //= 2
            part_ref[slot] = xs[0]
            return slot + 1

        return lax.fori_loop(g0, g1, grp, slot)

    nslot = lax.fori_loop(0, tb, tok_a, 0)

    def slots_b(j, carry):
        sl = pl.ds(pl.multiple_of(j * PLAN_BATCH, PLAN_BATCH), PLAN_BATCH)
        act = jnp.sum(part_ref[sl], axis=-1, keepdims=True)
        part_ref[sl] = jnp.broadcast_to(_gelu(act), (PLAN_BATCH, g8, LANES))
        return carry

    lax.fori_loop(0, (nslot + PLAN_BATCH - 1) // PLAN_BATCH, slots_b, 0)

    def tok_c(n, slot):
        g0, g1 = groups(n)

        def grp(g, carry):
            slot, acc = carry
            base = (n * gpt + g) * g8
            w8 = part_ref[slot]
            for i in range(g8):
                w = w8[i:i + 1, :] * gates_ref[base + i]
                acc = acc + w * v_ref[rows_ref[base + i]].astype(F32)
            return slot + 1, acc

        slot, acc = lax.fori_loop(g0, g1, grp, (slot, jnp.zeros((ROW_SUB, LANES), F32)))
        o_ref[n] = oin_ref[n] + acc
        return slot

    lax.fori_loop(0, tb, tok_c, 0)


def _expert_plan(idx, gate, n_exp, ch):
    m, nsel = idx.shape
    nch = n_exp // ch
    g8 = PEER_GROUP
    nfill = g8 - 1
    slots = -(-(nsel + nfill * nch) // g8) * g8
    cidx = idx // ch
    local = idx - cidx * ch
    chunks = jnp.arange(nch, dtype=I32)
    cnt = jnp.sum((cidx[:, :, None] == chunks[None, None, :]).astype(I32), axis=1)
    pad = (-cnt) % g8
    goff = jnp.cumsum((cnt + pad) // g8, axis=1)
    goff = jnp.concatenate([jnp.zeros((m, 1), I32), goff, jnp.zeros((m, PLAN_OFF_W - nch - 1), I32)], 1)
    span = 2 * ch
    fill_c = jnp.repeat(chunks, nfill)
    fill_r = jnp.tile(jnp.arange(nfill, dtype=I32), nch)
    active = fill_r[None, :] < pad[:, fill_c]
    fill_key = jnp.where(active, fill_c[None, :] * span + ch + fill_r[None, :], nch * span)
    ndead = slots - nsel - nfill * nch
    keys = jnp.concatenate([cidx * span + local, fill_key, jnp.full((m, ndead), nch * span, I32)], 1)
    zeros = jnp.zeros((m, slots - nsel), I32)
    rows = jnp.concatenate([local, zeros], 1)
    gates = jnp.concatenate([gate, zeros.astype(F32)], 1)
    _, rows, gates = lax.sort((keys, rows, gates), dimension=1, num_keys=1)
    return rows.reshape(-1), gates.reshape(-1), goff.reshape(-1), slots


def _experts(rows, gates, goff, slots, h3, u3, v3, *, tb, ch):
    m = h3.shape[0]
    e = u3.shape[0]
    gpt = slots // PEER_GROUP
    assert h3.shape[-1] == LANES and e % ch == 0 and m % tb == 0
    smem = lambda w: pl.BlockSpec((tb * w,), lambda c, t: (t,), memory_space=pltpu.SMEM)
    tok_blk = pl.BlockSpec((tb, ROW_SUB, LANES), lambda c, t: (t, 0, 0))
    tab_blk = pl.BlockSpec((ch, ROW_SUB, LANES), lambda c, t: (c, 0, 0), pipeline_mode=pl.Buffered(1))
    out0 = jnp.zeros((m, ROW_SUB, LANES), F32)
    return pl.pallas_call(
        functools.partial(_expert_kernel, tb=tb, gpt=gpt),
        grid=(e // ch, m // tb),
        in_specs=[smem(slots), smem(slots), smem(PLAN_OFF_W), tok_blk, tab_blk, tab_blk, tok_blk],
        out_specs=tok_blk,
        out_shape=jax.ShapeDtypeStruct((m, ROW_SUB, LANES), F32),
        scratch_shapes=[pltpu.VMEM((tb * gpt + PLAN_BATCH, PEER_GROUP, LANES), F32)],
        input_output_aliases={6: 0},
        compiler_params=_cparams(("arbitrary", "arbitrary")),
    )(rows, gates, goff, h3, u3, v3, out0)


def _block_diag2(w):
    z = jnp.zeros_like(w[0])
    return jnp.concatenate([jnp.concatenate([w[0], z], 1), jnp.concatenate([z, w[1]], 1)], 0)


def _per_head_pad(w, width, pad_to):
    k = w.shape[0]
    w = w.reshape(k, -1, width)
    return jnp.pad(w, ((0, 0), (0, 0), (0, pad_to - width))).reshape(k, -1)


def _rope_swap_cols(w_rope):
    half = MLA_ROPE // 2
    return jnp.concatenate([-w_rope[..., half:], w_rope[..., :half]], -1)


def _prep_layer_weights(l, w_in, mla_w_uq, mla_w_ukv, rk_w_up, rk_a_up, rk_g_up):
    d = w_in.shape[1]
    wi = w_in[l]
    lo, hi = RW_COLS, RW_COLS + MLA_Q_RANK + MLA_KV_RANK + MLA_ROPE
    w_rw = jnp.pad(wi[:, :lo], ((0, 0), (0, RW_COLS_PAD - RW_COLS))).astype(BF16)
    w_kr = wi[:, hi - MLA_ROPE:hi]
    place = lambda w: jnp.pad(w, ((0, 0), (MLA_NOPE, LANES - MLA_NOPE - MLA_ROPE)))
    w_mla = jnp.concatenate([wi[:, lo:hi - MLA_ROPE], place(w_kr), place(_rope_swap_cols(w_kr))], 1).astype(BF16)
    w_gate = wi[:, hi:].astype(BF16)

    uq = mla_w_uq[l].reshape(MLA_Q_RANK, MLA_HEADS, MLA_QK)
    zeros = jnp.zeros((MLA_Q_RANK, MLA_HEADS, LANES - MLA_QK), F32)
    wqa = jnp.concatenate([uq, zeros], -1).reshape(MLA_Q_RANK, -1).astype(BF16)
    zn = jnp.zeros((MLA_Q_RANK, MLA_HEADS, MLA_NOPE), F32)
    wqs = jnp.concatenate([zn, _rope_swap_cols(uq[..., MLA_NOPE:]), zeros], -1).reshape(MLA_Q_RANK, -1).astype(BF16)
    ukv = mla_w_ukv[l].reshape(MLA_KV_RANK, MLA_HEADS, MLA_NOPE + MLA_V)
    wk = jnp.pad(ukv[..., :MLA_NOPE], ((0, 0), (0, 0), (0, LANES - MLA_NOPE))).reshape(MLA_KV_RANK, -1).astype(BF16)
    wv = ukv[..., MLA_NOPE:].reshape(MLA_KV_RANK, -1).astype(BF16)

    wup = _block_diag2(rk_w_up[l]).astype(BF16)
    aup = _block_diag2(rk_a_up[l]).astype(BF16)
    gup = jnp.pad(_block_diag2(rk_g_up[l]), ((0, LANES - 2 * RW_GATE_LORA), (0, 0))).astype(BF16)
    return dict(w_rw=w_rw, w_mla=w_mla, w_gate=w_gate, wqa=wqa, wqs=wqs, wk=wk, wv=wv, wup=wup, aup=aup, gup=gup)


def _rope_lane_tables(l, c):
    rows = l // GRID_W
    row = jnp.repeat(jnp.arange(rows, dtype=F32), GRID_W)
    col = jnp.tile(jnp.arange(GRID_W, dtype=F32), rows)
    n_freq = MLA_ROPE // 4
    freqs = ROPE_BASE ** (-jnp.arange(n_freq, dtype=F32) / n_freq)
    ang = jnp.concatenate([row[:, None] * freqs, col[:, None] * freqs], -1)
    cos, sin = jnp.cos(ang), jnp.sin(ang)
    tail = LANES - MLA_NOPE - MLA_ROPE
    cos_t = jnp.concatenate([jnp.ones((l, MLA_NOPE), F32), cos, cos, jnp.ones((l, tail), F32)], -1)
    sin_t = jnp.concatenate([jnp.zeros((l, MLA_NOPE), F32), sin, sin, jnp.zeros((l, tail), F32)], -1)
    cos_t = jnp.concatenate([cos_t, jnp.ones((c, LANES), F32)], 0)
    sin_t = jnp.concatenate([sin_t, jnp.zeros((c, LANES), F32)], 0)
    return cos_t, sin_t


def _peer(h2, w_q, keys, u_tab, v_tab, *, ch, tb):
    m, d = h2.shape
    q = _mm(h2, w_q, tm=256, tn=_pick_tile(w_q.shape[1], 1024))
    idx_t, gate_t = _peer_topk(q, keys)
    idx = idx_t.transpose(2, 0, 1).reshape(m, PEER_SEL)
    gate = gate_t.transpose(2, 0, 1).reshape(m, PEER_SEL)
    n_exp = u_tab.shape[0]
    rows, gates, goff, slots = _expert_plan(idx, gate, n_exp, ch)
    lanes = d // ROW_SUB
    out = _experts(rows, gates, goff, slots, h2.reshape(m, ROW_SUB, lanes), u_tab.reshape(n_exp, ROW_SUB, lanes),
                   v_tab.reshape(n_exp, ROW_SUB, lanes), tb=tb, ch=ch)
    return out.reshape(m, d)


def _forward(x, c, ctx, c_ctx, ada_w, ada_b, norm_mix_g, w_in, gate_b, rk_mu, rk_w0, rk_w_up, rk_a0,
             rk_a_up, rk_g_up, rk_k_k, rk_k_a, rk_r_k, rk_ln_w, rk_ln_b, mla_q_norm, mla_w_uq, mla_kv_norm,
             mla_w_ukv, w_branch_a, w_branch_b, w_out, norm_ffn_g, peer_w_q, peer_keys, peer_u, peer_v,
             final_norm_g, *, flash_tq=512, flash_tk=768, scan_tt=32, peer_ch=4096, peer_tb=64):
    nb, l, d = x.shape
    cl = ctx.shape[1]
    t = l + cl
    m = nb * t
    depth = w_in.shape[0]
    dims = dict(B=nb, L=l, C=cl, T=t)
    assert nb == 2 and 4 * nb * RW_HEADS == LANES and l % 256 == 0 and cl % 256 == 0

    xa = jnp.concatenate([x, ctx], 1).reshape(m, d)
    cos_t, sin_t = _rope_lane_tables(l, cl)
    seg = jnp.arange(RW_WIDTH, dtype=I32) // RW_HEAD
    bd = (seg[:, None] == seg[None, :]).astype(BF16)
    cvec = jnp.zeros((SUBLANES, d), F32).at[:nb].set(jax.nn.silu(c)).at[nb].set(jax.nn.silu(c_ctx))

    delta, mods_prev = None, None
    out = None
    for li in range(depth):
        wl = _prep_layer_weights(li, w_in, mla_w_uq, mla_w_ukv, rk_w_up, rk_a_up, rk_g_up)
        tn_ada = _pick_tile(6 * d, 768)
        mods = _mm(cvec, ada_w[li], tm=SUBLANES, tn=tn_ada, epilogue=lambda acc, bias: acc + bias,
                   extras=[(ada_b[li][None, :], pl.BlockSpec((1, tn_ada), lambda j, i: (0, j)))])
        mods = mods[:nb + 1].reshape((nb + 1) * 6, 1, d)
        g_mix = norm_mix_g[li][None, :]
        if li == 0:
            h = _norm_mod(xa, g_mix, mods, 0, 1, dims)
        else:
            xa, h = _resid_norm_mod(xa, delta, mods_prev, 5, g_mix, mods, 0, 1, dims)

        p_rw = _mm(h, wl["w_rw"], tm=256, tn=_pick_tile(RW_COLS_PAD, 1152))
        p_mla = _mm(h, wl["w_mla"], tm=256, tn=MLA_P_COLS)
        p_gate = _mm(h, wl["w_gate"], tm=256, tn=1024)

        mu = jnp.pad(rk_mu[li], (0, RW_COLS_PAD - RW_COLS))[None, :]
        feats = _rwfeat(p_rw.reshape(nb, t, RW_COLS_PAD), mu, rk_w0[li], wl["wup"], rk_a0[li], wl["aup"], wl["gup"],
                        rk_k_k[li][None, :], rk_k_a[li][None, :], rk_r_k[li].reshape(1, RW_WIDTH), bd, dims)
        r, nkk, v, dec0, dec1, b0, b1, ke0, ke1, bon0, bon1, g0, g1 = feats
        a_next = _to_scan_k(nkk, nkk, l)
        a_next = jnp.concatenate([a_next[1:], jnp.zeros_like(a_next[:1])], 0)
        kv = jnp.stack([_to_scan_k(dec0, dec1, l), _to_scan_k(b0, b1, l), _to_scan_k(ke0, ke1, l),
                        _to_scan_k(r, r, l), a_next], 1)
        yn = _scan(kv, _to_scan_v(v, l), tt=scan_tt)
        yn0, yn1 = _from_scan_y(yn, nb, cl)
        flat = lambda z: z.reshape(m, RW_WIDTH)
        ya = _readout(flat(yn0), flat(yn1), flat(bon0), flat(bon1), flat(g0), flat(g1),
                      rk_ln_w[li][None, :], rk_ln_b[li][None, :])

        q, k, vv = _mla_prep(p_mla, cos_t, sin_t, mla_q_norm[li][None, :], mla_kv_norm[li][None, :],
                             wl["wqa"], wl["wqs"], wl["wk"], wl["wv"], dims)
        q3, k3, v3 = (z.reshape(nb, t, -1) for z in (q, k, vv))
        yb = _flash(q3, k3, v3, tq=flash_tq, tk=flash_tk, q_off=0, nq=l // flash_tq, k_off=0, nk=t // flash_tk)
        yb = _flash(q3, k3, v3, tq=cl, tk=cl, q_off=l // cl, nq=1, k_off=l // cl, nk=1, prev_out=yb)
        yb = yb.reshape(m, MLA_HEADS * MLA_V)

        tpb, lt = t // 256, l // 256
        br_a = _mm(ya, w_branch_a[li].astype(BF16), tm=256, tn=1024)
        gb = gate_b[li]
        merged = _mm(
            yb, w_branch_b[li].astype(BF16), tm=256, tn=1024, out_dtype=BF16,
            epilogue=lambda acc, ga, gbb, ba, bb, a: _sigmoid(ga + ba) * a + _sigmoid(gbb + bb) * acc,
            extras=[(p_gate, pl.BlockSpec((256, 1024), lambda j, i: (i, j))),
                    (p_gate, pl.BlockSpec((256, 1024), lambda j, i: (i, j + d // 1024))),
                    (gb[0][None, :], pl.BlockSpec((1, 1024), lambda j, i: (0, j))),
                    (gb[1][None, :], pl.BlockSpec((1, 1024), lambda j, i: (0, j))),
                    (br_a, pl.BlockSpec((256, 1024), lambda j, i: (i, j)))])
        x1 = _mm(merged, w_out[li].astype(BF16), tm=256, tn=1024,
                 epilogue=lambda acc, xr, gt: xr + gt * acc,
                 extras=[(xa, pl.BlockSpec((256, 1024), lambda j, i: (i, j))),
                         (mods, pl.BlockSpec((None, 1, 1024), lambda j, i: (_mod_row(i, tpb, lt, 2, nb), 0, j)))])

        h2 = _norm_mod(x1, norm_ffn_g[li][None, :], mods, 3, 4, dims)
        delta = _peer(h2, peer_w_q[li].astype(BF16), peer_keys[li], peer_u[li].astype(BF16),
                      peer_v[li].astype(BF16), ch=peer_ch, tb=peer_tb)
        xa, mods_prev = x1, mods

    out = _resid_final_norm(xa.reshape(nb, t, d), delta.reshape(nb, t, d), mods_prev, 5,
                            final_norm_g[None, :], dims)
    return out


def kernel(x, c, ctx, c_ctx, ada_w, ada_b, norm_mix_g, w_in, gate_b, rk_mu, rk_w0, rk_w_up, rk_a0, rk_a_up, rk_g_up, rk_k_k, rk_k_a, rk_r_k, rk_ln_w, rk_ln_b, mla_q_norm, mla_w_uq, mla_kv_norm, mla_w_ukv, w_branch_a, w_branch_b, w_out, norm_ffn_g, peer_w_q, peer_keys, peer_u, peer_v, final_norm_g):
    return _forward(x, c, ctx, c_ctx, ada_w, ada_b, norm_mix_g, w_in, gate_b, rk_mu, rk_w0, rk_w_up, rk_a0,
                    rk_a_up, rk_g_up, rk_k_k, rk_k_a, rk_r_k, rk_ln_w, rk_ln_b, mla_q_norm, mla_w_uq,
                    mla_kv_norm, mla_w_ukv, w_branch_a, w_branch_b, w_out, norm_ffn_g, peer_w_q, peer_keys,
                    peer_u, peer_v, final_norm_g)
```

```python
import functools
import math

import jax
import jax.numpy as jnp
from jax import lax
from jax.experimental import pallas as pl
from jax.experimental.pallas import tpu as pltpu

F32 = jnp.float32
BF16 = jnp.bfloat16
I32 = jnp.int32

NORM_EPS = 1e-6
GRID_W = 64
ROPE_BASE = 10000.0

RW_HEADS = 16
RW_HEAD = 64
RW_WIDTH = RW_HEADS * RW_HEAD
RW_DECAY_LORA = 64
RW_A_LORA = 64
RW_GATE_LORA = 32
RW_GN_EPS = 64e-5
RW_COLS = 3 * RW_WIDTH + 2 * (RW_DECAY_LORA + RW_A_LORA + RW_GATE_LORA)
RW_COLS_PAD = 3456

MLA_HEADS = 16
MLA_NOPE = 64
MLA_ROPE = 32
MLA_QK = MLA_NOPE + MLA_ROPE
MLA_V = 64
MLA_Q_RANK = 512
MLA_KV_RANK = 256
MLA_HEAD_PAD = 128
MLA_P_COLS = MLA_Q_RANK + MLA_KV_RANK + 2 * MLA_HEAD_PAD

PEER_HEADS = 8
PEER_NKEYS = 128
PEER_HALF = 128
PEER_TOPK = 16
PEER_SEL = PEER_HEADS * PEER_TOPK

LANES = 128
SUBLANES = 8
VMEM_LIMIT = 56 * 1024 * 1024


def _cparams(sem):
    return pltpu.CompilerParams(dimension_semantics=sem, vmem_limit_bytes=VMEM_LIMIT)


def _pick_tile(n, cap, mult=LANES):
    best = None
    t = mult
    while t <= min(n, cap):
        if n % t == 0:
            best = t
        t += mult
    assert best is not None, (n, cap)
    return best


def _mm_kernel(*refs, n_extra, epilogue):
    a_ref, w_ref = refs[0], refs[1]
    extra = refs[2:2 + n_extra]
    o_ref = refs[2 + n_extra]
    acc = jnp.dot(a_ref[...].astype(BF16), w_ref[...].astype(BF16), preferred_element_type=F32)
    if epilogue is not None:
        acc = epilogue(acc, *[e[...] for e in extra])
    o_ref[...] = acc.astype(o_ref.dtype)


def _mm(a, w, *, tm, tn, out_dtype=F32, epilogue=None, extras=()):
    m, k = a.shape
    k2, n = w.shape
    assert k == k2 and m % tm == 0 and n % tn == 0, (a.shape, w.shape, tm, tn)
    in_specs = [pl.BlockSpec((tm, k), lambda j, i: (i, 0)),
                pl.BlockSpec((k, tn), lambda j, i: (0, j))] + [s for _, s in extras]
    return pl.pallas_call(
        functools.partial(_mm_kernel, n_extra=len(extras), epilogue=epilogue),
        grid=(n // tn, m // tm),
        in_specs=in_specs,
        out_specs=pl.BlockSpec((tm, tn), lambda j, i: (i, j)),
        out_shape=jax.ShapeDtypeStruct((m, n), out_dtype),
        compiler_params=_cparams(("parallel", "parallel")),
    )(a, w, *[x for x, _ in extras])


def _sigmoid(x):
    return 1.0 / (1.0 + jnp.exp(-x))


def _mod_row(i, tiles_per_batch, lat_tiles, part, n_batch):
    which = jnp.where(i % tiles_per_batch >= lat_tiles, n_batch, i // tiles_per_batch)
    return which * 6 + part


def _rms(x, g):
    return x * lax.rsqrt(jnp.mean(x * x, -1, keepdims=True) + NORM_EPS) * g


def _norm_mod_kernel(x_ref, g_ref, sh_ref, sc_ref, h_ref):
    y = _rms(x_ref[...], g_ref[...])
    h_ref[...] = (y * (1.0 + sc_ref[...]) + sh_ref[...]).astype(h_ref.dtype)


def _resid_norm_mod_kernel(x_ref, d_ref, gt_ref, g_ref, sh_ref, sc_ref, xo_ref, h_ref):
    x = x_ref[...] + gt_ref[...] * d_ref[...]
    xo_ref[...] = x
    y = _rms(x, g_ref[...])
    h_ref[...] = (y * (1.0 + sc_ref[...]) + sh_ref[...]).astype(h_ref.dtype)


def _resid_final_norm_kernel(x_ref, d_ref, gt_ref, g_ref, o_ref):
    x = x_ref[...] + gt_ref[...] * d_ref[...]
    o_ref[...] = _rms(x, g_ref[...])


def _mod_spec(d, part, dims, tm):
    tpb, lt, nb = dims["T"] // tm, dims["L"] // tm, dims["B"]
    return pl.BlockSpec((None, 1, d), lambda i: (_mod_row(i, tpb, lt, part, nb), 0, 0))


def _norm_mod(x, g, mods, part_sh, part_sc, dims, tm=256):
    m, d = x.shape
    row = pl.BlockSpec((tm, d), lambda i: (i, 0))
    return pl.pallas_call(
        _norm_mod_kernel, grid=(m // tm,),
        in_specs=[row, pl.BlockSpec((1, d), lambda i: (0, 0)),
                  _mod_spec(d, part_sh, dims, tm), _mod_spec(d, part_sc, dims, tm)],
        out_specs=row, out_shape=jax.ShapeDtypeStruct((m, d), BF16),
        compiler_params=_cparams(("parallel",)),
    )(x, g, mods, mods)


def _resid_norm_mod(x, delta, mods_gt, part_gt, g, mods, part_sh, part_sc, dims, tm=256):
    m, d = x.shape
    row = pl.BlockSpec((tm, d), lambda i: (i, 0))
    return pl.pallas_call(
        _resid_norm_mod_kernel, grid=(m // tm,),
        in_specs=[row, row, _mod_spec(d, part_gt, dims, tm), pl.BlockSpec((1, d), lambda i: (0, 0)),
                  _mod_spec(d, part_sh, dims, tm), _mod_spec(d, part_sc, dims, tm)],
        out_specs=[row, row],
        out_shape=[jax.ShapeDtypeStruct((m, d), F32), jax.ShapeDtypeStruct((m, d), BF16)],
        compiler_params=_cparams(("parallel",)),
    )(x, delta, mods_gt, g, mods, mods)


def _resid_final_norm(x3, delta3, mods, part_gt, g, dims, tm=256):
    b, t, d = x3.shape
    lt = dims["L"] // tm
    blk = pl.BlockSpec((None, tm, d), lambda bi, i: (bi, i, 0))
    return pl.pallas_call(
        _resid_final_norm_kernel, grid=(b, lt),
        in_specs=[blk, blk, pl.BlockSpec((None, 1, d), lambda bi, i: (bi * 6 + part_gt, 0, 0)),
                  pl.BlockSpec((1, d), lambda bi, i: (0, 0))],
        out_specs=blk, out_shape=jax.ShapeDtypeStruct((b, dims["L"], d), F32),
        compiler_params=_cparams(("parallel", "parallel")),
    )(x3, delta3, mods, g)


def _segsum(x, bd):
    hi = x.astype(BF16)
    lo = (x - hi.astype(F32)).astype(BF16)
    return (jnp.dot(hi, bd, preferred_element_type=F32) + jnp.dot(lo, bd, preferred_element_type=F32))


def _rwfeat_kernel(p_ref, pp_ref, pn_ref, mu_ref, w0_ref, wup_ref, a0_ref, aup_ref, gup_ref, kk_ref, ka_ref,
                   rk_ref, bd_ref,
                   r_ref, nkk_ref, v_ref, dec0_ref, dec1_ref, b0_ref, b1_ref, ke0_ref, ke1_ref,
                   bon0_ref, bon1_ref, g0_ref, g1_ref, *, tm, lat_tiles, all_tiles):
    i = pl.program_id(1)
    p = p_ref[...]
    prev_ok = jnp.logical_and(i != 0, i != lat_tiles)
    next_ok = jnp.logical_and(i != lat_tiles - 1, i != all_tiles - 1)
    prow = jnp.where(prev_ok, pp_ref[SUBLANES - 1:SUBLANES, :], 0.0)
    nrow = jnp.where(next_ok, pn_ref[0:1, :], 0.0)
    rid = lax.broadcasted_iota(I32, (tm, 1), 0)
    prev = jnp.where(rid == 0, prow, pltpu.roll(p, 1, 0))
    nxt = jnp.where(rid == tm - 1, nrow, pltpu.roll(p, tm - 1, 0))
    ps = p + (0.5 * (prev + nxt) - p) * mu_ref[...]

    w_ = RW_WIDTH
    r = ps[:, 0:w_]
    k = ps[:, w_:2 * w_]
    v = ps[:, 2 * w_:3 * w_]
    wd = ps[:, 3 * w_:3 * w_ + LANES]
    ad = ps[:, 3 * w_ + LANES:3 * w_ + 2 * LANES]
    gd = ps[:, 3 * w_ + 2 * LANES:3 * w_ + 3 * LANES]
    bd = bd_ref[...]

    lw = jnp.dot(jnp.tanh(wd).astype(BF16), wup_ref[...], preferred_element_type=F32)
    la = jnp.dot(ad.astype(BF16), aup_ref[...], preferred_element_type=F32)
    lg = jnp.dot(_sigmoid(gd).astype(BF16), gup_ref[...], preferred_element_type=F32)

    kk = k * kk_ref[...]
    kk = kk * lax.rsqrt(_segsum(kk * kk, bd) + 1e-12)
    r_ref[...] = r
    nkk_ref[...] = -kk
    v_ref[...] = v
    rk = rk_ref[...]
    ka = ka_ref[...]
    outs = ((dec0_ref, b0_ref, ke0_ref, bon0_ref, g0_ref), (dec1_ref, b1_ref, ke1_ref, bon1_ref, g1_ref))
    for d in range(2):
        dec_ref, b_ref, ke_ref, bon_ref, g_ref = outs[d]
        z = w0_ref[d:d + 1, :] + lw[:, d * w_:(d + 1) * w_]
        nz = -z
        softplus = jnp.maximum(nz, 0.0) + jnp.log(1.0 + jnp.exp(-jnp.abs(nz)))
        wlog = -softplus - 0.5
        dec_ref[...] = jnp.exp(-jnp.exp(wlog))
        a = _sigmoid(a0_ref[d:d + 1, :] + la[:, d * w_:(d + 1) * w_])
        b_ref[...] = kk * a
        ke = k * (1.0 + (a - 1.0) * ka)
        ke_ref[...] = ke
        bon_ref[...] = _segsum(r * ke * rk, bd) * v
        g_ref[...] = lg[:, d * w_:(d + 1) * w_]


def _rwfeat(p3, mu, w0, wup, a0, aup, gup, k_k, k_a, r_k, bd, dims, tm=128):
    b, t, wc = p3.shape
    lt, at = dims["L"] // tm, t // tm
    hb = tm // SUBLANES
    nblk8 = t // SUBLANES
    w_ = RW_WIDTH
    full = lambda arr: pl.BlockSpec(arr.shape, lambda bi, i: (0,) * arr.ndim)
    out_blk = pl.BlockSpec((None, tm, w_), lambda bi, i: (bi, i, 0))
    n_out = 13
    return pl.pallas_call(
        functools.partial(_rwfeat_kernel, tm=tm, lat_tiles=lt, all_tiles=at),
        grid=(b, at),
        in_specs=[pl.BlockSpec((None, tm, wc), lambda bi, i: (bi, i, 0)),
                  pl.BlockSpec((None, SUBLANES, wc), lambda bi, i: (bi, jnp.maximum(i * hb - 1, 0), 0)),
                  pl.BlockSpec((None, SUBLANES, wc), lambda bi, i: (bi, jnp.minimum((i + 1) * hb, nblk8 - 1), 0)),
                  full(mu), full(w0), full(wup), full(a0), full(aup), full(gup), full(k_k), full(k_a),
                  full(r_k), full(bd)],
        out_specs=[out_blk] * n_out,
        out_shape=[jax.ShapeDtypeStruct((b, t, w_), F32)] * n_out,
        compiler_params=_cparams(("parallel", "parallel")),
    )(p3, p3, p3, mu, w0, wup, a0, aup, gup, k_k, k_a, r_k, bd)


SCAN_KH = RW_HEAD // 2


def _scan_kernel(kv_ref, v_ref, y_ref, s_ref, sa_ref, *, tt):
    @pl.when(pl.program_id(0) == 0)
    def _():
        s_ref[...] = jnp.zeros_like(s_ref)
        sa_ref[...] = jnp.zeros_like(sa_ref)

    lane = lax.broadcasted_iota(I32, (RW_HEAD, LANES), 1)

    def one_step(t, sa):
        vt = v_ref[t]
        yacc = jnp.zeros((RW_HEAD, LANES), F32)
        sn = jnp.zeros((RW_HEAD, LANES), F32)
        for k in range(SCAN_KH):
            w = kv_ref[t, 0, pl.ds(k, 1), :]
            b = kv_ref[t, 1, pl.ds(k, 1), :]
            ke = kv_ref[t, 2, pl.ds(k, 1), :]
            r = kv_ref[t, 3, pl.ds(k, 1), :]
            an = kv_ref[t, 4, pl.ds(k, 1), :]
            s = s_ref[k] * w + (sa * b + vt * ke)
            s_ref[k] = s
            yacc = yacc + s * r
            sn = sn + s * an
        y = yacc + pltpu.roll(yacc, LANES // 2, 1)
        sa_new = sn + pltpu.roll(sn, LANES // 2, 1)
        mean = jnp.mean(y, axis=0, keepdims=True)
        dlt = y - mean
        var = jnp.mean(dlt * dlt, axis=0, keepdims=True)
        return sa_new, dlt * lax.rsqrt(var + RW_GN_EPS)

    def body(j, sa):
        sa, y0 = one_step(2 * j, sa)
        sa, y1 = one_step(2 * j + 1, sa)
        y_ref[j] = jnp.where(lane < LANES // 2, y0, y1)
        return sa

    sa_ref[...] = lax.fori_loop(0, tt // 2, body, sa_ref[...])


def _scan(kv, vv, tt=32):
    t = kv.shape[0]
    assert t % tt == 0 and tt % 2 == 0
    return pl.pallas_call(
        functools.partial(_scan_kernel, tt=tt),
        grid=(t // tt,),
        in_specs=[pl.BlockSpec((tt, 5, SCAN_KH, LANES), lambda i: (i, 0, 0, 0)),
                  pl.BlockSpec((tt, RW_HEAD, LANES), lambda i: (i, 0, 0))],
        out_specs=pl.BlockSpec((tt // 2, RW_HEAD, LANES), lambda i: (i, 0, 0)),
        out_shape=jax.ShapeDtypeStruct((t // 2, RW_HEAD, LANES), F32),
        scratch_shapes=[pltpu.VMEM((SCAN_KH, RW_HEAD, LANES), F32), pltpu.VMEM((RW_HEAD, LANES), F32)],
        compiler_params=_cparams(("arbitrary",)),
    )(kv, vv)


def _seq_order(a, l, reverse):
    lat, ctx = a[:l], a[l:]
    if reverse:
        lat, ctx = jnp.flip(lat, 0), jnp.flip(ctx, 0)
    return jnp.concatenate([ctx, lat], 0)


def _seq_unorder(a, c, reverse):
    ctx, lat = a[:c], a[c:]
    if reverse:
        lat, ctx = jnp.flip(lat, 0), jnp.flip(ctx, 0)
    return jnp.concatenate([lat, ctx], 0)


def _lay_k(x):
    b, t, _ = x.shape
    return x.reshape(b, t, RW_HEADS, 2, SCAN_KH).transpose(1, 4, 3, 0, 2).reshape(t, SCAN_KH, 2 * b * RW_HEADS)


def _to_scan_k(x0, x1, l):
    s0 = _seq_order(_lay_k(x0), l, False)
    s1 = _seq_order(_lay_k(x0 if x1 is x0 else x1), l, True)
    q = s0.shape[-1] // 2
    return jnp.concatenate([s0[..., :q], s1[..., :q], s0[..., q:], s1[..., q:]], -1)


def _to_scan_v(x, l):
    b, t, _ = x.shape
    s = x.reshape(b, t, RW_HEADS, RW_HEAD).transpose(1, 3, 0, 2).reshape(t, RW_HEAD, b * RW_HEADS)
    s0, s1 = _seq_order(s, l, False), _seq_order(s, l, True)
    return jnp.concatenate([s0, s1, s0, s1], -1)


def _from_scan_y(y, b, c):
    t2 = y.shape[0]
    q = b * RW_HEADS
    s = y.reshape(t2, RW_HEAD, 2, 2 * q).transpose(0, 2, 1, 3).reshape(2 * t2, RW_HEAD, 2 * q)
    outs = []
    for d in range(2):
        sd = _seq_unorder(s[..., d * q:(d + 1) * q], c, d == 1)
        outs.append(sd.reshape(2 * t2, RW_HEAD, b, RW_HEADS).transpose(2, 0, 3, 1).reshape(b, 2 * t2, RW_WIDTH))
    return outs[0], outs[1]


def _readout_kernel(y0_ref, y1_ref, bon0_ref, bon1_ref, g0_ref, g1_ref, lw_ref, lb_ref, o_ref):
    lw, lb = lw_ref[...], lb_ref[...]
    o = (y0_ref[...] * lw + lb + bon0_ref[...]) * g0_ref[...]
    o = o + (y1_ref[...] * lw + lb + bon1_ref[...]) * g1_ref[...]
    o_ref[...] = o.astype(o_ref.dtype)


def _readout(y0, y1, bon0, bon1, g0, g1, ln_w, ln_b, tm=256):
    m, w_ = y0.shape
    row = pl.BlockSpec((tm, w_), lambda i: (i, 0))
    vec = pl.BlockSpec((1, w_), lambda i: (0, 0))
    return pl.pallas_call(
        _readout_kernel, grid=(m // tm,), in_specs=[row] * 6 + [vec, vec], out_specs=row,
        out_shape=jax.ShapeDtypeStruct((m, w_), BF16), compiler_params=_cparams(("parallel",)),
    )(y0, y1, bon0, bon1, g0, g1, ln_w, ln_b)


def _mla_prep_kernel(p_ref, c_ref, s_ref, qn_ref, kn_ref, wqa_ref, wqs_ref, wk_ref, wv_ref, q_ref, k_ref, v_ref):
    p = p_ref[...]
    cos, sin = c_ref[...], s_ref[...]
    qc = _rms(p[:, :MLA_Q_RANK], qn_ref[...]).astype(BF16)
    kvc = _rms(p[:, MLA_Q_RANK:MLA_Q_RANK + MLA_KV_RANK], kn_ref[...]).astype(BF16)
    kr = p[:, MLA_Q_RANK + MLA_KV_RANK:MLA_Q_RANK + MLA_KV_RANK + LANES]
    krs = p[:, MLA_Q_RANK + MLA_KV_RANK + LANES:MLA_Q_RANK + MLA_KV_RANK + 2 * LANES]
    krope = kr * cos + krs * sin
    qa = jnp.dot(qc, wqa_ref[...], preferred_element_type=F32)
    qs = jnp.dot(qc, wqs_ref[...], preferred_element_type=F32)
    kk = jnp.dot(kvc, wk_ref[...], preferred_element_type=F32)
    v_ref[...] = jnp.dot(kvc, wv_ref[...], preferred_element_type=F32).astype(v_ref.dtype)
    scale = MLA_QK ** -0.5
    for h in range(MLA_HEADS):
        sl = slice(h * LANES, (h + 1) * LANES)
        q_ref[:, sl] = ((qa[:, sl] * cos + qs[:, sl] * sin) * scale).astype(q_ref.dtype)
        k_ref[:, sl] = (kk[:, sl] + krope).astype(k_ref.dtype)


def _mla_prep(p, cos_t, sin_t, q_norm, kv_norm, wqa, wqs, wk, wv, dims, tm=256):
    m, pc = p.shape
    tpb = dims["T"] // tm
    full = lambda arr: pl.BlockSpec(arr.shape, lambda i: (0,) * arr.ndim)
    tab = pl.BlockSpec((tm, LANES), lambda i: (i % tpb, 0))
    hw = MLA_HEADS * LANES
    return pl.pallas_call(
        _mla_prep_kernel, grid=(m // tm,),
        in_specs=[pl.BlockSpec((tm, pc), lambda i: (i, 0)), tab, tab, full(q_norm), full(kv_norm),
                  full(wqa), full(wqs), full(wk), full(wv)],
        out_specs=[pl.BlockSpec((tm, hw), lambda i: (i, 0)), pl.BlockSpec((tm, hw), lambda i: (i, 0)),
                   pl.BlockSpec((tm, MLA_HEADS * MLA_V), lambda i: (i, 0))],
        out_shape=[jax.ShapeDtypeStruct((m, hw), BF16), jax.ShapeDtypeStruct((m, hw), BF16),
                   jax.ShapeDtypeStruct((m, MLA_HEADS * MLA_V), BF16)],
        compiler_params=_cparams(("parallel",)),
    )(p, cos_t, sin_t, q_norm, kv_norm, wqa, wqs, wk, wv)


def _flash_kernel(q_ref, k_ref, v_ref, *rest, nk):
    o_ref = rest[-7]
    state = (rest[-6:-3], rest[-3:])
    ki = pl.program_id(3)

    @pl.when(ki == 0)
    def _():
        for m_ref, l_ref, acc_ref in state:
            m_ref[...] = jnp.full_like(m_ref, -jnp.inf)
            l_ref[...] = jnp.zeros_like(l_ref)
            acc_ref[...] = jnp.zeros_like(acc_ref)

    v = v_ref[...]
    for hh, (m_ref, l_ref, acc_ref) in enumerate(state):
        q = q_ref[:, hh * LANES:(hh + 1) * LANES]
        k = k_ref[:, hh * LANES:(hh + 1) * LANES]
        s = lax.dot_general(q, k, (((1,), (1,)), ((), ())), preferred_element_type=F32)
        m_prev = m_ref[...]
        m_new = jnp.maximum(m_prev, jnp.max(s, -1, keepdims=True))
        alpha = jnp.exp(m_prev - m_new)
        pr = jnp.exp(s - m_new)
        l_ref[...] = alpha * l_ref[...] + jnp.sum(pr, -1, keepdims=True)
        acc_ref[...] = alpha * acc_ref[...] + jnp.dot(pr.astype(BF16), v, preferred_element_type=F32)
        m_ref[...] = m_new

    @pl.when(ki == nk - 1)
    def _():
        (_, l0, acc0), (_, l1, acc1) = state
        lane = lax.broadcasted_iota(I32, acc0.shape, 1)
        o_ref[...] = jnp.where(lane < MLA_V, acc0[...] / l0[...], acc1[...] / l1[...]).astype(o_ref.dtype)


def _flash(q3, k3, v3, *, tq, tk, q_off, nq, k_off, nk, prev_out=None):
    b, t, _ = q3.shape
    hp = MLA_HEADS // 2
    in_specs = [pl.BlockSpec((None, tq, 2 * LANES), lambda bi, h, qi, ki: (bi, q_off + qi, h)),
                pl.BlockSpec((None, tk, 2 * LANES), lambda bi, h, qi, ki: (bi, k_off + ki, h)),
                pl.BlockSpec((None, tk, LANES), lambda bi, h, qi, ki: (bi, k_off + ki, h))]
    args = [q3, k3, v3]
    aliases = {}
    if prev_out is not None:
        in_specs.append(pl.BlockSpec(memory_space=pl.ANY))
        args.append(prev_out)
        aliases = {3: 0}
    return pl.pallas_call(
        functools.partial(_flash_kernel, nk=nk),
        grid=(b, hp, nq, nk),
        in_specs=in_specs,
        out_specs=pl.BlockSpec((None, tq, LANES), lambda bi, h, qi, ki: (bi, q_off + qi, h)),
        out_shape=jax.ShapeDtypeStruct((b, t, MLA_HEADS * MLA_V), BF16),
        scratch_shapes=[pltpu.VMEM((tq, 1), F32), pltpu.VMEM((tq, 1), F32), pltpu.VMEM((tq, LANES), F32)] * 2,
        input_output_aliases=aliases,
        compiler_params=_cparams(("parallel", "parallel", "parallel", "arbitrary")),
    )(*args)


def _topk_rows(s, kk, payload=None):
    nrow = s.shape[0]
    rid = lax.broadcasted_iota(I32, s.shape, 0)
    vals, sel = [], []
    for _ in range(kk):
        m = jnp.max(s, axis=0, keepdims=True)
        pos = jnp.min(jnp.where(s == m, rid, nrow), axis=0, keepdims=True)
        hit = rid == pos
        vals.append(m)
        sel.append(pos if payload is None else jnp.max(jnp.where(hit, payload, -1), axis=0, keepdims=True))
        s = jnp.where(hit, -jnp.inf, s)
    return jnp.concatenate(vals, 0), jnp.concatenate(sel, 0)


def _peer_topk_kernel(q0_ref, q1_ref, keys_ref, idx_ref, gate_ref):
    nt = (((1,), (1,)), ((), ()))
    s0 = lax.dot_general(keys_ref[0].astype(BF16), q0_ref[...].astype(BF16), nt, preferred_element_type=F32)
    s1 = lax.dot_general(keys_ref[1].astype(BF16), q1_ref[...].astype(BF16), nt, preferred_element_type=F32)
    v1, i1 = _topk_rows(s0, PEER_TOPK)
    v2, i2 = _topk_rows(s1, PEER_TOPK)
    cand_s = jnp.concatenate([v1[a:a + 1, :] + v2 for a in range(PEER_TOPK)], 0)
    cand_i = jnp.concatenate([i1[a:a + 1, :] * PEER_NKEYS + i2 for a in range(PEER_TOPK)], 0)
    top_s, top_i = _topk_rows(cand_s, PEER_TOPK, payload=cand_i)
    e = jnp.exp(top_s - top_s[0:1, :])
    idx_ref[...] = top_i
    gate_ref[...] = e / jnp.sum(e, axis=0, keepdims=True)


def _peer_topk(q, keys, tm=256):
    m = q.shape[0]
    out_blk = pl.BlockSpec((None, PEER_TOPK, tm), lambda i, h: (h, 0, i))
    return pl.pallas_call(
        _peer_topk_kernel, grid=(m // tm, PEER_HEADS),
        in_specs=[pl.BlockSpec((tm, PEER_HALF), lambda i, h: (i, 2 * h)),
                  pl.BlockSpec((tm, PEER_HALF), lambda i, h: (i, 2 * h + 1)),
                  pl.BlockSpec((None, 2, PEER_NKEYS, PEER_HALF), lambda i, h: (h, 0, 0, 0))],
        out_specs=[out_blk, out_blk],
        out_shape=[jax.ShapeDtypeStruct((PEER_HEADS, PEER_TOPK, m), I32),
                   jax.ShapeDtypeStruct((PEER_HEADS, PEER_TOPK, m), F32)],
        compiler_params=_cparams(("parallel", "parallel")),
    )(q, q, keys)


PEER_GROUP = 16
ROW_SUB = 16
PLAN_OFF_W = 16
PLAN_BATCH = 16
BITREV8 = (0, 4, 2, 6, 1, 5, 3, 7)


def _gelu(x):
    return 0.5 * x * (1.0 + lax.erf(x * (2.0 ** -0.5)))


def _fold_pair(xa, xb, k, sub):
    mask = (sub & (2 * k - 1)) < k
    a = jnp.where(mask, xa, xb)
    if 2 * k == SUBLANES:
        return a + pltpu.roll(jnp.where(mask, xb, xa), k, 0)
    return a + jnp.where(mask, pltpu.roll(xa, SUBLANES - k, 0), pltpu.roll(xb, k, 0))


def _expert_kernel(rows_ref, gates_ref, goff_ref, h_ref, u_ref, v_ref, oin_ref, o_ref, part_ref, *, tb, gpt):
    c = pl.program_id(0)
    g8 = SUBLANES
    spg = PEER_GROUP // g8
    sub = lax.broadcasted_iota(I32, (g8, LANES), 0)

    @pl.when(jnp.logical_and(c == 0, pl.program_id(1) == 0))
    def _():
        part_ref[...] = jnp.zeros_like(part_ref)

    def groups(n):
        return goff_ref[n * PLAN_OFF_W + c], goff_ref[n * PLAN_OFF_W + c + 1]

    def tok_a(n, slot):
        g0, g1 = groups(n)
        h = h_ref[n].astype(F32)

        def grp(g, slot):
            for s in range(spg):
                base = (n * gpt + g) * PEER_GROUP + s * g8
                xs = []
                for i in range(g8):
                    prod = h * u_ref[rows_ref[base + i]].astype(F32)
                    xs.append(prod[0:SUBLANES] + prod[SUBLANES:ROW_SUB])
                xs = [xs[BITREV8[j]] for j in range(g8)]
                k = g8 // 2
                while k >= 1:
                    xs = [_fold_pair(xs[2 * j], xs[2 * j + 1], k, sub) for j in range(len(xs) // 2)]
                    k //= 2
                part_ref[slot + s] = xs[0]
            return slot + spg

        return lax.fori_loop(g0, g1, grp, slot)

    nslot = lax.fori_loop(0, tb, tok_a, 0)

    def slots_b(j, carry):
        sl = pl.ds(pl.multiple_of(j * PLAN_BATCH, PLAN_BATCH), PLAN_BATCH)
        act = jnp.sum(part_ref[sl], axis=-1, keepdims=True)
        part_ref[sl] = jnp.broadcast_to(_gelu(act), (PLAN_BATCH, g8, LANES))
        return carry

    lax.fori_loop(0, (nslot + PLAN_BATCH - 1) // PLAN_BATCH, slots_b, 0)

    def tok_c(n, slot):
        g0, g1 = groups(n)

        def grp(g, carry):
            slot, acc = carry
            for s in range(spg):
                base = (n * gpt + g) * PEER_GROUP + s * g8
                w8 = part_ref[slot + s]
                for i in range(g8):
                    w = w8[i:i + 1, :] * gates_ref[base + i]
                    acc = acc + w * v_ref[rows_ref[base + i]].astype(F32)
            return slot + spg, acc

        slot, acc = lax.fori_loop(g0, g1, grp, (slot, jnp.zeros((ROW_SUB, LANES), F32)))
        o_ref[n] = oin_ref[n] + acc
        return slot

    lax.fori_loop(0, tb, tok_c, 0)


def _expert_plan(idx, gate, n_exp, ch):
    m, nsel = idx.shape
    nch = n_exp // ch
    g8 = PEER_GROUP
    nfill = g8 - 1
    slots = -(-(nsel + nfill * nch) // g8) * g8
    cidx = idx // ch
    local = idx - cidx * ch
    chunks = jnp.arange(nch, dtype=I32)
    cnt = jnp.sum((cidx[:, :, None] == chunks[None, None, :]).astype(I32), axis=1)
    pad = (-cnt) % g8
    goff = jnp.cumsum((cnt + pad) // g8, axis=1)
    goff = jnp.concatenate([jnp.zeros((m, 1), I32), goff, jnp.zeros((m, PLAN_OFF_W - nch - 1), I32)], 1)
    span = 2 * ch
    fill_c = jnp.repeat(chunks, nfill)
    fill_r = jnp.tile(jnp.arange(nfill, dtype=I32), nch)
    active = fill_r[None, :] < pad[:, fill_c]
    fill_key = jnp.where(active, fill_c[None, :] * span + ch + fill_r[None, :], nch * span)
    ndead = slots - nsel - nfill * nch
    keys = jnp.concatenate([cidx * span + local, fill_key, jnp.full((m, ndead), nch * span, I32)], 1)
    zeros = jnp.zeros((m, slots - nsel), I32)
    rows = jnp.concatenate([local, zeros], 1)
    gates = jnp.concatenate([gate, zeros.astype(F32)], 1)
    _, rows, gates = lax.sort((keys, rows, gates), dimension=1, num_keys=1)
    return rows.reshape(-1), gates.reshape(-1), goff.reshape(-1), slots


def _experts(rows, gates, goff, slots, h3, u3, v3, *, tb, ch):
    m = h3.shape[0]
    e = u3.shape[0]
    gpt = slots // PEER_GROUP
    assert h3.shape[-1] == LANES and e % ch == 0 and m % tb == 0
    smem = lambda w: pl.BlockSpec((tb * w,), lambda c, t: (t,), memory_space=pltpu.SMEM)
    tok_blk = pl.BlockSpec((tb, ROW_SUB, LANES), lambda c, t: (t, 0, 0))
    tab_blk = pl.BlockSpec((ch, ROW_SUB, LANES), lambda c, t: (c, 0, 0), pipeline_mode=pl.Buffered(1))
    out0 = jnp.zeros((m, ROW_SUB, LANES), F32)
    return pl.pallas_call(
        functools.partial(_expert_kernel, tb=tb, gpt=gpt),
        grid=(e // ch, m // tb),
        in_specs=[smem(slots), smem(slots), smem(PLAN_OFF_W), tok_blk, tab_blk, tab_blk, tok_blk],
        out_specs=tok_blk,
        out_shape=jax.ShapeDtypeStruct((m, ROW_SUB, LANES), F32),
        scratch_shapes=[pltpu.VMEM((tb * slots // SUBLANES + PLAN_BATCH, SUBLANES, LANES), F32)],
        input_output_aliases={6: 0},
        compiler_params=_cparams(("arbitrary", "arbitrary")),
    )(rows, gates, goff, h3, u3, v3, out0)


def _block_diag2(w):
    z = jnp.zeros_like(w[0])
    return jnp.concatenate([jnp.concatenate([w[0], z], 1), jnp.concatenate([z, w[1]], 1)], 0)


def _per_head_pad(w, width, pad_to):
    k = w.shape[0]
    w = w.reshape(k, -1, width)
    return jnp.pad(w, ((0, 0), (0, 0), (0, pad_to - width))).reshape(k, -1)


def _rope_swap_cols(w_rope):
    half = MLA_ROPE // 2
    return jnp.concatenate([-w_rope[..., half:], w_rope[..., :half]], -1)


def _prep_layer_weights(l, w_in, mla_w_uq, mla_w_ukv, rk_w_up, rk_a_up, rk_g_up):
    d = w_in.shape[1]
    wi = w_in[l]
    lo, hi = RW_COLS, RW_COLS + MLA_Q_RANK + MLA_KV_RANK + MLA_ROPE
    w_rw = jnp.pad(wi[:, :lo], ((0, 0), (0, RW_COLS_PAD - RW_COLS))).astype(BF16)
    w_kr = wi[:, hi - MLA_ROPE:hi]
    place = lambda w: jnp.pad(w, ((0, 0), (MLA_NOPE, LANES - MLA_NOPE - MLA_ROPE)))
    w_mla = jnp.concatenate([wi[:, lo:hi - MLA_ROPE], place(w_kr), place(_rope_swap_cols(w_kr))], 1).astype(BF16)
    w_gate = wi[:, hi:].astype(BF16)

    uq = mla_w_uq[l].reshape(MLA_Q_RANK, MLA_HEADS, MLA_QK)
    zeros = jnp.zeros((MLA_Q_RANK, MLA_HEADS, LANES - MLA_QK), F32)
    wqa = jnp.concatenate([uq, zeros], -1).reshape(MLA_Q_RANK, -1).astype(BF16)
    zn = jnp.zeros((MLA_Q_RANK, MLA_HEADS, MLA_NOPE), F32)
    wqs = jnp.concatenate([zn, _rope_swap_cols(uq[..., MLA_NOPE:]), zeros], -1).reshape(MLA_Q_RANK, -1).astype(BF16)
    ukv = mla_w_ukv[l].reshape(MLA_KV_RANK, MLA_HEADS, MLA_NOPE + MLA_V)
    wk = jnp.pad(ukv[..., :MLA_NOPE], ((0, 0), (0, 0), (0, LANES - MLA_NOPE))).reshape(MLA_KV_RANK, -1).astype(BF16)
    wv = ukv[..., MLA_NOPE:].reshape(MLA_KV_RANK, -1).astype(BF16)

    wup = _block_diag2(rk_w_up[l]).astype(BF16)
    aup = _block_diag2(rk_a_up[l]).astype(BF16)
    gup = jnp.pad(_block_diag2(rk_g_up[l]), ((0, LANES - 2 * RW_GATE_LORA), (0, 0))).astype(BF16)
    return dict(w_rw=w_rw, w_mla=w_mla, w_gate=w_gate, wqa=wqa, wqs=wqs, wk=wk, wv=wv, wup=wup, aup=aup, gup=gup)


def _rope_lane_tables(l, c):
    rows = l // GRID_W
    row = jnp.repeat(jnp.arange(rows, dtype=F32), GRID_W)
    col = jnp.tile(jnp.arange(GRID_W, dtype=F32), rows)
    n_freq = MLA_ROPE // 4
    freqs = ROPE_BASE ** (-jnp.arange(n_freq, dtype=F32) / n_freq)
    ang = jnp.concatenate([row[:, None] * freqs, col[:, None] * freqs], -1)
    cos, sin = jnp.cos(ang), jnp.sin(ang)
    tail = LANES - MLA_NOPE - MLA_ROPE
    cos_t = jnp.concatenate([jnp.ones((l, MLA_NOPE), F32), cos, cos, jnp.ones((l, tail), F32)], -1)
    sin_t = jnp.concatenate([jnp.zeros((l, MLA_NOPE), F32), sin, sin, jnp.zeros((l, tail), F32)], -1)
    cos_t = jnp.concatenate([cos_t, jnp.ones((c, LANES), F32)], 0)
    sin_t = jnp.concatenate([sin_t, jnp.zeros((c, LANES), F32)], 0)
    return cos_t, sin_t


def _peer(h2, w_q, keys, u_tab, v_tab, *, ch, tb):
    m, d = h2.shape
    q = _mm(h2, w_q, tm=256, tn=_pick_tile(w_q.shape[1], 1024))
    idx_t, gate_t = _peer_topk(q, keys)
    idx = idx_t.transpose(2, 0, 1).reshape(m, PEER_SEL)
    gate = gate_t.transpose(2, 0, 1).reshape(m, PEER_SEL)
    n_exp = u_tab.shape[0]
    rows, gates, goff, slots = _expert_plan(idx, gate, n_exp, ch)
    lanes = d // ROW_SUB
    out = _experts(rows, gates, goff, slots, h2.reshape(m, ROW_SUB, lanes), u_tab.reshape(n_exp, ROW_SUB, lanes),
                   v_tab.reshape(n_exp, ROW_SUB, lanes), tb=tb, ch=ch)
    return out.reshape(m, d)


def _forward(x, c, ctx, c_ctx, ada_w, ada_b, norm_mix_g, w_in, gate_b, rk_mu, rk_w0, rk_w_up, rk_a0,
             rk_a_up, rk_g_up, rk_k_k, rk_k_a, rk_r_k, rk_ln_w, rk_ln_b, mla_q_norm, mla_w_uq, mla_kv_norm,
             mla_w_ukv, w_branch_a, w_branch_b, w_out, norm_ffn_g, peer_w_q, peer_keys, peer_u, peer_v,
             final_norm_g, *, flash_tq=512, flash_tk=768, scan_tt=32, peer_ch=4096, peer_tb=64):
    nb, l, d = x.shape
    cl = ctx.shape[1]
    t = l + cl
    m = nb * t
    depth = w_in.shape[0]
    dims = dict(B=nb, L=l, C=cl, T=t)
    assert nb == 2 and 4 * nb * RW_HEADS == LANES and l % 256 == 0 and cl % 256 == 0

    xa = jnp.concatenate([x, ctx], 1).reshape(m, d)
    cos_t, sin_t = _rope_lane_tables(l, cl)
    seg = jnp.arange(RW_WIDTH, dtype=I32) // RW_HEAD
    bd = (seg[:, None] == seg[None, :]).astype(BF16)
    cvec = jnp.zeros((SUBLANES, d), F32).at[:nb].set(jax.nn.silu(c)).at[nb].set(jax.nn.silu(c_ctx))

    delta, mods_prev = None, None
    out = None
    for li in range(depth):
        wl = _prep_layer_weights(li, w_in, mla_w_uq, mla_w_ukv, rk_w_up, rk_a_up, rk_g_up)
        tn_ada = _pick_tile(6 * d, 768)
        mods = _mm(cvec, ada_w[li], tm=SUBLANES, tn=tn_ada, epilogue=lambda acc, bias: acc + bias,
                   extras=[(ada_b[li][None, :], pl.BlockSpec((1, tn_ada), lambda j, i: (0, j)))])
        mods = mods[:nb + 1].reshape((nb + 1) * 6, 1, d)
        g_mix = norm_mix_g[li][None, :]
        if li == 0:
            h = _norm_mod(xa, g_mix, mods, 0, 1, dims)
        else:
            xa, h = _resid_norm_mod(xa, delta, mods_prev, 5, g_mix, mods, 0, 1, dims)

        p_rw = _mm(h, wl["w_rw"], tm=256, tn=_pick_tile(RW_COLS_PAD, 1152))
        p_mla = _mm(h, wl["w_mla"], tm=256, tn=MLA_P_COLS)
        p_gate = _mm(h, wl["w_gate"], tm=256, tn=1024)

        mu = jnp.pad(rk_mu[li], (0, RW_COLS_PAD - RW_COLS))[None, :]
        feats = _rwfeat(p_rw.reshape(nb, t, RW_COLS_PAD), mu, rk_w0[li], wl["wup"], rk_a0[li], wl["aup"], wl["gup"],
                        rk_k_k[li][None, :], rk_k_a[li][None, :], rk_r_k[li].reshape(1, RW_WIDTH), bd, dims)
        r, nkk, v, dec0, dec1, b0, b1, ke0, ke1, bon0, bon1, g0, g1 = feats
        a_next = _to_scan_k(nkk, nkk, l)
        a_next = jnp.concatenate([a_next[1:], jnp.zeros_like(a_next[:1])], 0)
        kv = jnp.stack([_to_scan_k(dec0, dec1, l), _to_scan_k(b0, b1, l), _to_scan_k(ke0, ke1, l),
                        _to_scan_k(r, r, l), a_next], 1)
        yn = _scan(kv, _to_scan_v(v, l), tt=scan_tt)
        yn0, yn1 = _from_scan_y(yn, nb, cl)
        flat = lambda z: z.reshape(m, RW_WIDTH)
        ya = _readout(flat(yn0), flat(yn1), flat(bon0), flat(bon1), flat(g0), flat(g1),
                      rk_ln_w[li][None, :], rk_ln_b[li][None, :])

        q, k, vv = _mla_prep(p_mla, cos_t, sin_t, mla_q_norm[li][None, :], mla_kv_norm[li][None, :],
                             wl["wqa"], wl["wqs"], wl["wk"], wl["wv"], dims)
        q3, k3, v3 = (z.reshape(nb, t, -1) for z in (q, k, vv))
        yb = _flash(q3, k3, v3, tq=flash_tq, tk=flash_tk, q_off=0, nq=l // flash_tq, k_off=0, nk=t // flash_tk)
        yb = _flash(q3, k3, v3, tq=cl, tk=cl, q_off=l // cl, nq=1, k_off=l // cl, nk=1, prev_out=yb)
        yb = yb.reshape(m, MLA_HEADS * MLA_V)

        tpb, lt = t // 256, l // 256
        br_a = _mm(ya, w_branch_a[li].astype(BF16), tm=256, tn=1024)
        gb = gate_b[li]
        merged = _mm(
            yb, w_branch_b[li].astype(BF16), tm=256, tn=1024, out_dtype=BF16,
            epilogue=lambda acc, ga, gbb, ba, bb, a: _sigmoid(ga + ba) * a + _sigmoid(gbb + bb) * acc,
            extras=[(p_gate, pl.BlockSpec((256, 1024), lambda j, i: (i, j))),
                    (p_gate, pl.BlockSpec((256, 1024), lambda j, i: (i, j + d // 1024))),
                    (gb[0][None, :], pl.BlockSpec((1, 1024), lambda j, i: (0, j))),
                    (gb[1][None, :], pl.BlockSpec((1, 1024), lambda j, i: (0, j))),
                    (br_a, pl.BlockSpec((256, 1024), lambda j, i: (i, j)))])
        x1 = _mm(merged, w_out[li].astype(BF16), tm=256, tn=1024,
                 epilogue=lambda acc, xr, gt: xr + gt * acc,
                 extras=[(xa, pl.BlockSpec((256, 1024), lambda j, i: (i, j))),
                         (mods, pl.BlockSpec((None, 1, 1024), lambda j, i: (_mod_row(i, tpb, lt, 2, nb), 0, j)))])

        h2 = _norm_mod(x1, norm_ffn_g[li][None, :], mods, 3, 4, dims)
        delta = _peer(h2, peer_w_q[li].astype(BF16), peer_keys[li], peer_u[li].astype(BF16),
                      peer_v[li].astype(BF16), ch=peer_ch, tb=peer_tb)
        xa, mods_prev = x1, mods

    out = _resid_final_norm(xa.reshape(nb, t, d), delta.reshape(nb, t, d), mods_prev, 5,
                            final_norm_g[None, :], dims)
    return out


def kernel(x, c, ctx, c_ctx, ada_w, ada_b, norm_mix_g, w_in, gate_b, rk_mu, rk_w0, rk_w_up, rk_a0, rk_a_up, rk_g_up, rk_k_k, rk_k_a, rk_r_k, rk_ln_w, rk_ln_b, mla_q_norm, mla_w_uq, mla_kv_norm, mla_w_ukv, w_branch_a, w_branch_b, w_out, norm_ffn_g, peer_w_q, peer_keys, peer_u, peer_v, final_norm_g):
    return _forward(x, c, ctx, c_ctx, ada_w, ada_b, norm_mix_g, w_in, gate_b, rk_mu, rk_w0, rk_w_up, rk_a0,
                    rk_a_up, rk_g_up, rk_k_k, rk_k_a, rk_r_k, rk_ln_w, rk_ln_b, mla_q_norm, mla_w_uq,
                    mla_kv_norm, mla_w_ukv, w_branch_a, w_branch_b, w_out, norm_ffn_g, peer_w_q, peer_keys,
                    peer_u, peer_v, final_norm_g)
```

```python
import functools
import math

import jax
import jax.numpy as jnp
from jax import lax
from jax.experimental import pallas as pl
from jax.experimental.pallas import tpu as pltpu

F32 = jnp.float32
BF16 = jnp.bfloat16
I32 = jnp.int32

NORM_EPS = 1e-6
GRID_W = 64
ROPE_BASE = 10000.0

RW_HEADS = 16
RW_HEAD = 64
RW_WIDTH = RW_HEADS * RW_HEAD
RW_DECAY_LORA = 64
RW_A_LORA = 64
RW_GATE_LORA = 32
RW_GN_EPS = 64e-5
RW_COLS = 3 * RW_WIDTH + 2 * (RW_DECAY_LORA + RW_A_LORA + RW_GATE_LORA)
RW_COLS_PAD = 3456

MLA_HEADS = 16
MLA_NOPE = 64
MLA_ROPE = 32
MLA_QK = MLA_NOPE + MLA_ROPE
MLA_V = 64
MLA_Q_RANK = 512
MLA_KV_RANK = 256
MLA_HEAD_PAD = 128
MLA_P_COLS = MLA_Q_RANK + MLA_KV_RANK + 2 * MLA_HEAD_PAD

PEER_HEADS = 8
PEER_NKEYS = 128
PEER_HALF = 128
PEER_TOPK = 16
PEER_SEL = PEER_HEADS * PEER_TOPK

LANES = 128
SUBLANES = 8
VMEM_LIMIT = 56 * 1024 * 1024


def _cparams(sem):
    return pltpu.CompilerParams(dimension_semantics=sem, vmem_limit_bytes=VMEM_LIMIT)


def _pick_tile(n, cap, mult=LANES):
    best = None
    t = mult
    while t <= min(n, cap):
        if n % t == 0:
            best = t
        t += mult
    assert best is not None, (n, cap)
    return best


def _mm_kernel(*refs, n_extra, epilogue):
    a_ref, w_ref = refs[0], refs[1]
    extra = refs[2:2 + n_extra]
    o_ref = refs[2 + n_extra]
    acc = jnp.dot(a_ref[...].astype(BF16), w_ref[...].astype(BF16), preferred_element_type=F32)
    if epilogue is not None:
        acc = epilogue(acc, *[e[...] for e in extra])
    o_ref[...] = acc.astype(o_ref.dtype)


def _mm(a, w, *, tm, tn, out_dtype=F32, epilogue=None, extras=()):
    m, k = a.shape
    k2, n = w.shape
    assert k == k2 and m % tm == 0 and n % tn == 0, (a.shape, w.shape, tm, tn)
    in_specs = [pl.BlockSpec((tm, k), lambda j, i: (i, 0)),
                pl.BlockSpec((k, tn), lambda j, i: (0, j))] + [s for _, s in extras]
    return pl.pallas_call(
        functools.partial(_mm_kernel, n_extra=len(extras), epilogue=epilogue),
        grid=(n // tn, m // tm),
        in_specs=in_specs,
        out_specs=pl.BlockSpec((tm, tn), lambda j, i: (i, j)),
        out_shape=jax.ShapeDtypeStruct((m, n), out_dtype),
        compiler_params=_cparams(("parallel", "parallel")),
    )(a, w, *[x for x, _ in extras])


def _sigmoid(x):
    return 1.0 / (1.0 + jnp.exp(-x))


def _mod_row(i, tiles_per_batch, lat_tiles, part, n_batch):
    which = jnp.where(i % tiles_per_batch >= lat_tiles, n_batch, i // tiles_per_batch)
    return which * 6 + part


def _rms(x, g):
    return x * lax.rsqrt(jnp.mean(x * x, -1, keepdims=True) + NORM_EPS) * g


def _norm_mod_kernel(x_ref, g_ref, sh_ref, sc_ref, h_ref):
    y = _rms(x_ref[...], g_ref[...])
    h_ref[...] = (y * (1.0 + sc_ref[...]) + sh_ref[...]).astype(h_ref.dtype)


def _resid_norm_mod_kernel(x_ref, d_ref, gt_ref, g_ref, sh_ref, sc_ref, xo_ref, h_ref):
    x = x_ref[...] + gt_ref[...] * d_ref[...]
    xo_ref[...] = x
    y = _rms(x, g_ref[...])
    h_ref[...] = (y * (1.0 + sc_ref[...]) + sh_ref[...]).astype(h_ref.dtype)


def _resid_final_norm_kernel(x_ref, d_ref, gt_ref, g_ref, o_ref):
    x = x_ref[...] + gt_ref[...] * d_ref[...]
    o_ref[...] = _rms(x, g_ref[...])


def _mod_spec(d, part, dims, tm):
    tpb, lt, nb = dims["T"] // tm, dims["L"] // tm, dims["B"]
    return pl.BlockSpec((None, 1, d), lambda i: (_mod_row(i, tpb, lt, part, nb), 0, 0))


def _norm_mod(x, g, mods, part_sh, part_sc, dims, tm=256):
    m, d = x.shape
    row = pl.BlockSpec((tm, d), lambda i: (i, 0))
    return pl.pallas_call(
        _norm_mod_kernel, grid=(m // tm,),
        in_specs=[row, pl.BlockSpec((1, d), lambda i: (0, 0)),
                  _mod_spec(d, part_sh, dims, tm), _mod_spec(d, part_sc, dims, tm)],
        out_specs=row, out_shape=jax.ShapeDtypeStruct((m, d), BF16),
        compiler_params=_cparams(("parallel",)),
    )(x, g, mods, mods)


def _resid_norm_mod(x, delta, mods_gt, part_gt, g, mods, part_sh, part_sc, dims, tm=256):
    m, d = x.shape
    row = pl.BlockSpec((tm, d), lambda i: (i, 0))
    return pl.pallas_call(
        _resid_norm_mod_kernel, grid=(m // tm,),
        in_specs=[row, row, _mod_spec(d, part_gt, dims, tm), pl.BlockSpec((1, d), lambda i: (0, 0)),
                  _mod_spec(d, part_sh, dims, tm), _mod_spec(d, part_sc, dims, tm)],
        out_specs=[row, row],
        out_shape=[jax.ShapeDtypeStruct((m, d), F32), jax.ShapeDtypeStruct((m, d), BF16)],
        compiler_params=_cparams(("parallel",)),
    )(x, delta, mods_gt, g, mods, mods)


def _resid_final_norm(x3, delta3, mods, part_gt, g, dims, tm=256):
    b, t, d = x3.shape
    lt = dims["L"] // tm
    blk = pl.BlockSpec((None, tm, d), lambda bi, i: (bi, i, 0))
    return pl.pallas_call(
        _resid_final_norm_kernel, grid=(b, lt),
        in_specs=[blk, blk, pl.BlockSpec((None, 1, d), lambda bi, i: (bi * 6 + part_gt, 0, 0)),
                  pl.BlockSpec((1, d), lambda bi, i: (0, 0))],
        out_specs=blk, out_shape=jax.ShapeDtypeStruct((b, dims["L"], d), F32),
        compiler_params=_cparams(("parallel", "parallel")),
    )(x3, delta3, mods, g)


def _segsum(x, bd):
    hi = x.astype(BF16)
    lo = (x - hi.astype(F32)).astype(BF16)
    return (jnp.dot(hi, bd, preferred_element_type=F32) + jnp.dot(lo, bd, preferred_element_type=F32))


def _rwfeat_kernel(p_ref, pp_ref, pn_ref, mu_ref, w0_ref, wup_ref, a0_ref, aup_ref, gup_ref, kk_ref, ka_ref,
                   rk_ref, bd_ref,
                   r_ref, nkk_ref, v_ref, dec0_ref, dec1_ref, b0_ref, b1_ref, ke0_ref, ke1_ref,
                   bon0_ref, bon1_ref, g0_ref, g1_ref, *, tm, lat_tiles, all_tiles):
    i = pl.program_id(1)
    p = p_ref[...]
    prev_ok = jnp.logical_and(i != 0, i != lat_tiles)
    next_ok = jnp.logical_and(i != lat_tiles - 1, i != all_tiles - 1)
    prow = jnp.where(prev_ok, pp_ref[SUBLANES - 1:SUBLANES, :], 0.0)
    nrow = jnp.where(next_ok, pn_ref[0:1, :], 0.0)
    rid = lax.broadcasted_iota(I32, (tm, 1), 0)
    prev = jnp.where(rid == 0, prow, pltpu.roll(p, 1, 0))
    nxt = jnp.where(rid == tm - 1, nrow, pltpu.roll(p, tm - 1, 0))
    ps = p + (0.5 * (prev + nxt) - p) * mu_ref[...]

    w_ = RW_WIDTH
    r = ps[:, 0:w_]
    k = ps[:, w_:2 * w_]
    v = ps[:, 2 * w_:3 * w_]
    wd = ps[:, 3 * w_:3 * w_ + LANES]
    ad = ps[:, 3 * w_ + LANES:3 * w_ + 2 * LANES]
    gd = ps[:, 3 * w_ + 2 * LANES:3 * w_ + 3 * LANES]
    bd = bd_ref[...]

    lw = jnp.dot(jnp.tanh(wd).astype(BF16), wup_ref[...], preferred_element_type=F32)
    la = jnp.dot(ad.astype(BF16), aup_ref[...], preferred_element_type=F32)
    lg = jnp.dot(_sigmoid(gd).astype(BF16), gup_ref[...], preferred_element_type=F32)

    kk = k * kk_ref[...]
    kk = kk * lax.rsqrt(_segsum(kk * kk, bd) + 1e-12)
    r_ref[...] = r
    nkk_ref[...] = -kk
    v_ref[...] = v
    rk = rk_ref[...]
    ka = ka_ref[...]
    outs = ((dec0_ref, b0_ref, ke0_ref, bon0_ref, g0_ref), (dec1_ref, b1_ref, ke1_ref, bon1_ref, g1_ref))
    for d in range(2):
        dec_ref, b_ref, ke_ref, bon_ref, g_ref = outs[d]
        z = w0_ref[d:d + 1, :] + lw[:, d * w_:(d + 1) * w_]
        nz = -z
        softplus = jnp.maximum(nz, 0.0) + jnp.log(1.0 + jnp.exp(-jnp.abs(nz)))
        wlog = -softplus - 0.5
        dec_ref[...] = jnp.exp(-jnp.exp(wlog))
        a = _sigmoid(a0_ref[d:d + 1, :] + la[:, d * w_:(d + 1) * w_])
        b_ref[...] = kk * a
        ke = k * (1.0 + (a - 1.0) * ka)
        ke_ref[...] = ke
        bon_ref[...] = _segsum(r * ke * rk, bd) * v
        g_ref[...] = lg[:, d * w_:(d + 1) * w_]


def _rwfeat(p3, mu, w0, wup, a0, aup, gup, k_k, k_a, r_k, bd, dims, tm=128):
    b, t, wc = p3.shape
    lt, at = dims["L"] // tm, t // tm
    hb = tm // SUBLANES
    nblk8 = t // SUBLANES
    w_ = RW_WIDTH
    full = lambda arr: pl.BlockSpec(arr.shape, lambda bi, i: (0,) * arr.ndim)
    out_blk = pl.BlockSpec((None, tm, w_), lambda bi, i: (bi, i, 0))
    n_out = 13
    return pl.pallas_call(
        functools.partial(_rwfeat_kernel, tm=tm, lat_tiles=lt, all_tiles=at),
        grid=(b, at),
        in_specs=[pl.BlockSpec((None, tm, wc), lambda bi, i: (bi, i, 0)),
                  pl.BlockSpec((None, SUBLANES, wc), lambda bi, i: (bi, jnp.maximum(i * hb - 1, 0), 0)),
                  pl.BlockSpec((None, SUBLANES, wc), lambda bi, i: (bi, jnp.minimum((i + 1) * hb, nblk8 - 1), 0)),
                  full(mu), full(w0), full(wup), full(a0), full(aup), full(gup), full(k_k), full(k_a),
                  full(r_k), full(bd)],
        out_specs=[out_blk] * n_out,
        out_shape=[jax.ShapeDtypeStruct((b, t, w_), F32)] * n_out,
        compiler_params=_cparams(("parallel", "parallel")),
    )(p3, p3, p3, mu, w0, wup, a0, aup, gup, k_k, k_a, r_k, bd)


SCAN_KH = RW_HEAD // 2


def _scan_kernel(kv_ref, v_ref, y_ref, s_ref, sa_ref, *, tt):
    @pl.when(pl.program_id(0) == 0)
    def _():
        s_ref[...] = jnp.zeros_like(s_ref)
        sa_ref[...] = jnp.zeros_like(sa_ref)

    lane = lax.broadcasted_iota(I32, (RW_HEAD, LANES), 1)

    def one_step(t, sa):
        vt = v_ref[t]
        yacc = jnp.zeros((RW_HEAD, LANES), F32)
        sn = jnp.zeros((RW_HEAD, LANES), F32)
        for k in range(SCAN_KH):
            w = kv_ref[t, 0, pl.ds(k, 1), :]
            b = kv_ref[t, 1, pl.ds(k, 1), :]
            ke = kv_ref[t, 2, pl.ds(k, 1), :]
            r = kv_ref[t, 3, pl.ds(k, 1), :]
            an = kv_ref[t, 4, pl.ds(k, 1), :]
            s = s_ref[k] * w + (sa * b + vt * ke)
            s_ref[k] = s
            yacc = yacc + s * r
            sn = sn + s * an
        y = yacc + pltpu.roll(yacc, LANES // 2, 1)
        sa_new = sn + pltpu.roll(sn, LANES // 2, 1)
        mean = jnp.mean(y, axis=0, keepdims=True)
        dlt = y - mean
        var = jnp.mean(dlt * dlt, axis=0, keepdims=True)
        return sa_new, dlt * lax.rsqrt(var + RW_GN_EPS)

    def body(j, sa):
        sa, y0 = one_step(2 * j, sa)
        sa, y1 = one_step(2 * j + 1, sa)
        y_ref[j] = jnp.where(lane < LANES // 2, y0, y1)
        return sa

    sa_ref[...] = lax.fori_loop(0, tt // 2, body, sa_ref[...])


def _scan(kv, vv, tt=32):
    t = kv.shape[0]
    assert t % tt == 0 and tt % 2 == 0
    return pl.pallas_call(
        functools.partial(_scan_kernel, tt=tt),
        grid=(t // tt,),
        in_specs=[pl.BlockSpec((tt, 5, SCAN_KH, LANES), lambda i: (i, 0, 0, 0)),
                  pl.BlockSpec((tt, RW_HEAD, LANES), lambda i: (i, 0, 0))],
        out_specs=pl.BlockSpec((tt // 2, RW_HEAD, LANES), lambda i: (i, 0, 0)),
        out_shape=jax.ShapeDtypeStruct((t // 2, RW_HEAD, LANES), F32),
        scratch_shapes=[pltpu.VMEM((SCAN_KH, RW_HEAD, LANES), F32), pltpu.VMEM((RW_HEAD, LANES), F32)],
        compiler_params=_cparams(("arbitrary",)),
    )(kv, vv)


def _seq_order(a, l, reverse):
    lat, ctx = a[:l], a[l:]
    if reverse:
        lat, ctx = jnp.flip(lat, 0), jnp.flip(ctx, 0)
    return jnp.concatenate([ctx, lat], 0)


def _seq_unorder(a, c, reverse):
    ctx, lat = a[:c], a[c:]
    if reverse:
        lat, ctx = jnp.flip(lat, 0), jnp.flip(ctx, 0)
    return jnp.concatenate([lat, ctx], 0)


def _lay_k(x):
    b, t, _ = x.shape
    return x.reshape(b, t, RW_HEADS, 2, SCAN_KH).transpose(1, 4, 3, 0, 2).reshape(t, SCAN_KH, 2 * b * RW_HEADS)


def _to_scan_k(x0, x1, l):
    s0 = _seq_order(_lay_k(x0), l, False)
    s1 = _seq_order(_lay_k(x0 if x1 is x0 else x1), l, True)
    q = s0.shape[-1] // 2
    return jnp.concatenate([s0[..., :q], s1[..., :q], s0[..., q:], s1[..., q:]], -1)


def _to_scan_v(x, l):
    b, t, _ = x.shape
    s = x.reshape(b, t, RW_HEADS, RW_HEAD).transpose(1, 3, 0, 2).reshape(t, RW_HEAD, b * RW_HEADS)
    s0, s1 = _seq_order(s, l, False), _seq_order(s, l, True)
    return jnp.concatenate([s0, s1, s0, s1], -1)


def _from_scan_y(y, b, c):
    t2 = y.shape[0]
    q = b * RW_HEADS
    s = y.reshape(t2, RW_HEAD, 2, 2 * q).transpose(0, 2, 1, 3).reshape(2 * t2, RW_HEAD, 2 * q)
    outs = []
    for d in range(2):
        sd = _seq_unorder(s[..., d * q:(d + 1) * q], c, d == 1)
        outs.append(sd.reshape(2 * t2, RW_HEAD, b, RW_HEADS).transpose(2, 0, 3, 1).reshape(b, 2 * t2, RW_WIDTH))
    return outs[0], outs[1]


def _readout_kernel(y0_ref, y1_ref, bon0_ref, bon1_ref, g0_ref, g1_ref, lw_ref, lb_ref, o_ref):
    lw, lb = lw_ref[...], lb_ref[...]
    o = (y0_ref[...] * lw + lb + bon0_ref[...]) * g0_ref[...]
    o = o + (y1_ref[...] * lw + lb + bon1_ref[...]) * g1_ref[...]
    o_ref[...] = o.astype(o_ref.dtype)


def _readout(y0, y1, bon0, bon1, g0, g1, ln_w, ln_b, tm=256):
    m, w_ = y0.shape
    row = pl.BlockSpec((tm, w_), lambda i: (i, 0))
    vec = pl.BlockSpec((1, w_), lambda i: (0, 0))
    return pl.pallas_call(
        _readout_kernel, grid=(m // tm,), in_specs=[row] * 6 + [vec, vec], out_specs=row,
        out_shape=jax.ShapeDtypeStruct((m, w_), BF16), compiler_params=_cparams(("parallel",)),
    )(y0, y1, bon0, bon1, g0, g1, ln_w, ln_b)


def _mla_prep_kernel(p_ref, c_ref, s_ref, qn_ref, kn_ref, wqa_ref, wqs_ref, wk_ref, wv_ref, q_ref, k_ref, v_ref):
    p = p_ref[...]
    cos, sin = c_ref[...], s_ref[...]
    qc = _rms(p[:, :MLA_Q_RANK], qn_ref[...]).astype(BF16)
    kvc = _rms(p[:, MLA_Q_RANK:MLA_Q_RANK + MLA_KV_RANK], kn_ref[...]).astype(BF16)
    kr = p[:, MLA_Q_RANK + MLA_KV_RANK:MLA_Q_RANK + MLA_KV_RANK + LANES]
    krs = p[:, MLA_Q_RANK + MLA_KV_RANK + LANES:MLA_Q_RANK + MLA_KV_RANK + 2 * LANES]
    krope = kr * cos + krs * sin
    qa = jnp.dot(qc, wqa_ref[...], preferred_element_type=F32)
    qs = jnp.dot(qc, wqs_ref[...], preferred_element_type=F32)
    kk = jnp.dot(kvc, wk_ref[...], preferred_element_type=F32)
    v_ref[...] = jnp.dot(kvc, wv_ref[...], preferred_element_type=F32).astype(v_ref.dtype)
    scale = MLA_QK ** -0.5
    for h in range(MLA_HEADS):
        sl = slice(h * LANES, (h + 1) * LANES)
        q_ref[:, sl] = ((qa[:, sl] * cos + qs[:, sl] * sin) * scale).astype(q_ref.dtype)
        k_ref[:, sl] = (kk[:, sl] + krope).astype(k_ref.dtype)


def _mla_prep(p, cos_t, sin_t, q_norm, kv_norm, wqa, wqs, wk, wv, dims, tm=256):
    m, pc = p.shape
    tpb = dims["T"] // tm
    full = lambda arr: pl.BlockSpec(arr.shape, lambda i: (0,) * arr.ndim)
    tab = pl.BlockSpec((tm, LANES), lambda i: (i % tpb, 0))
    hw = MLA_HEADS * LANES
    return pl.pallas_call(
        _mla_prep_kernel, grid=(m // tm,),
        in_specs=[pl.BlockSpec((tm, pc), lambda i: (i, 0)), tab, tab, full(q_norm), full(kv_norm),
                  full(wqa), full(wqs), full(wk), full(wv)],
        out_specs=[pl.BlockSpec((tm, hw), lambda i: (i, 0)), pl.BlockSpec((tm, hw), lambda i: (i, 0)),
                   pl.BlockSpec((tm, MLA_HEADS * MLA_V), lambda i: (i, 0))],
        out_shape=[jax.ShapeDtypeStruct((m, hw), BF16), jax.ShapeDtypeStruct((m, hw), BF16),
                   jax.ShapeDtypeStruct((m, MLA_HEADS * MLA_V), BF16)],
        compiler_params=_cparams(("parallel",)),
    )(p, cos_t, sin_t, q_norm, kv_norm, wqa, wqs, wk, wv)


def _flash_kernel(q_ref, k_ref, v_ref, *rest, nk):
    o_ref = rest[-7]
    state = (rest[-6:-3], rest[-3:])
    ki = pl.program_id(3)

    @pl.when(ki == 0)
    def _():
        for m_ref, l_ref, acc_ref in state:
            m_ref[...] = jnp.full_like(m_ref, -jnp.inf)
            l_ref[...] = jnp.zeros_like(l_ref)
            acc_ref[...] = jnp.zeros_like(acc_ref)

    v = v_ref[...]
    for hh, (m_ref, l_ref, acc_ref) in enumerate(state):
        q = q_ref[:, hh * LANES:(hh + 1) * LANES]
        k = k_ref[:, hh * LANES:(hh + 1) * LANES]
        s = lax.dot_general(q, k, (((1,), (1,)), ((), ())), preferred_element_type=F32)
        m_prev = m_ref[...]
        m_new = jnp.maximum(m_prev, jnp.max(s, -1, keepdims=True))
        alpha = jnp.exp(m_prev - m_new)
        pr = jnp.exp(s - m_new)
        l_ref[...] = alpha * l_ref[...] + jnp.sum(pr, -1, keepdims=True)
        acc_ref[...] = alpha * acc_ref[...] + jnp.dot(pr.astype(BF16), v, preferred_element_type=F32)
        m_ref[...] = m_new

    @pl.when(ki == nk - 1)
    def _():
        (_, l0, acc0), (_, l1, acc1) = state
        lane = lax.broadcasted_iota(I32, acc0.shape, 1)
        o_ref[...] = jnp.where(lane < MLA_V, acc0[...] / l0[...], acc1[...] / l1[...]).astype(o_ref.dtype)


def _flash(q3, k3, v3, *, tq, tk, q_off, nq, k_off, nk, prev_out=None):
    b, t, _ = q3.shape
    hp = MLA_HEADS // 2
    in_specs = [pl.BlockSpec((None, tq, 2 * LANES), lambda bi, h, qi, ki: (bi, q_off + qi, h)),
                pl.BlockSpec((None, tk, 2 * LANES), lambda bi, h, qi, ki: (bi, k_off + ki, h)),
                pl.BlockSpec((None, tk, LANES), lambda bi, h, qi, ki: (bi, k_off + ki, h))]
    args = [q3, k3, v3]
    aliases = {}
    if prev_out is not None:
        in_specs.append(pl.BlockSpec(memory_space=pl.ANY))
        args.append(prev_out)
        aliases = {3: 0}
    return pl.pallas_call(
        functools.partial(_flash_kernel, nk=nk),
        grid=(b, hp, nq, nk),
        in_specs=in_specs,
        out_specs=pl.BlockSpec((None, tq, LANES), lambda bi, h, qi, ki: (bi, q_off + qi, h)),
        out_shape=jax.ShapeDtypeStruct((b, t, MLA_HEADS * MLA_V), BF16),
        scratch_shapes=[pltpu.VMEM((tq, 1), F32), pltpu.VMEM((tq, 1), F32), pltpu.VMEM((tq, LANES), F32)] * 2,
        input_output_aliases=aliases,
        compiler_params=_cparams(("parallel", "parallel", "parallel", "arbitrary")),
    )(*args)


def _topk_rows(s, kk, payload=None):
    nrow = s.shape[0]
    rid = lax.broadcasted_iota(I32, s.shape, 0)
    vals, sel = [], []
    for _ in range(kk):
        m = jnp.max(s, axis=0, keepdims=True)
        pos = jnp.min(jnp.where(s == m, rid, nrow), axis=0, keepdims=True)
        hit = rid == pos
        vals.append(m)
        sel.append(pos if payload is None else jnp.max(jnp.where(hit, payload, -1), axis=0, keepdims=True))
        s = jnp.where(hit, -jnp.inf, s)
    return jnp.concatenate(vals, 0), jnp.concatenate(sel, 0)


def _peer_topk_kernel(q0_ref, q1_ref, keys_ref, idx_ref, gate_ref):
    nt = (((1,), (1,)), ((), ()))
    s0 = lax.dot_general(keys_ref[0].astype(BF16), q0_ref[...].astype(BF16), nt, preferred_element_type=F32)
    s1 = lax.dot_general(keys_ref[1].astype(BF16), q1_ref[...].astype(BF16), nt, preferred_element_type=F32)
    v1, i1 = _topk_rows(s0, PEER_TOPK)
    v2, i2 = _topk_rows(s1, PEER_TOPK)
    cand_s = jnp.concatenate([v1[a:a + 1, :] + v2 for a in range(PEER_TOPK)], 0)
    cand_i = jnp.concatenate([i1[a:a + 1, :] * PEER_NKEYS + i2 for a in range(PEER_TOPK)], 0)
    top_s, top_i = _topk_rows(cand_s, PEER_TOPK, payload=cand_i)
    e = jnp.exp(top_s - top_s[0:1, :])
    idx_ref[...] = top_i
    gate_ref[...] = e / jnp.sum(e, axis=0, keepdims=True)


def _peer_topk(q, keys, tm=256):
    m = q.shape[0]
    out_blk = pl.BlockSpec((None, PEER_TOPK, tm), lambda i, h: (h, 0, i))
    return pl.pallas_call(
        _peer_topk_kernel, grid=(m // tm, PEER_HEADS),
        in_specs=[pl.BlockSpec((tm, PEER_HALF), lambda i, h: (i, 2 * h)),
                  pl.BlockSpec((tm, PEER_HALF), lambda i, h: (i, 2 * h + 1)),
                  pl.BlockSpec((None, 2, PEER_NKEYS, PEER_HALF), lambda i, h: (h, 0, 0, 0))],
        out_specs=[out_blk, out_blk],
        out_shape=[jax.ShapeDtypeStruct((PEER_HEADS, PEER_TOPK, m), I32),
                   jax.ShapeDtypeStruct((PEER_HEADS, PEER_TOPK, m), F32)],
        compiler_params=_cparams(("parallel", "parallel")),
    )(q, q, keys)


PEER_GROUP = 16
ROW_SUB = 16
PLAN_OFF_W = 16
PLAN_BATCH = 16
BITREV8 = (0, 4, 2, 6, 1, 5, 3, 7)


def _gelu(x):
    return 0.5 * x * (1.0 + lax.erf(x * (2.0 ** -0.5)))


def _fold_pair(xa, xb, k, sub):
    mask = (sub & (2 * k - 1)) < k
    a = jnp.where(mask, xa, xb)
    if 2 * k == SUBLANES:
        return a + pltpu.roll(jnp.where(mask, xb, xa), k, 0)
    return a + jnp.where(mask, pltpu.roll(xa, SUBLANES - k, 0), pltpu.roll(xb, k, 0))


def _expert_kernel(rows_ref, gates_ref, goff_ref, h_ref, u_ref, v_ref, oin_ref, o_ref, part_ref, *, tb, gpt):
    c = pl.program_id(0)
    g8 = SUBLANES
    spg = PEER_GROUP // g8
    sub = lax.broadcasted_iota(I32, (g8, LANES), 0)

    @pl.when(jnp.logical_and(c == 0, pl.program_id(1) == 0))
    def _():
        part_ref[...] = jnp.zeros_like(part_ref)

    def groups(n):
        return goff_ref[n * PLAN_OFF_W + c], goff_ref[n * PLAN_OFF_W + c + 1]

    def tok_a(n, slot):
        g0, g1 = groups(n)
        h = h_ref[n].astype(F32)

        def grp(g, slot):
            for s in range(spg):
                base = (n * gpt + g) * PEER_GROUP + s * g8
                xs = []
                for i in range(g8):
                    prod = h * u_ref[rows_ref[base + i]].astype(F32)
                    xs.append(prod[0:SUBLANES] + prod[SUBLANES:ROW_SUB])
                xs = [xs[BITREV8[j]] for j in range(g8)]
                k = g8 // 2
                while k >= 1:
                    xs = [_fold_pair(xs[2 * j], xs[2 * j + 1], k, sub) for j in range(len(xs) // 2)]
                    k //= 2
                part_ref[slot + s] = xs[0]
            return slot + spg

        return lax.fori_loop(g0, g1, grp, slot)

    nslot = lax.fori_loop(0, tb, tok_a, 0)

    def slots_b(j, carry):
        sl = pl.ds(pl.multiple_of(j * PLAN_BATCH, PLAN_BATCH), PLAN_BATCH)
        act = jnp.sum(part_ref[sl], axis=-1, keepdims=True)
        part_ref[sl] = jnp.broadcast_to(_gelu(act), (PLAN_BATCH, g8, LANES))
        return carry

    lax.fori_loop(0, (nslot + PLAN_BATCH - 1) // PLAN_BATCH, slots_b, 0)

    def tok_c(n, slot):
        g0, g1 = groups(n)

        def grp(g, carry):
            slot, acc = carry
            for s in range(spg):
                base = (n * gpt + g) * PEER_GROUP + s * g8
                w8 = part_ref[slot + s]
                for i in range(g8):
                    w = w8[i:i + 1, :] * gates_ref[base + i]
                    acc = acc + w * v_ref[rows_ref[base + i]].astype(F32)
            return slot + spg, acc

        slot, acc = lax.fori_loop(g0, g1, grp, (slot, jnp.zeros((ROW_SUB, LANES), F32)))
        o_ref[n] = oin_ref[n] + acc
        return slot

    lax.fori_loop(0, tb, tok_c, 0)


def _expert_plan(idx, gate, n_exp, ch):
    m, nsel = idx.shape
    nch = n_exp // ch
    g8 = PEER_GROUP
    nfill = g8 - 1
    slots = -(-(nsel + nfill * nch) // g8) * g8
    cidx = idx // ch
    local = idx - cidx * ch
    chunks = jnp.arange(nch, dtype=I32)
    cnt = jnp.sum((cidx[:, :, None] == chunks[None, None, :]).astype(I32), axis=1)
    pad = (-cnt) % g8
    goff = jnp.cumsum((cnt + pad) // g8, axis=1)
    goff = jnp.concatenate([jnp.zeros((m, 1), I32), goff, jnp.zeros((m, PLAN_OFF_W - nch - 1), I32)], 1)
    span = 2 * ch
    fill_c = jnp.repeat(chunks, nfill)
    fill_r = jnp.tile(jnp.arange(nfill, dtype=I32), nch)
    active = fill_r[None, :] < pad[:, fill_c]
    fill_key = jnp.where(active, fill_c[None, :] * span + ch + fill_r[None, :], nch * span)
    ndead = slots - nsel - nfill * nch
    keys = jnp.concatenate([cidx * span + local, fill_key, jnp.full((m, ndead), nch * span, I32)], 1)
    zeros = jnp.zeros((m, slots - nsel), I32)
    rows = jnp.concatenate([local, zeros], 1)
    gates = jnp.concatenate([gate, zeros.astype(F32)], 1)
    _, rows, gates = lax.sort((keys, rows, gates), dimension=1, num_keys=1)
    return rows.reshape(-1), gates.reshape(-1), goff.reshape(-1), slots


def _experts(rows, gates, goff, slots, h3, u3, v3, *, tb, ch):
    m = h3.shape[0]
    e = u3.shape[0]
    gpt = slots // PEER_GROUP
    assert h3.shape[-1] == LANES and e % ch == 0 and m % tb == 0
    smem = lambda w: pl.BlockSpec((tb * w,), lambda c, t: (t,), memory_space=pltpu.SMEM)
    tok_blk = pl.BlockSpec((tb, ROW_SUB, LANES), lambda c, t: (t, 0, 0))
    tab_blk = pl.BlockSpec((ch, ROW_SUB, LANES), lambda c, t: (c, 0, 0), pipeline_mode=pl.Buffered(1))
    out0 = jnp.zeros((m, ROW_SUB, LANES), F32)
    return pl.pallas_call(
        functools.partial(_expert_kernel, tb=tb, gpt=gpt),
        grid=(e // ch, m // tb),
        in_specs=[smem(slots), smem(slots), smem(PLAN_OFF_W), tok_blk, tab_blk, tab_blk, tok_blk],
        out_specs=tok_blk,
        out_shape=jax.ShapeDtypeStruct((m, ROW_SUB, LANES), F32),
        scratch_shapes=[pltpu.VMEM((tb * slots // SUBLANES + PLAN_BATCH, SUBLANES, LANES), F32)],
        input_output_aliases={6: 0},
        compiler_params=_cparams(("arbitrary", "arbitrary")),
    )(rows, gates, goff, h3, u3, v3, out0)


def _block_diag2(w):
    z = jnp.zeros_like(w[0])
    return jnp.concatenate([jnp.concatenate([w[0], z], 1), jnp.concatenate([z, w[1]], 1)], 0)


def _per_head_pad(w, width, pad_to):
    k = w.shape[0]
    w = w.reshape(k, -1, width)
    return jnp.pad(w, ((0, 0), (0, 0), (0, pad_to - width))).reshape(k, -1)


def _rope_swap_cols(w_rope):
    half = MLA_ROPE // 2
    return jnp.concatenate([-w_rope[..., half:], w_rope[..., :half]], -1)


def _prep_layer_weights(l, w_in, mla_w_uq, mla_w_ukv, rk_w_up, rk_a_up, rk_g_up):
    d = w_in.shape[1]
    wi = w_in[l]
    lo, hi = RW_COLS, RW_COLS + MLA_Q_RANK + MLA_KV_RANK + MLA_ROPE
    w_rw = jnp.pad(wi[:, :lo], ((0, 0), (0, RW_COLS_PAD - RW_COLS))).astype(BF16)
    w_kr = wi[:, hi - MLA_ROPE:hi]
    place = lambda w: jnp.pad(w, ((0, 0), (MLA_NOPE, LANES - MLA_NOPE - MLA_ROPE)))
    w_mla = jnp.concatenate([wi[:, lo:hi - MLA_ROPE], place(w_kr), place(_rope_swap_cols(w_kr))], 1).astype(BF16)
    w_gate = wi[:, hi:].astype(BF16)

    uq = mla_w_uq[l].reshape(MLA_Q_RANK, MLA_HEADS, MLA_QK)
    zeros = jnp.zeros((MLA_Q_RANK, MLA_HEADS, LANES - MLA_QK), F32)
    wqa = jnp.concatenate([uq, zeros], -1).reshape(MLA_Q_RANK, -1).astype(BF16)
    zn = jnp.zeros((MLA_Q_RANK, MLA_HEADS, MLA_NOPE), F32)
    wqs = jnp.concatenate([zn, _rope_swap_cols(uq[..., MLA_NOPE:]), zeros], -1).reshape(MLA_Q_RANK, -1).astype(BF16)
    ukv = mla_w_ukv[l].reshape(MLA_KV_RANK, MLA_HEADS, MLA_NOPE + MLA_V)
    wk = jnp.pad(ukv[..., :MLA_NOPE], ((0, 0), (0, 0), (0, LANES - MLA_NOPE))).reshape(MLA_KV_RANK, -1).astype(BF16)
    wv = ukv[..., MLA_NOPE:].reshape(MLA_KV_RANK, -1).astype(BF16)

    wup = _block_diag2(rk_w_up[l]).astype(BF16)
    aup = _block_diag2(rk_a_up[l]).astype(BF16)
    gup = jnp.pad(_block_diag2(rk_g_up[l]), ((0, LANES - 2 * RW_GATE_LORA), (0, 0))).astype(BF16)
    return dict(w_rw=w_rw, w_mla=w_mla, w_gate=w_gate, wqa=wqa, wqs=wqs, wk=wk, wv=wv, wup=wup, aup=aup, gup=gup)


def _rope_lane_tables(l, c):
    rows = l // GRID_W
    row = jnp.repeat(jnp.arange(rows, dtype=F32), GRID_W)
    col = jnp.tile(jnp.arange(GRID_W, dtype=F32), rows)
    n_freq = MLA_ROPE // 4
    freqs = ROPE_BASE ** (-jnp.arange(n_freq, dtype=F32) / n_freq)
    ang = jnp.concatenate([row[:, None] * freqs, col[:, None] * freqs], -1)
    cos, sin = jnp.cos(ang), jnp.sin(ang)
    tail = LANES - MLA_NOPE - MLA_ROPE
    cos_t = jnp.concatenate([jnp.ones((l, MLA_NOPE), F32), cos, cos, jnp.ones((l, tail), F32)], -1)
    sin_t = jnp.concatenate([jnp.zeros((l, MLA_NOPE), F32), sin, sin, jnp.zeros((l, tail), F32)], -1)
    cos_t = jnp.concatenate([cos_t, jnp.ones((c, LANES), F32)], 0)
    sin_t = jnp.concatenate([sin_t, jnp.zeros((c, LANES), F32)], 0)
    return cos_t, sin_t


def _peer(h2, w_q, keys, u_tab, v_tab, *, ch, tb):
    m, d = h2.shape
    q = _mm(h2, w_q, tm=256, tn=_pick_tile(w_q.shape[1], 1024))
    idx_t, gate_t = _peer_topk(q, keys)
    idx = idx_t.transpose(2, 0, 1).reshape(m, PEER_SEL)
    gate = gate_t.transpose(2, 0, 1).reshape(m, PEER_SEL)
    n_exp = u_tab.shape[0]
    rows, gates, goff, slots = _expert_plan(idx, gate, n_exp, ch)
    lanes = d // ROW_SUB
    out = _experts(rows, gates, goff, slots, h2.reshape(m, ROW_SUB, lanes), u_tab.reshape(n_exp, ROW_SUB, lanes),
                   v_tab.reshape(n_exp, ROW_SUB, lanes), tb=tb, ch=ch)
    return out.reshape(m, d)


def _forward(x, c, ctx, c_ctx, ada_w, ada_b, norm_mix_g, w_in, gate_b, rk_mu, rk_w0, rk_w_up, rk_a0,
             rk_a_up, rk_g_up, rk_k_k, rk_k_a, rk_r_k, rk_ln_w, rk_ln_b, mla_q_norm, mla_w_uq, mla_kv_norm,
             mla_w_ukv, w_branch_a, w_branch_b, w_out, norm_ffn_g, peer_w_q, peer_keys, peer_u, peer_v,
             final_norm_g, *, flash_tq=512, flash_tk=2816, scan_tt=32, peer_ch=4096, peer_tb=64):
    nb, l, d = x.shape
    cl = ctx.shape[1]
    t = l + cl
    m = nb * t
    depth = w_in.shape[0]
    dims = dict(B=nb, L=l, C=cl, T=t)
    assert nb == 2 and 4 * nb * RW_HEADS == LANES and l % 256 == 0 and cl % 256 == 0

    xa = jnp.concatenate([x, ctx], 1).reshape(m, d)
    cos_t, sin_t = _rope_lane_tables(l, cl)
    seg = jnp.arange(RW_WIDTH, dtype=I32) // RW_HEAD
    bd = (seg[:, None] == seg[None, :]).astype(BF16)
    cvec = jnp.zeros((SUBLANES, d), F32).at[:nb].set(jax.nn.silu(c)).at[nb].set(jax.nn.silu(c_ctx))

    delta, mods_prev = None, None
    out = None
    for li in range(depth):
        wl = _prep_layer_weights(li, w_in, mla_w_uq, mla_w_ukv, rk_w_up, rk_a_up, rk_g_up)
        tn_ada = _pick_tile(6 * d, 768)
        mods = _mm(cvec, ada_w[li], tm=SUBLANES, tn=tn_ada, epilogue=lambda acc, bias: acc + bias,
                   extras=[(ada_b[li][None, :], pl.BlockSpec((1, tn_ada), lambda j, i: (0, j)))])
        mods = mods[:nb + 1].reshape((nb + 1) * 6, 1, d)
        g_mix = norm_mix_g[li][None, :]
        if li == 0:
            h = _norm_mod(xa, g_mix, mods, 0, 1, dims)
        else:
            xa, h = _resid_norm_mod(xa, delta, mods_prev, 5, g_mix, mods, 0, 1, dims)

        p_rw = _mm(h, wl["w_rw"], tm=256, tn=_pick_tile(RW_COLS_PAD, 1152))
        p_mla = _mm(h, wl["w_mla"], tm=256, tn=MLA_P_COLS)
        p_gate = _mm(h, wl["w_gate"], tm=256, tn=1024)

        mu = jnp.pad(rk_mu[li], (0, RW_COLS_PAD - RW_COLS))[None, :]
        feats = _rwfeat(p_rw.reshape(nb, t, RW_COLS_PAD), mu, rk_w0[li], wl["wup"], rk_a0[li], wl["aup"], wl["gup"],
                        rk_k_k[li][None, :], rk_k_a[li][None, :], rk_r_k[li].reshape(1, RW_WIDTH), bd, dims)
        r, nkk, v, dec0, dec1, b0, b1, ke0, ke1, bon0, bon1, g0, g1 = feats
        a_next = _to_scan_k(nkk, nkk, l)
        a_next = jnp.concatenate([a_next[1:], jnp.zeros_like(a_next[:1])], 0)
        kv = jnp.stack([_to_scan_k(dec0, dec1, l), _to_scan_k(b0, b1, l), _to_scan_k(ke0, ke1, l),
                        _to_scan_k(r, r, l), a_next], 1)
        yn = _scan(kv, _to_scan_v(v, l), tt=scan_tt)
        yn0, yn1 = _from_scan_y(yn, nb, cl)
        flat = lambda z: z.reshape(m, RW_WIDTH)
        ya = _readout(flat(yn0), flat(yn1), flat(bon0), flat(bon1), flat(g0), flat(g1),
                      rk_ln_w[li][None, :], rk_ln_b[li][None, :])

        q, k, vv = _mla_prep(p_mla, cos_t, sin_t, mla_q_norm[li][None, :], mla_kv_norm[li][None, :],
                             wl["wqa"], wl["wqs"], wl["wk"], wl["wv"], dims)
        q3, k3, v3 = (z.reshape(nb, t, -1) for z in (q, k, vv))
        yb = _flash(q3, k3, v3, tq=flash_tq, tk=flash_tk, q_off=0, nq=l // flash_tq, k_off=0, nk=t // flash_tk)
        yb = _flash(q3, k3, v3, tq=cl, tk=cl, q_off=l // cl, nq=1, k_off=l // cl, nk=1, prev_out=yb)
        yb = yb.reshape(m, MLA_HEADS * MLA_V)

        tpb, lt = t // 256, l // 256
        br_a = _mm(ya, w_branch_a[li].astype(BF16), tm=256, tn=1024)
        gb = gate_b[li]
        merged = _mm(
            yb, w_branch_b[li].astype(BF16), tm=256, tn=1024, out_dtype=BF16,
            epilogue=lambda acc, ga, gbb, ba, bb, a: _sigmoid(ga + ba) * a + _sigmoid(gbb + bb) * acc,
            extras=[(p_gate, pl.BlockSpec((256, 1024), lambda j, i: (i, j))),
                    (p_gate, pl.BlockSpec((256, 1024), lambda j, i: (i, j + d // 1024))),
                    (gb[0][None, :], pl.BlockSpec((1, 1024), lambda j, i: (0, j))),
                    (gb[1][None, :], pl.BlockSpec((1, 1024), lambda j, i: (0, j))),
                    (br_a, pl.BlockSpec((256, 1024), lambda j, i: (i, j)))])
        x1 = _mm(merged, w_out[li].astype(BF16), tm=256, tn=1024,
                 epilogue=lambda acc, xr, gt: xr + gt * acc,
                 extras=[(xa, pl.BlockSpec((256, 1024), lambda j, i: (i, j))),
                         (mods, pl.BlockSpec((None, 1, 1024), lambda j, i: (_mod_row(i, tpb, lt, 2, nb), 0, j)))])

        h2 = _norm_mod(x1, norm_ffn_g[li][None, :], mods, 3, 4, dims)
        delta = _peer(h2, peer_w_q[li].astype(BF16), peer_keys[li], peer_u[li].astype(BF16),
                      peer_v[li].astype(BF16), ch=peer_ch, tb=peer_tb)
        xa, mods_prev = x1, mods

    out = _resid_final_norm(xa.reshape(nb, t, d), delta.reshape(nb, t, d), mods_prev, 5,
                            final_norm_g[None, :], dims)
    return out


def kernel(x, c, ctx, c_ctx, ada_w, ada_b, norm_mix_g, w_in, gate_b, rk_mu, rk_w0, rk_w_up, rk_a0, rk_a_up, rk_g_up, rk_k_k, rk_k_a, rk_r_k, rk_ln_w, rk_ln_b, mla_q_norm, mla_w_uq, mla_kv_norm, mla_w_ukv, w_branch_a, w_branch_b, w_out, norm_ffn_g, peer_w_q, peer_keys, peer_u, peer_v, final_norm_g):
    return _forward(x, c, ctx, c_ctx, ada_w, ada_b, norm_mix_g, w_in, gate_b, rk_mu, rk_w0, rk_w_up, rk_a0,
                    rk_a_up, rk_g_up, rk_k_k, rk_k_a, rk_r_k, rk_ln_w, rk_ln_b, mla_q_norm, mla_w_uq,
                    mla_kv_norm, mla_w_ukv, w_branch_a, w_branch_b, w_out, norm_ffn_g, peer_w_q, peer_keys,
                    peer_u, peer_v, final_norm_g)
```

```python
import functools
import math

import jax
import jax.numpy as jnp
from jax import lax
from jax.experimental import pallas as pl
from jax.experimental.pallas import tpu as pltpu

F32 = jnp.float32
BF16 = jnp.bfloat16
I32 = jnp.int32

NORM_EPS = 1e-6
GRID_W = 64
ROPE_BASE = 10000.0

RW_HEADS = 16
RW_HEAD = 64
RW_WIDTH = RW_HEADS * RW_HEAD
RW_DECAY_LORA = 64
RW_A_LORA = 64
RW_GATE_LORA = 32
RW_GN_EPS = 64e-5
RW_COLS = 3 * RW_WIDTH + 2 * (RW_DECAY_LORA + RW_A_LORA + RW_GATE_LORA)
RW_COLS_PAD = 3456

MLA_HEADS = 16
MLA_NOPE = 64
MLA_ROPE = 32
MLA_QK = MLA_NOPE + MLA_ROPE
MLA_V = 64
MLA_Q_RANK = 512
MLA_KV_RANK = 256
MLA_HEAD_PAD = 128
MLA_P_COLS = MLA_Q_RANK + MLA_KV_RANK + 2 * MLA_HEAD_PAD

PEER_HEADS = 8
PEER_NKEYS = 128
PEER_HALF = 128
PEER_TOPK = 16
PEER_SEL = PEER_HEADS * PEER_TOPK

LANES = 128
SUBLANES = 8
VMEM_LIMIT = 56 * 1024 * 1024


def _cparams(sem):
    return pltpu.CompilerParams(dimension_semantics=sem, vmem_limit_bytes=VMEM_LIMIT)


def _pick_tile(n, cap, mult=LANES):
    best = None
    t = mult
    while t <= min(n, cap):
        if n % t == 0:
            best = t
        t += mult
    assert best is not None, (n, cap)
    return best


def _mm_kernel(*refs, n_extra, epilogue):
    a_ref, w_ref = refs[0], refs[1]
    extra = refs[2:2 + n_extra]
    o_ref = refs[2 + n_extra]
    acc = jnp.dot(a_ref[...].astype(BF16), w_ref[...].astype(BF16), preferred_element_type=F32)
    if epilogue is not None:
        acc = epilogue(acc, *[e[...] for e in extra])
    o_ref[...] = acc.astype(o_ref.dtype)


def _mm(a, w, *, tm, tn, out_dtype=F32, epilogue=None, extras=()):
    m, k = a.shape
    k2, n = w.shape
    assert k == k2 and m % tm == 0 and n % tn == 0, (a.shape, w.shape, tm, tn)
    in_specs = [pl.BlockSpec((tm, k), lambda j, i: (i, 0)),
                pl.BlockSpec((k, tn), lambda j, i: (0, j))] + [s for _, s in extras]
    return pl.pallas_call(
        functools.partial(_mm_kernel, n_extra=len(extras), epilogue=epilogue),
        grid=(n // tn, m // tm),
        in_specs=in_specs,
        out_specs=pl.BlockSpec((tm, tn), lambda j, i: (i, j)),
        out_shape=jax.ShapeDtypeStruct((m, n), out_dtype),
        compiler_params=_cparams(("parallel", "parallel")),
    )(a, w, *[x for x, _ in extras])


def _sigmoid(x):
    return 1.0 / (1.0 + jnp.exp(-x))


def _mod_row(i, tiles_per_batch, lat_tiles, part, n_batch):
    which = jnp.where(i % tiles_per_batch >= lat_tiles, n_batch, i // tiles_per_batch)
    return which * 6 + part


def _rms(x, g):
    return x * lax.rsqrt(jnp.mean(x * x, -1, keepdims=True) + NORM_EPS) * g


def _norm_mod_kernel(x_ref, g_ref, sh_ref, sc_ref, h_ref):
    y = _rms(x_ref[...], g_ref[...])
    h_ref[...] = (y * (1.0 + sc_ref[...]) + sh_ref[...]).astype(h_ref.dtype)


def _resid_norm_mod_kernel(x_ref, d_ref, gt_ref, g_ref, sh_ref, sc_ref, xo_ref, h_ref):
    x = x_ref[...] + gt_ref[...] * d_ref[...]
    xo_ref[...] = x
    y = _rms(x, g_ref[...])
    h_ref[...] = (y * (1.0 + sc_ref[...]) + sh_ref[...]).astype(h_ref.dtype)


def _resid_final_norm_kernel(x_ref, d_ref, gt_ref, g_ref, o_ref):
    x = x_ref[...] + gt_ref[...] * d_ref[...]
    o_ref[...] = _rms(x, g_ref[...])


def _mod_spec(d, part, dims, tm):
    tpb, lt, nb = dims["T"] // tm, dims["L"] // tm, dims["B"]
    return pl.BlockSpec((None, 1, d), lambda i: (_mod_row(i, tpb, lt, part, nb), 0, 0))


def _norm_mod(x, g, mods, part_sh, part_sc, dims, tm=256):
    m, d = x.shape
    row = pl.BlockSpec((tm, d), lambda i: (i, 0))
    return pl.pallas_call(
        _norm_mod_kernel, grid=(m // tm,),
        in_specs=[row, pl.BlockSpec((1, d), lambda i: (0, 0)),
                  _mod_spec(d, part_sh, dims, tm), _mod_spec(d, part_sc, dims, tm)],
        out_specs=row, out_shape=jax.ShapeDtypeStruct((m, d), BF16),
        compiler_params=_cparams(("parallel",)),
    )(x, g, mods, mods)


def _resid_norm_mod(x, delta, mods_gt, part_gt, g, mods, part_sh, part_sc, dims, tm=256):
    m, d = x.shape
    row = pl.BlockSpec((tm, d), lambda i: (i, 0))
    return pl.pallas_call(
        _resid_norm_mod_kernel, grid=(m // tm,),
        in_specs=[row, row, _mod_spec(d, part_gt, dims, tm), pl.BlockSpec((1, d), lambda i: (0, 0)),
                  _mod_spec(d, part_sh, dims, tm), _mod_spec(d, part_sc, dims, tm)],
        out_specs=[row, row],
        out_shape=[jax.ShapeDtypeStruct((m, d), F32), jax.ShapeDtypeStruct((m, d), BF16)],
        compiler_params=_cparams(("parallel",)),
    )(x, delta, mods_gt, g, mods, mods)


def _resid_final_norm(x3, delta3, mods, part_gt, g, dims, tm=256):
    b, t, d = x3.shape
    lt = dims["L"] // tm
    blk = pl.BlockSpec((None, tm, d), lambda bi, i: (bi, i, 0))
    return pl.pallas_call(
        _resid_final_norm_kernel, grid=(b, lt),
        in_specs=[blk, blk, pl.BlockSpec((None, 1, d), lambda bi, i: (bi * 6 + part_gt, 0, 0)),
                  pl.BlockSpec((1, d), lambda bi, i: (0, 0))],
        out_specs=blk, out_shape=jax.ShapeDtypeStruct((b, dims["L"], d), F32),
        compiler_params=_cparams(("parallel", "parallel")),
    )(x3, delta3, mods, g)


def _segsum(x, bd):
    hi = x.astype(BF16)
    lo = (x - hi.astype(F32)).astype(BF16)
    return (jnp.dot(hi, bd, preferred_element_type=F32) + jnp.dot(lo, bd, preferred_element_type=F32))


def _rwfeat_kernel(p_ref, pp_ref, pn_ref, mu_ref, w0_ref, wup_ref, a0_ref, aup_ref, gup_ref, kk_ref, ka_ref,
                   rk_ref, bd_ref,
                   r_ref, nkk_ref, v_ref, dec0_ref, dec1_ref, b0_ref, b1_ref, ke0_ref, ke1_ref,
                   bon0_ref, bon1_ref, g0_ref, g1_ref, *, tm, lat_tiles, all_tiles):
    i = pl.program_id(1)
    p = p_ref[...]
    prev_ok = jnp.logical_and(i != 0, i != lat_tiles)
    next_ok = jnp.logical_and(i != lat_tiles - 1, i != all_tiles - 1)
    prow = jnp.where(prev_ok, pp_ref[SUBLANES - 1:SUBLANES, :], 0.0)
    nrow = jnp.where(next_ok, pn_ref[0:1, :], 0.0)
    rid = lax.broadcasted_iota(I32, (tm, 1), 0)
    prev = jnp.where(rid == 0, prow, pltpu.roll(p, 1, 0))
    nxt = jnp.where(rid == tm - 1, nrow, pltpu.roll(p, tm - 1, 0))
    ps = p + (0.5 * (prev + nxt) - p) * mu_ref[...]

    w_ = RW_WIDTH
    r = ps[:, 0:w_]
    k = ps[:, w_:2 * w_]
    v = ps[:, 2 * w_:3 * w_]
    wd = ps[:, 3 * w_:3 * w_ + LANES]
    ad = ps[:, 3 * w_ + LANES:3 * w_ + 2 * LANES]
    gd = ps[:, 3 * w_ + 2 * LANES:3 * w_ + 3 * LANES]
    bd = bd_ref[...]

    lw = jnp.dot(jnp.tanh(wd).astype(BF16), wup_ref[...], preferred_element_type=F32)
    la = jnp.dot(ad.astype(BF16), aup_ref[...], preferred_element_type=F32)
    lg = jnp.dot(_sigmoid(gd).astype(BF16), gup_ref[...], preferred_element_type=F32)

    kk = k * kk_ref[...]
    kk = kk * lax.rsqrt(_segsum(kk * kk, bd) + 1e-12)
    r_ref[...] = r
    nkk_ref[...] = -kk
    v_ref[...] = v
    rk = rk_ref[...]
    ka = ka_ref[...]
    outs = ((dec0_ref, b0_ref, ke0_ref, bon0_ref, g0_ref), (dec1_ref, b1_ref, ke1_ref, bon1_ref, g1_ref))
    for d in range(2):
        dec_ref, b_ref, ke_ref, bon_ref, g_ref = outs[d]
        z = w0_ref[d:d + 1, :] + lw[:, d * w_:(d + 1) * w_]
        nz = -z
        softplus = jnp.maximum(nz, 0.0) + jnp.log(1.0 + jnp.exp(-jnp.abs(nz)))
        wlog = -softplus - 0.5
        dec_ref[...] = jnp.exp(-jnp.exp(wlog))
        a = _sigmoid(a0_ref[d:d + 1, :] + la[:, d * w_:(d + 1) * w_])
        b_ref[...] = kk * a
        ke = k * (1.0 + (a - 1.0) * ka)
        ke_ref[...] = ke
        bon_ref[...] = _segsum(r * ke * rk, bd) * v
        g_ref[...] = lg[:, d * w_:(d + 1) * w_]


def _rwfeat(p3, mu, w0, wup, a0, aup, gup, k_k, k_a, r_k, bd, dims, tm=128):
    b, t, wc = p3.shape
    lt, at = dims["L"] // tm, t // tm
    hb = tm // SUBLANES
    nblk8 = t // SUBLANES
    w_ = RW_WIDTH
    full = lambda arr: pl.BlockSpec(arr.shape, lambda bi, i: (0,) * arr.ndim)
    out_blk = pl.BlockSpec((None, tm, w_), lambda bi, i: (bi, i, 0))
    n_out = 13
    return pl.pallas_call(
        functools.partial(_rwfeat_kernel, tm=tm, lat_tiles=lt, all_tiles=at),
        grid=(b, at),
        in_specs=[pl.BlockSpec((None, tm, wc), lambda bi, i: (bi, i, 0)),
                  pl.BlockSpec((None, SUBLANES, wc), lambda bi, i: (bi, jnp.maximum(i * hb - 1, 0), 0)),
                  pl.BlockSpec((None, SUBLANES, wc), lambda bi, i: (bi, jnp.minimum((i + 1) * hb, nblk8 - 1), 0)),
                  full(mu), full(w0), full(wup), full(a0), full(aup), full(gup), full(k_k), full(k_a),
                  full(r_k), full(bd)],
        out_specs=[out_blk] * n_out,
        out_shape=[jax.ShapeDtypeStruct((b, t, w_), F32)] * n_out,
        compiler_params=_cparams(("parallel", "parallel")),
    )(p3, p3, p3, mu, w0, wup, a0, aup, gup, k_k, k_a, r_k, bd)


SCAN_KH = RW_HEAD // 2


def _scan_kernel(kv_ref, v_ref, y_ref, s_ref, sa_ref, *, tt):
    @pl.when(pl.program_id(0) == 0)
    def _():
        s_ref[...] = jnp.zeros_like(s_ref)
        sa_ref[...] = jnp.zeros_like(sa_ref)

    lane = lax.broadcasted_iota(I32, (RW_HEAD, LANES), 1)

    def one_step(t, sa):
        vt = v_ref[t]
        yacc = jnp.zeros((RW_HEAD, LANES), F32)
        sn = jnp.zeros((RW_HEAD, LANES), F32)
        for k in range(SCAN_KH):
            w = kv_ref[t, 0, pl.ds(k, 1), :]
            b = kv_ref[t, 1, pl.ds(k, 1), :]
            ke = kv_ref[t, 2, pl.ds(k, 1), :]
            r = kv_ref[t, 3, pl.ds(k, 1), :]
            an = kv_ref[t, 4, pl.ds(k, 1), :]
            s = s_ref[k] * w + (sa * b + vt * ke)
            s_ref[k] = s
            yacc = yacc + s * r
            sn = sn + s * an
        y = yacc + pltpu.roll(yacc, LANES // 2, 1)
        sa_new = sn + pltpu.roll(sn, LANES // 2, 1)
        mean = jnp.mean(y, axis=0, keepdims=True)
        dlt = y - mean
        var = jnp.mean(dlt * dlt, axis=0, keepdims=True)
        return sa_new, dlt * lax.rsqrt(var + RW_GN_EPS)

    def body(j, sa):
        sa, y0 = one_step(2 * j, sa)
        sa, y1 = one_step(2 * j + 1, sa)
        y_ref[j] = jnp.where(lane < LANES // 2, y0, y1)
        return sa

    sa_ref[...] = lax.fori_loop(0, tt // 2, body, sa_ref[...])


def _scan(kv, vv, tt=32):
    t = kv.shape[0]
    assert t % tt == 0 and tt % 2 == 0
    return pl.pallas_call(
        functools.partial(_scan_kernel, tt=tt),
        grid=(t // tt,),
        in_specs=[pl.BlockSpec((tt, 5, SCAN_KH, LANES), lambda i: (i, 0, 0, 0)),
                  pl.BlockSpec((tt, RW_HEAD, LANES), lambda i: (i, 0, 0))],
        out_specs=pl.BlockSpec((tt // 2, RW_HEAD, LANES), lambda i: (i, 0, 0)),
        out_shape=jax.ShapeDtypeStruct((t // 2, RW_HEAD, LANES), F32),
        scratch_shapes=[pltpu.VMEM((SCAN_KH, RW_HEAD, LANES), F32), pltpu.VMEM((RW_HEAD, LANES), F32)],
        compiler_params=_cparams(("arbitrary",)),
    )(kv, vv)


def _seq_order(a, l, reverse):
    lat, ctx = a[:l], a[l:]
    if reverse:
        lat, ctx = jnp.flip(lat, 0), jnp.flip(ctx, 0)
    return jnp.concatenate([ctx, lat], 0)


def _seq_unorder(a, c, reverse):
    ctx, lat = a[:c], a[c:]
    if reverse:
        lat, ctx = jnp.flip(lat, 0), jnp.flip(ctx, 0)
    return jnp.concatenate([lat, ctx], 0)


def _lay_k(x):
    b, t, _ = x.shape
    return x.reshape(b, t, RW_HEADS, 2, SCAN_KH).transpose(1, 4, 3, 0, 2).reshape(t, SCAN_KH, 2 * b * RW_HEADS)


def _to_scan_k(x0, x1, l):
    s0 = _seq_order(_lay_k(x0), l, False)
    s1 = _seq_order(_lay_k(x0 if x1 is x0 else x1), l, True)
    q = s0.shape[-1] // 2
    return jnp.concatenate([s0[..., :q], s1[..., :q], s0[..., q:], s1[..., q:]], -1)


def _to_scan_v(x, l):
    b, t, _ = x.shape
    s = x.reshape(b, t, RW_HEADS, RW_HEAD).transpose(1, 3, 0, 2).reshape(t, RW_HEAD, b * RW_HEADS)
    s0, s1 = _seq_order(s, l, False), _seq_order(s, l, True)
    return jnp.concatenate([s0, s1, s0, s1], -1)


def _from_scan_y(y, b, c):
    t2 = y.shape[0]
    q = b * RW_HEADS
    s = y.reshape(t2, RW_HEAD, 2, 2 * q).transpose(0, 2, 1, 3).reshape(2 * t2, RW_HEAD, 2 * q)
    outs = []
    for d in range(2):
        sd = _seq_unorder(s[..., d * q:(d + 1) * q], c, d == 1)
        outs.append(sd.reshape(2 * t2, RW_HEAD, b, RW_HEADS).transpose(2, 0, 3, 1).reshape(b, 2 * t2, RW_WIDTH))
    return outs[0], outs[1]


def _readout_kernel(y0_ref, y1_ref, bon0_ref, bon1_ref, g0_ref, g1_ref, lw_ref, lb_ref, o_ref):
    lw, lb = lw_ref[...], lb_ref[...]
    o = (y0_ref[...] * lw + lb + bon0_ref[...]) * g0_ref[...]
    o = o + (y1_ref[...] * lw + lb + bon1_ref[...]) * g1_ref[...]
    o_ref[...] = o.astype(o_ref.dtype)


def _readout(y0, y1, bon0, bon1, g0, g1, ln_w, ln_b, tm=256):
    m, w_ = y0.shape
    row = pl.BlockSpec((tm, w_), lambda i: (i, 0))
    vec = pl.BlockSpec((1, w_), lambda i: (0, 0))
    return pl.pallas_call(
        _readout_kernel, grid=(m // tm,), in_specs=[row] * 6 + [vec, vec], out_specs=row,
        out_shape=jax.ShapeDtypeStruct((m, w_), BF16), compiler_params=_cparams(("parallel",)),
    )(y0, y1, bon0, bon1, g0, g1, ln_w, ln_b)


def _mla_prep_kernel(p_ref, c_ref, s_ref, qn_ref, kn_ref, wqa_ref, wqs_ref, wk_ref, wv_ref, q_ref, k_ref, v_ref):
    p = p_ref[...]
    cos, sin = c_ref[...], s_ref[...]
    qc = _rms(p[:, :MLA_Q_RANK], qn_ref[...]).astype(BF16)
    kvc = _rms(p[:, MLA_Q_RANK:MLA_Q_RANK + MLA_KV_RANK], kn_ref[...]).astype(BF16)
    kr = p[:, MLA_Q_RANK + MLA_KV_RANK:MLA_Q_RANK + MLA_KV_RANK + LANES]
    krs = p[:, MLA_Q_RANK + MLA_KV_RANK + LANES:MLA_Q_RANK + MLA_KV_RANK + 2 * LANES]
    krope = kr * cos + krs * sin
    qa = jnp.dot(qc, wqa_ref[...], preferred_element_type=F32)
    qs = jnp.dot(qc, wqs_ref[...], preferred_element_type=F32)
    kk = jnp.dot(kvc, wk_ref[...], preferred_element_type=F32)
    v_ref[...] = jnp.dot(kvc, wv_ref[...], preferred_element_type=F32).astype(v_ref.dtype)
    scale = MLA_QK ** -0.5
    for h in range(MLA_HEADS):
        sl = slice(h * LANES, (h + 1) * LANES)
        q_ref[:, sl] = ((qa[:, sl] * cos + qs[:, sl] * sin) * scale).astype(q_ref.dtype)
        k_ref[:, sl] = (kk[:, sl] + krope).astype(k_ref.dtype)


def _mla_prep(p, cos_t, sin_t, q_norm, kv_norm, wqa, wqs, wk, wv, dims, tm=256):
    m, pc = p.shape
    tpb = dims["T"] // tm
    full = lambda arr: pl.BlockSpec(arr.shape, lambda i: (0,) * arr.ndim)
    tab = pl.BlockSpec((tm, LANES), lambda i: (i % tpb, 0))
    hw = MLA_HEADS * LANES
    return pl.pallas_call(
        _mla_prep_kernel, grid=(m // tm,),
        in_specs=[pl.BlockSpec((tm, pc), lambda i: (i, 0)), tab, tab, full(q_norm), full(kv_norm),
                  full(wqa), full(wqs), full(wk), full(wv)],
        out_specs=[pl.BlockSpec((tm, hw), lambda i: (i, 0)), pl.BlockSpec((tm, hw), lambda i: (i, 0)),
                   pl.BlockSpec((tm, MLA_HEADS * MLA_V), lambda i: (i, 0))],
        out_shape=[jax.ShapeDtypeStruct((m, hw), BF16), jax.ShapeDtypeStruct((m, hw), BF16),
                   jax.ShapeDtypeStruct((m, MLA_HEADS * MLA_V), BF16)],
        compiler_params=_cparams(("parallel",)),
    )(p, cos_t, sin_t, q_norm, kv_norm, wqa, wqs, wk, wv)


def _flash_kernel(q_ref, k_ref, v_ref, *rest, nk):
    o_ref = rest[-7]
    state = (rest[-6:-3], rest[-3:])
    ki = pl.program_id(3)

    @pl.when(ki == 0)
    def _():
        for m_ref, l_ref, acc_ref in state:
            m_ref[...] = jnp.full_like(m_ref, -jnp.inf)
            l_ref[...] = jnp.zeros_like(l_ref)
            acc_ref[...] = jnp.zeros_like(acc_ref)

    v = v_ref[...]
    for hh, (m_ref, l_ref, acc_ref) in enumerate(state):
        q = q_ref[:, hh * LANES:(hh + 1) * LANES]
        k = k_ref[:, hh * LANES:(hh + 1) * LANES]
        s = lax.dot_general(q, k, (((1,), (1,)), ((), ())), preferred_element_type=F32)
        m_prev = m_ref[...]
        m_new = jnp.maximum(m_prev, jnp.max(s, -1, keepdims=True))
        alpha = jnp.exp(m_prev - m_new)
        pr = jnp.exp(s - m_new)
        l_ref[...] = alpha * l_ref[...] + jnp.sum(pr, -1, keepdims=True)
        acc_ref[...] = alpha * acc_ref[...] + jnp.dot(pr.astype(BF16), v, preferred_element_type=F32)
        m_ref[...] = m_new

    @pl.when(ki == nk - 1)
    def _():
        (_, l0, acc0), (_, l1, acc1) = state
        lane = lax.broadcasted_iota(I32, acc0.shape, 1)
        o_ref[...] = jnp.where(lane < MLA_V, acc0[...] / l0[...], acc1[...] / l1[...]).astype(o_ref.dtype)


def _flash(q3, k3, v3, *, tq, tk, q_off, nq, k_off, nk, prev_out=None):
    b, t, _ = q3.shape
    hp = MLA_HEADS // 2
    in_specs = [pl.BlockSpec((None, tq, 2 * LANES), lambda bi, h, qi, ki: (bi, q_off + qi, h)),
                pl.BlockSpec((None, tk, 2 * LANES), lambda bi, h, qi, ki: (bi, k_off + ki, h)),
                pl.BlockSpec((None, tk, LANES), lambda bi, h, qi, ki: (bi, k_off + ki, h))]
    args = [q3, k3, v3]
    aliases = {}
    if prev_out is not None:
        in_specs.append(pl.BlockSpec(memory_space=pl.ANY))
        args.append(prev_out)
        aliases = {3: 0}
    return pl.pallas_call(
        functools.partial(_flash_kernel, nk=nk),
        grid=(b, hp, nq, nk),
        in_specs=in_specs,
        out_specs=pl.BlockSpec((None, tq, LANES), lambda bi, h, qi, ki: (bi, q_off + qi, h)),
        out_shape=jax.ShapeDtypeStruct((b, t, MLA_HEADS * MLA_V), BF16),
        scratch_shapes=[pltpu.VMEM((tq, 1), F32), pltpu.VMEM((tq, 1), F32), pltpu.VMEM((tq, LANES), F32)] * 2,
        input_output_aliases=aliases,
        compiler_params=_cparams(("parallel", "parallel", "parallel", "arbitrary")),
    )(*args)


def _topk_rows(s, kk, payload=None):
    nrow = s.shape[0]
    rid = lax.broadcasted_iota(I32, s.shape, 0)
    vals, sel = [], []
    for _ in range(kk):
        m = jnp.max(s, axis=0, keepdims=True)
        pos = jnp.min(jnp.where(s == m, rid, nrow), axis=0, keepdims=True)
        hit = rid == pos
        vals.append(m)
        sel.append(pos if payload is None else jnp.max(jnp.where(hit, payload, -1), axis=0, keepdims=True))
        s = jnp.where(hit, -jnp.inf, s)
    return jnp.concatenate(vals, 0), jnp.concatenate(sel, 0)


def _peer_topk_kernel(q0_ref, q1_ref, keys_ref, idx_ref, gate_ref):
    nt = (((1,), (1,)), ((), ()))
    s0 = lax.dot_general(keys_ref[0].astype(BF16), q0_ref[...].astype(BF16), nt, preferred_element_type=F32)
    s1 = lax.dot_general(keys_ref[1].astype(BF16), q1_ref[...].astype(BF16), nt, preferred_element_type=F32)
    v1, i1 = _topk_rows(s0, PEER_TOPK)
    v2, i2 = _topk_rows(s1, PEER_TOPK)
    cand_s = jnp.concatenate([v1[a:a + 1, :] + v2 for a in range(PEER_TOPK)], 0)
    cand_i = jnp.concatenate([i1[a:a + 1, :] * PEER_NKEYS + i2 for a in range(PEER_TOPK)], 0)
    top_s, top_i = _topk_rows(cand_s, PEER_TOPK, payload=cand_i)
    e = jnp.exp(top_s - top_s[0:1, :])
    idx_ref[...] = top_i
    gate_ref[...] = e / jnp.sum(e, axis=0, keepdims=True)


def _peer_topk(q, keys, tm=256):
    m = q.shape[0]
    out_blk = pl.BlockSpec((None, PEER_TOPK, tm), lambda i, h: (h, 0, i))
    return pl.pallas_call(
        _peer_topk_kernel, grid=(m // tm, PEER_HEADS),
        in_specs=[pl.BlockSpec((tm, PEER_HALF), lambda i, h: (i, 2 * h)),
                  pl.BlockSpec((tm, PEER_HALF), lambda i, h: (i, 2 * h + 1)),
                  pl.BlockSpec((None, 2, PEER_NKEYS, PEER_HALF), lambda i, h: (h, 0, 0, 0))],
        out_specs=[out_blk, out_blk],
        out_shape=[jax.ShapeDtypeStruct((PEER_HEADS, PEER_TOPK, m), I32),
                   jax.ShapeDtypeStruct((PEER_HEADS, PEER_TOPK, m), F32)],
        compiler_params=_cparams(("parallel", "parallel")),
    )(q, q, keys)


PEER_GROUP = 16
ROW_SUB = 16
PLAN_OFF_W = 16
PLAN_BATCH = 16
BITREV8 = (0, 4, 2, 6, 1, 5, 3, 7)


def _gelu(x):
    return 0.5 * x * (1.0 + lax.erf(x * (2.0 ** -0.5)))


def _fold_pair(xa, xb, k, sub):
    mask = (sub & (2 * k - 1)) < k
    a = jnp.where(mask, xa, xb)
    if 2 * k == SUBLANES:
        return a + pltpu.roll(jnp.where(mask, xb, xa), k, 0)
    return a + jnp.where(mask, pltpu.roll(xa, SUBLANES - k, 0), pltpu.roll(xb, k, 0))


def _expert_kernel(rows_ref, gates_ref, goff_ref, h_ref, u_ref, v_ref, oin_ref, o_ref, part_ref, *, tb, gpt):
    c = pl.program_id(0)
    g8 = SUBLANES
    spg = PEER_GROUP // g8
    sub = lax.broadcasted_iota(I32, (g8, LANES), 0)

    @pl.when(jnp.logical_and(c == 0, pl.program_id(1) == 0))
    def _():
        part_ref[...] = jnp.zeros_like(part_ref)

    def groups(n):
        return goff_ref[n * PLAN_OFF_W + c], goff_ref[n * PLAN_OFF_W + c + 1]

    def tok_a(n, slot):
        g0, g1 = groups(n)
        h = h_ref[n].astype(F32)

        def grp(g, slot):
            for s in range(spg):
                base = (n * gpt + g) * PEER_GROUP + s * g8
                xs = []
                for i in range(g8):
                    prod = h * u_ref[rows_ref[base + i]].astype(F32)
                    xs.append(prod[0:SUBLANES] + prod[SUBLANES:ROW_SUB])
                xs = [xs[BITREV8[j]] for j in range(g8)]
                k = g8 // 2
                while k >= 1:
                    xs = [_fold_pair(xs[2 * j], xs[2 * j + 1], k, sub) for j in range(len(xs) // 2)]
                    k //= 2
                part_ref[slot + s] = xs[0]
            return slot + spg

        return lax.fori_loop(g0, g1, grp, slot)

    nslot = lax.fori_loop(0, tb, tok_a, 0)

    def slots_b(j, carry):
        sl = pl.ds(pl.multiple_of(j * PLAN_BATCH, PLAN_BATCH), PLAN_BATCH)
        act = jnp.sum(part_ref[sl], axis=-1, keepdims=True)
        part_ref[sl] = jnp.broadcast_to(_gelu(act), (PLAN_BATCH, g8, LANES))
        return carry

    lax.fori_loop(0, (nslot + PLAN_BATCH - 1) // PLAN_BATCH, slots_b, 0)

    def tok_c(n, slot):
        g0, g1 = groups(n)

        def grp(g, carry):
            slot, acc = carry
            for s in range(spg):
                base = (n * gpt + g) * PEER_GROUP + s * g8
                w8 = part_ref[slot + s]
                for i in range(g8):
                    w = w8[i:i + 1, :] * gates_ref[base + i]
                    acc = acc + w * v_ref[rows_ref[base + i]].astype(F32)
            return slot + spg, acc

        slot, acc = lax.fori_loop(g0, g1, grp, (slot, jnp.zeros((ROW_SUB, LANES), F32)))
        o_ref[n] = oin_ref[n] + acc
        return slot

    lax.fori_loop(0, tb, tok_c, 0)


def _expert_plan(idx, gate, n_exp, ch):
    m, nsel = idx.shape
    nch = n_exp // ch
    g8 = PEER_GROUP
    nfill = g8 - 1
    slots = -(-(nsel + nfill * nch) // g8) * g8
    cidx = idx // ch
    local = idx - cidx * ch
    chunks = jnp.arange(nch, dtype=I32)
    cnt = jnp.sum((cidx[:, :, None] == chunks[None, None, :]).astype(I32), axis=1)
    pad = (-cnt) % g8
    goff = jnp.cumsum((cnt + pad) // g8, axis=1)
    goff = jnp.concatenate([jnp.zeros((m, 1), I32), goff, jnp.zeros((m, PLAN_OFF_W - nch - 1), I32)], 1)
    span = 2 * ch
    fill_c = jnp.repeat(chunks, nfill)
    fill_r = jnp.tile(jnp.arange(nfill, dtype=I32), nch)
    active = fill_r[None, :] < pad[:, fill_c]
    fill_key = jnp.where(active, fill_c[None, :] * span + ch + fill_r[None, :], nch * span)
    ndead = slots - nsel - nfill * nch
    keys = jnp.concatenate([cidx * span + local, fill_key, jnp.full((m, ndead), nch * span, I32)], 1)
    zeros = jnp.zeros((m, slots - nsel), I32)
    rows = jnp.concatenate([local, zeros], 1)
    gates = jnp.concatenate([gate, zeros.astype(F32)], 1)
    _, rows, gates = lax.sort((keys, rows, gates), dimension=1, num_keys=1)
    return rows.reshape(-1), gates.reshape(-1), goff.reshape(-1), slots


def _experts(rows, gates, goff, slots, h3, u3, v3, *, tb, ch):
    m = h3.shape[0]
    e = u3.shape[0]
    gpt = slots // PEER_GROUP
    assert h3.shape[-1] == LANES and e % ch == 0 and m % tb == 0
    smem = lambda w: pl.BlockSpec((tb * w,), lambda c, t: (t,), memory_space=pltpu.SMEM)
    tok_blk = pl.BlockSpec((tb, ROW_SUB, LANES), lambda c, t: (t, 0, 0))
    tab_blk = pl.BlockSpec((ch, ROW_SUB, LANES), lambda c, t: (c, 0, 0), pipeline_mode=pl.Buffered(1))
    out0 = jnp.zeros((m, ROW_SUB, LANES), F32)
    return pl.pallas_call(
        functools.partial(_expert_kernel, tb=tb, gpt=gpt),
        grid=(e // ch, m // tb),
        in_specs=[smem(slots), smem(slots), smem(PLAN_OFF_W), tok_blk, tab_blk, tab_blk, tok_blk],
        out_specs=tok_blk,
        out_shape=jax.ShapeDtypeStruct((m, ROW_SUB, LANES), F32),
        scratch_shapes=[pltpu.VMEM((tb * slots // SUBLANES + PLAN_BATCH, SUBLANES, LANES), F32)],
        input_output_aliases={6: 0},
        compiler_params=_cparams(("arbitrary", "arbitrary")),
    )(rows, gates, goff, h3, u3, v3, out0)


def _peer_gates_kernel(idx_ref, gate_ref, g_ref, *, tm, unroll):
    rid = lax.broadcasted_iota(I32, (PEER_NKEYS, PEER_SEL), 0)
    nt = (((1,), (1,)), ((), ()))

    def body(t, carry):
        for u in range(unroll):
            n = t * unroll + u
            idx = idx_ref[pl.ds(n, 1), :]
            gate = gate_ref[pl.ds(n, 1), :]
            i1 = idx // PEER_NKEYS
            i2 = idx - i1 * PEER_NKEYS
            a = jnp.where(rid == i1, gate, 0.0).astype(BF16)
            b = jnp.where(rid == i2, 1.0, 0.0).astype(BF16)
            g_ref[n] = lax.dot_general(a, b, nt, preferred_element_type=F32).astype(g_ref.dtype)
        return carry

    lax.fori_loop(0, tm // unroll, body, 0)


def _peer_gates(idx, gate, tm=128, unroll=4):
    m = idx.shape[0]
    sel = pl.BlockSpec((tm, PEER_SEL), lambda i: (i, 0))
    return pl.pallas_call(
        functools.partial(_peer_gates_kernel, tm=tm, unroll=unroll), grid=(m // tm,),
        in_specs=[sel, sel],
        out_specs=pl.BlockSpec((tm, PEER_NKEYS, PEER_NKEYS), lambda i: (i, 0, 0)),
        out_shape=jax.ShapeDtypeStruct((m, PEER_NKEYS, PEER_NKEYS), BF16),
        compiler_params=_cparams(("parallel",)),
    )(idx, gate)


def _peer_dense_kernel(h_ref, ut_ref, v_ref, g_ref, oin_ref, o_ref):
    act = jnp.dot(h_ref[...], ut_ref[...], preferred_element_type=F32)
    g = g_ref[...].astype(F32)
    wt = jnp.where(g != 0.0, _gelu(act) * g, 0.0).astype(BF16)
    o_ref[...] = oin_ref[...] + jnp.dot(wt, v_ref[...], preferred_element_type=F32)


def _peer_dense(h2, u_t, v_tab, g, *, tm, te):
    m, d = h2.shape
    e = v_tab.shape[0]
    assert m % tm == 0 and e % te == 0
    tok = pl.BlockSpec((tm, d), lambda j, i: (i, 0))
    return pl.pallas_call(
        _peer_dense_kernel, grid=(e // te, m // tm),
        in_specs=[tok, pl.BlockSpec((d, te), lambda j, i: (0, j), pipeline_mode=pl.Buffered(1)),
                  pl.BlockSpec((te, d), lambda j, i: (j, 0), pipeline_mode=pl.Buffered(1)),
                  pl.BlockSpec((tm, te), lambda j, i: (i, j)), tok],
        out_specs=tok,
        out_shape=jax.ShapeDtypeStruct((m, d), F32),
        input_output_aliases={4: 0},
        compiler_params=_cparams(("arbitrary", "arbitrary")),
    )(h2, u_t, v_tab, g, jnp.zeros((m, d), F32))


def _block_diag2(w):
    z = jnp.zeros_like(w[0])
    return jnp.concatenate([jnp.concatenate([w[0], z], 1), jnp.concatenate([z, w[1]], 1)], 0)


def _per_head_pad(w, width, pad_to):
    k = w.shape[0]
    w = w.reshape(k, -1, width)
    return jnp.pad(w, ((0, 0), (0, 0), (0, pad_to - width))).reshape(k, -1)


def _rope_swap_cols(w_rope):
    half = MLA_ROPE // 2
    return jnp.concatenate([-w_rope[..., half:], w_rope[..., :half]], -1)


def _prep_layer_weights(l, w_in, mla_w_uq, mla_w_ukv, rk_w_up, rk_a_up, rk_g_up):
    d = w_in.shape[1]
    wi = w_in[l]
    lo, hi = RW_COLS, RW_COLS + MLA_Q_RANK + MLA_KV_RANK + MLA_ROPE
    w_rw = jnp.pad(wi[:, :lo], ((0, 0), (0, RW_COLS_PAD - RW_COLS))).astype(BF16)
    w_kr = wi[:, hi - MLA_ROPE:hi]
    place = lambda w: jnp.pad(w, ((0, 0), (MLA_NOPE, LANES - MLA_NOPE - MLA_ROPE)))
    w_mla = jnp.concatenate([wi[:, lo:hi - MLA_ROPE], place(w_kr), place(_rope_swap_cols(w_kr))], 1).astype(BF16)
    w_gate = wi[:, hi:].astype(BF16)

    uq = mla_w_uq[l].reshape(MLA_Q_RANK, MLA_HEADS, MLA_QK)
    zeros = jnp.zeros((MLA_Q_RANK, MLA_HEADS, LANES - MLA_QK), F32)
    wqa = jnp.concatenate([uq, zeros], -1).reshape(MLA_Q_RANK, -1).astype(BF16)
    zn = jnp.zeros((MLA_Q_RANK, MLA_HEADS, MLA_NOPE), F32)
    wqs = jnp.concatenate([zn, _rope_swap_cols(uq[..., MLA_NOPE:]), zeros], -1).reshape(MLA_Q_RANK, -1).astype(BF16)
    ukv = mla_w_ukv[l].reshape(MLA_KV_RANK, MLA_HEADS, MLA_NOPE + MLA_V)
    wk = jnp.pad(ukv[..., :MLA_NOPE], ((0, 0), (0, 0), (0, LANES - MLA_NOPE))).reshape(MLA_KV_RANK, -1).astype(BF16)
    wv = ukv[..., MLA_NOPE:].reshape(MLA_KV_RANK, -1).astype(BF16)

    wup = _block_diag2(rk_w_up[l]).astype(BF16)
    aup = _block_diag2(rk_a_up[l]).astype(BF16)
    gup = jnp.pad(_block_diag2(rk_g_up[l]), ((0, LANES - 2 * RW_GATE_LORA), (0, 0))).astype(BF16)
    return dict(w_rw=w_rw, w_mla=w_mla, w_gate=w_gate, wqa=wqa, wqs=wqs, wk=wk, wv=wv, wup=wup, aup=aup, gup=gup)


def _rope_lane_tables(l, c):
    rows = l // GRID_W
    row = jnp.repeat(jnp.arange(rows, dtype=F32), GRID_W)
    col = jnp.tile(jnp.arange(GRID_W, dtype=F32), rows)
    n_freq = MLA_ROPE // 4
    freqs = ROPE_BASE ** (-jnp.arange(n_freq, dtype=F32) / n_freq)
    ang = jnp.concatenate([row[:, None] * freqs, col[:, None] * freqs], -1)
    cos, sin = jnp.cos(ang), jnp.sin(ang)
    tail = LANES - MLA_NOPE - MLA_ROPE
    cos_t = jnp.concatenate([jnp.ones((l, MLA_NOPE), F32), cos, cos, jnp.ones((l, tail), F32)], -1)
    sin_t = jnp.concatenate([jnp.zeros((l, MLA_NOPE), F32), sin, sin, jnp.zeros((l, tail), F32)], -1)
    cos_t = jnp.concatenate([cos_t, jnp.ones((c, LANES), F32)], 0)
    sin_t = jnp.concatenate([sin_t, jnp.zeros((c, LANES), F32)], 0)
    return cos_t, sin_t


def _peer(h2, w_q, keys, u_t, v_tab, *, te, tm=256):
    m, d = h2.shape
    q = _mm(h2, w_q, tm=256, tn=_pick_tile(w_q.shape[1], 1024))
    idx_t, gate_t = _peer_topk(q, keys)
    idx = idx_t.transpose(2, 0, 1).reshape(m, PEER_SEL)
    gate = gate_t.transpose(2, 0, 1).reshape(m, PEER_SEL)
    g = _peer_gates(idx, gate).reshape(m, PEER_NKEYS * PEER_NKEYS)
    return _peer_dense(h2, u_t, v_tab, g, tm=tm, te=te)


def _forward(x, c, ctx, c_ctx, ada_w, ada_b, norm_mix_g, w_in, gate_b, rk_mu, rk_w0, rk_w_up, rk_a0,
             rk_a_up, rk_g_up, rk_k_k, rk_k_a, rk_r_k, rk_ln_w, rk_ln_b, mla_q_norm, mla_w_uq, mla_kv_norm,
             mla_w_ukv, w_branch_a, w_branch_b, w_out, norm_ffn_g, peer_w_q, peer_keys, peer_u, peer_v,
             final_norm_g, *, flash_tq=512, flash_tk=2816, scan_tt=32, peer_te=2048):
    nb, l, d = x.shape
    cl = ctx.shape[1]
    t = l + cl
    m = nb * t
    depth = w_in.shape[0]
    dims = dict(B=nb, L=l, C=cl, T=t)
    assert nb == 2 and 4 * nb * RW_HEADS == LANES and l % 256 == 0 and cl % 256 == 0

    xa = jnp.concatenate([x, ctx], 1).reshape(m, d)
    cos_t, sin_t = _rope_lane_tables(l, cl)
    seg = jnp.arange(RW_WIDTH, dtype=I32) // RW_HEAD
    bd = (seg[:, None] == seg[None, :]).astype(BF16)
    cvec = jnp.zeros((SUBLANES, d), F32).at[:nb].set(jax.nn.silu(c)).at[nb].set(jax.nn.silu(c_ctx))

    delta, mods_prev = None, None
    out = None
    for li in range(depth):
        wl = _prep_layer_weights(li, w_in, mla_w_uq, mla_w_ukv, rk_w_up, rk_a_up, rk_g_up)
        tn_ada = _pick_tile(6 * d, 768)
        mods = _mm(cvec, ada_w[li], tm=SUBLANES, tn=tn_ada, epilogue=lambda acc, bias: acc + bias,
                   extras=[(ada_b[li][None, :], pl.BlockSpec((1, tn_ada), lambda j, i: (0, j)))])
        mods = mods[:nb + 1].reshape((nb + 1) * 6, 1, d)
        g_mix = norm_mix_g[li][None, :]
        if li == 0:
            h = _norm_mod(xa, g_mix, mods, 0, 1, dims)
        else:
            xa, h = _resid_norm_mod(xa, delta, mods_prev, 5, g_mix, mods, 0, 1, dims)

        p_rw = _mm(h, wl["w_rw"], tm=256, tn=_pick_tile(RW_COLS_PAD, 1152))
        p_mla = _mm(h, wl["w_mla"], tm=256, tn=MLA_P_COLS)
        p_gate = _mm(h, wl["w_gate"], tm=256, tn=1024)

        mu = jnp.pad(rk_mu[li], (0, RW_COLS_PAD - RW_COLS))[None, :]
        feats = _rwfeat(p_rw.reshape(nb, t, RW_COLS_PAD), mu, rk_w0[li], wl["wup"], rk_a0[li], wl["aup"], wl["gup"],
                        rk_k_k[li][None, :], rk_k_a[li][None, :], rk_r_k[li].reshape(1, RW_WIDTH), bd, dims)
        r, nkk, v, dec0, dec1, b0, b1, ke0, ke1, bon0, bon1, g0, g1 = feats
        a_next = _to_scan_k(nkk, nkk, l)
        a_next = jnp.concatenate([a_next[1:], jnp.zeros_like(a_next[:1])], 0)
        kv = jnp.stack([_to_scan_k(dec0, dec1, l), _to_scan_k(b0, b1, l), _to_scan_k(ke0, ke1, l),
                        _to_scan_k(r, r, l), a_next], 1)
        yn = _scan(kv, _to_scan_v(v, l), tt=scan_tt)
        yn0, yn1 = _from_scan_y(yn, nb, cl)
        flat = lambda z: z.reshape(m, RW_WIDTH)
        ya = _readout(flat(yn0), flat(yn1), flat(bon0), flat(bon1), flat(g0), flat(g1),
                      rk_ln_w[li][None, :], rk_ln_b[li][None, :])

        q, k, vv = _mla_prep(p_mla, cos_t, sin_t, mla_q_norm[li][None, :], mla_kv_norm[li][None, :],
                             wl["wqa"], wl["wqs"], wl["wk"], wl["wv"], dims)
        q3, k3, v3 = (z.reshape(nb, t, -1) for z in (q, k, vv))
        yb = _flash(q3, k3, v3, tq=flash_tq, tk=flash_tk, q_off=0, nq=l // flash_tq, k_off=0, nk=t // flash_tk)
        yb = _flash(q3, k3, v3, tq=cl, tk=cl, q_off=l // cl, nq=1, k_off=l // cl, nk=1, prev_out=yb)
        yb = yb.reshape(m, MLA_HEADS * MLA_V)

        tpb, lt = t // 256, l // 256
        br_a = _mm(ya, w_branch_a[li].astype(BF16), tm=256, tn=1024)
        gb = gate_b[li]
        merged = _mm(
            yb, w_branch_b[li].astype(BF16), tm=256, tn=1024, out_dtype=BF16,
            epilogue=lambda acc, ga, gbb, ba, bb, a: _sigmoid(ga + ba) * a + _sigmoid(gbb + bb) * acc,
            extras=[(p_gate, pl.BlockSpec((256, 1024), lambda j, i: (i, j))),
                    (p_gate, pl.BlockSpec((256, 1024), lambda j, i: (i, j + d // 1024))),
                    (gb[0][None, :], pl.BlockSpec((1, 1024), lambda j, i: (0, j))),
                    (gb[1][None, :], pl.BlockSpec((1, 1024), lambda j, i: (0, j))),
                    (br_a, pl.BlockSpec((256, 1024), lambda j, i: (i, j)))])
        x1 = _mm(merged, w_out[li].astype(BF16), tm=256, tn=1024,
                 epilogue=lambda acc, xr, gt: xr + gt * acc,
                 extras=[(xa, pl.BlockSpec((256, 1024), lambda j, i: (i, j))),
                         (mods, pl.BlockSpec((None, 1, 1024), lambda j, i: (_mod_row(i, tpb, lt, 2, nb), 0, j)))])

        h2 = _norm_mod(x1, norm_ffn_g[li][None, :], mods, 3, 4, dims)
        delta = _peer(h2, peer_w_q[li].astype(BF16), peer_keys[li], peer_u[li].astype(BF16).T,
                      peer_v[li].astype(BF16), te=peer_te)
        xa, mods_prev = x1, mods

    out = _resid_final_norm(xa.reshape(nb, t, d), delta.reshape(nb, t, d), mods_prev, 5,
                            final_norm_g[None, :], dims)
    return out


def kernel(x, c, ctx, c_ctx, ada_w, ada_b, norm_mix_g, w_in, gate_b, rk_mu, rk_w0, rk_w_up, rk_a0, rk_a_up, rk_g_up, rk_k_k, rk_k_a, rk_r_k, rk_ln_w, rk_ln_b, mla_q_norm, mla_w_uq, mla_kv_norm, mla_w_ukv, w_branch_a, w_branch_b, w_out, norm_ffn_g, peer_w_q, peer_keys, peer_u, peer_v, final_norm_g):
    return _forward(x, c, ctx, c_ctx, ada_w, ada_b, norm_mix_g, w_in, gate_b, rk_mu, rk_w0, rk_w_up, rk_a0,
                    rk_a_up, rk_g_up, rk_k_k, rk_k_a, rk_r_k, rk_ln_w, rk_ln_b, mla_q_norm, mla_w_uq,
                    mla_kv_norm, mla_w_ukv, w_branch_a, w_branch_b, w_out, norm_ffn_g, peer_w_q, peer_keys,
                    peer_u, peer_v, final_norm_g)
```

```python
import functools
import math

import jax
import jax.numpy as jnp
from jax import lax
from jax.experimental import pallas as pl
from jax.experimental.pallas import tpu as pltpu

F32 = jnp.float32
BF16 = jnp.bfloat16
I32 = jnp.int32

NORM_EPS = 1e-6
GRID_W = 64
ROPE_BASE = 10000.0

RW_HEADS = 16
RW_HEAD = 64
RW_WIDTH = RW_HEADS * RW_HEAD
RW_DECAY_LORA = 64
RW_A_LORA = 64
RW_GATE_LORA = 32
RW_GN_EPS = 64e-5
RW_COLS = 3 * RW_WIDTH + 2 * (RW_DECAY_LORA + RW_A_LORA + RW_GATE_LORA)
RW_COLS_PAD = 3456

MLA_HEADS = 16
MLA_NOPE = 64
MLA_ROPE = 32
MLA_QK = MLA_NOPE + MLA_ROPE
MLA_V = 64
MLA_Q_RANK = 512
MLA_KV_RANK = 256
MLA_HEAD_PAD = 128
MLA_P_COLS = MLA_Q_RANK + MLA_KV_RANK + 2 * MLA_HEAD_PAD

PEER_HEADS = 8
PEER_NKEYS = 128
PEER_HALF = 128
PEER_TOPK = 16
PEER_SEL = PEER_HEADS * PEER_TOPK

LANES = 128
SUBLANES = 8
VMEM_LIMIT = 56 * 1024 * 1024


def _cparams(sem):
    return pltpu.CompilerParams(dimension_semantics=sem, vmem_limit_bytes=VMEM_LIMIT)


def _pick_tile(n, cap, mult=LANES):
    best = None
    t = mult
    while t <= min(n, cap):
        if n % t == 0:
            best = t
        t += mult
    assert best is not None, (n, cap)
    return best


def _mm_kernel(*refs, n_extra, epilogue):
    a_ref, w_ref = refs[0], refs[1]
    extra = refs[2:2 + n_extra]
    o_ref = refs[2 + n_extra]
    acc = jnp.dot(a_ref[...].astype(BF16), w_ref[...].astype(BF16), preferred_element_type=F32)
    if epilogue is not None:
        acc = epilogue(acc, *[e[...] for e in extra])
    o_ref[...] = acc.astype(o_ref.dtype)


def _mm(a, w, *, tm, tn, out_dtype=F32, epilogue=None, extras=()):
    m, k = a.shape
    k2, n = w.shape
    assert k == k2 and m % tm == 0 and n % tn == 0, (a.shape, w.shape, tm, tn)
    in_specs = [pl.BlockSpec((tm, k), lambda j, i: (i, 0)),
                pl.BlockSpec((k, tn), lambda j, i: (0, j))] + [s for _, s in extras]
    return pl.pallas_call(
        functools.partial(_mm_kernel, n_extra=len(extras), epilogue=epilogue),
        grid=(n // tn, m // tm),
        in_specs=in_specs,
        out_specs=pl.BlockSpec((tm, tn), lambda j, i: (i, j)),
        out_shape=jax.ShapeDtypeStruct((m, n), out_dtype),
        compiler_params=_cparams(("parallel", "parallel")),
    )(a, w, *[x for x, _ in extras])


def _sigmoid(x):
    return 1.0 / (1.0 + jnp.exp(-x))


def _mod_row(i, tiles_per_batch, lat_tiles, part, n_batch):
    which = jnp.where(i % tiles_per_batch >= lat_tiles, n_batch, i // tiles_per_batch)
    return which * 6 + part


def _rms(x, g):
    return x * lax.rsqrt(jnp.mean(x * x, -1, keepdims=True) + NORM_EPS) * g


def _norm_mod_kernel(x_ref, g_ref, sh_ref, sc_ref, h_ref):
    y = _rms(x_ref[...], g_ref[...])
    h_ref[...] = (y * (1.0 + sc_ref[...]) + sh_ref[...]).astype(h_ref.dtype)


def _resid_norm_mod_kernel(x_ref, d_ref, gt_ref, g_ref, sh_ref, sc_ref, xo_ref, h_ref):
    x = x_ref[...] + gt_ref[...] * d_ref[...]
    xo_ref[...] = x
    y = _rms(x, g_ref[...])
    h_ref[...] = (y * (1.0 + sc_ref[...]) + sh_ref[...]).astype(h_ref.dtype)


def _resid_final_norm_kernel(x_ref, d_ref, gt_ref, g_ref, o_ref):
    x = x_ref[...] + gt_ref[...] * d_ref[...]
    o_ref[...] = _rms(x, g_ref[...])


def _mod_spec(d, part, dims, tm):
    tpb, lt, nb = dims["T"] // tm, dims["L"] // tm, dims["B"]
    return pl.BlockSpec((None, 1, d), lambda i: (_mod_row(i, tpb, lt, part, nb), 0, 0))


def _norm_mod(x, g, mods, part_sh, part_sc, dims, tm=256):
    m, d = x.shape
    row = pl.BlockSpec((tm, d), lambda i: (i, 0))
    return pl.pallas_call(
        _norm_mod_kernel, grid=(m // tm,),
        in_specs=[row, pl.BlockSpec((1, d), lambda i: (0, 0)),
                  _mod_spec(d, part_sh, dims, tm), _mod_spec(d, part_sc, dims, tm)],
        out_specs=row, out_shape=jax.ShapeDtypeStruct((m, d), BF16),
        compiler_params=_cparams(("parallel",)),
    )(x, g, mods, mods)


def _resid_norm_mod(x, delta, mods_gt, part_gt, g, mods, part_sh, part_sc, dims, tm=256):
    m, d = x.shape
    row = pl.BlockSpec((tm, d), lambda i: (i, 0))
    return pl.pallas_call(
        _resid_norm_mod_kernel, grid=(m // tm,),
        in_specs=[row, row, _mod_spec(d, part_gt, dims, tm), pl.BlockSpec((1, d), lambda i: (0, 0)),
                  _mod_spec(d, part_sh, dims, tm), _mod_spec(d, part_sc, dims, tm)],
        out_specs=[row, row],
        out_shape=[jax.ShapeDtypeStruct((m, d), F32), jax.ShapeDtypeStruct((m, d), BF16)],
        compiler_params=_cparams(("parallel",)),
    )(x, delta, mods_gt, g, mods, mods)


def _resid_final_norm(x3, delta3, mods, part_gt, g, dims, tm=256):
    b, t, d = x3.shape
    lt = dims["L"] // tm
    blk = pl.BlockSpec((None, tm, d), lambda bi, i: (bi, i, 0))
    return pl.pallas_call(
        _resid_final_norm_kernel, grid=(b, lt),
        in_specs=[blk, blk, pl.BlockSpec((None, 1, d), lambda bi, i: (bi * 6 + part_gt, 0, 0)),
                  pl.BlockSpec((1, d), lambda bi, i: (0, 0))],
        out_specs=blk, out_shape=jax.ShapeDtypeStruct((b, dims["L"], d), F32),
        compiler_params=_cparams(("parallel", "parallel")),
    )(x3, delta3, mods, g)


def _segsum(x, bd):
    hi = x.astype(BF16)
    lo = (x - hi.astype(F32)).astype(BF16)
    return (jnp.dot(hi, bd, preferred_element_type=F32) + jnp.dot(lo, bd, preferred_element_type=F32))


def _rwfeat_kernel(p_ref, pp_ref, pn_ref, mu_ref, w0_ref, wup_ref, a0_ref, aup_ref, gup_ref, kk_ref, ka_ref,
                   rk_ref, bd_ref,
                   r_ref, nkk_ref, v_ref, dec0_ref, dec1_ref, b0_ref, b1_ref, ke0_ref, ke1_ref,
                   bon0_ref, bon1_ref, g0_ref, g1_ref, *, tm, lat_tiles, all_tiles):
    i = pl.program_id(1)
    p = p_ref[...]
    prev_ok = jnp.logical_and(i != 0, i != lat_tiles)
    next_ok = jnp.logical_and(i != lat_tiles - 1, i != all_tiles - 1)
    prow = jnp.where(prev_ok, pp_ref[SUBLANES - 1:SUBLANES, :], 0.0)
    nrow = jnp.where(next_ok, pn_ref[0:1, :], 0.0)
    rid = lax.broadcasted_iota(I32, (tm, 1), 0)
    prev = jnp.where(rid == 0, prow, pltpu.roll(p, 1, 0))
    nxt = jnp.where(rid == tm - 1, nrow, pltpu.roll(p, tm - 1, 0))
    ps = p + (0.5 * (prev + nxt) - p) * mu_ref[...]

    w_ = RW_WIDTH
    r = ps[:, 0:w_]
    k = ps[:, w_:2 * w_]
    v = ps[:, 2 * w_:3 * w_]
    wd = ps[:, 3 * w_:3 * w_ + LANES]
    ad = ps[:, 3 * w_ + LANES:3 * w_ + 2 * LANES]
    gd = ps[:, 3 * w_ + 2 * LANES:3 * w_ + 3 * LANES]
    bd = bd_ref[...]

    lw = jnp.dot(jnp.tanh(wd).astype(BF16), wup_ref[...], preferred_element_type=F32)
    la = jnp.dot(ad.astype(BF16), aup_ref[...], preferred_element_type=F32)
    lg = jnp.dot(_sigmoid(gd).astype(BF16), gup_ref[...], preferred_element_type=F32)

    kk = k * kk_ref[...]
    kk = kk * lax.rsqrt(_segsum(kk * kk, bd) + 1e-12)
    r_ref[...] = r
    nkk_ref[...] = -kk
    v_ref[...] = v
    rk = rk_ref[...]
    ka = ka_ref[...]
    outs = ((dec0_ref, b0_ref, ke0_ref, bon0_ref, g0_ref), (dec1_ref, b1_ref, ke1_ref, bon1_ref, g1_ref))
    for d in range(2):
        dec_ref, b_ref, ke_ref, bon_ref, g_ref = outs[d]
        z = w0_ref[d:d + 1, :] + lw[:, d * w_:(d + 1) * w_]
        nz = -z
        softplus = jnp.maximum(nz, 0.0) + jnp.log(1.0 + jnp.exp(-jnp.abs(nz)))
        wlog = -softplus - 0.5
        dec_ref[...] = jnp.exp(-jnp.exp(wlog))
        a = _sigmoid(a0_ref[d:d + 1, :] + la[:, d * w_:(d + 1) * w_])
        b_ref[...] = kk * a
        ke = k * (1.0 + (a - 1.0) * ka)
        ke_ref[...] = ke
        bon_ref[...] = _segsum(r * ke * rk, bd) * v
        g_ref[...] = lg[:, d * w_:(d + 1) * w_]


def _rwfeat(p3, mu, w0, wup, a0, aup, gup, k_k, k_a, r_k, bd, dims, tm=128):
    b, t, wc = p3.shape
    lt, at = dims["L"] // tm, t // tm
    hb = tm // SUBLANES
    nblk8 = t // SUBLANES
    w_ = RW_WIDTH
    full = lambda arr: pl.BlockSpec(arr.shape, lambda bi, i: (0,) * arr.ndim)
    out_blk = pl.BlockSpec((None, tm, w_), lambda bi, i: (bi, i, 0))
    n_out = 13
    return pl.pallas_call(
        functools.partial(_rwfeat_kernel, tm=tm, lat_tiles=lt, all_tiles=at),
        grid=(b, at),
        in_specs=[pl.BlockSpec((None, tm, wc), lambda bi, i: (bi, i, 0)),
                  pl.BlockSpec((None, SUBLANES, wc), lambda bi, i: (bi, jnp.maximum(i * hb - 1, 0), 0)),
                  pl.BlockSpec((None, SUBLANES, wc), lambda bi, i: (bi, jnp.minimum((i + 1) * hb, nblk8 - 1), 0)),
                  full(mu), full(w0), full(wup), full(a0), full(aup), full(gup), full(k_k), full(k_a),
                  full(r_k), full(bd)],
        out_specs=[out_blk] * n_out,
        out_shape=[jax.ShapeDtypeStruct((b, t, w_), F32)] * n_out,
        compiler_params=_cparams(("parallel", "parallel")),
    )(p3, p3, p3, mu, w0, wup, a0, aup, gup, k_k, k_a, r_k, bd)


SCAN_KH = RW_HEAD // 2


def _scan_kernel(kvf_ref, kvr_ref, vf_ref, vr_ref, yf_ref, yr_ref, s_ref, sa_ref, *, tt):
    @pl.when(pl.program_id(0) == 0)
    def _():
        s_ref[...] = jnp.zeros_like(s_ref)
        sa_ref[...] = jnp.zeros_like(sa_ref)

    lane = lax.broadcasted_iota(I32, (RW_HEAD, LANES), 1)
    lane1 = lax.broadcasted_iota(I32, (1, LANES), 1)
    quarter = LANES // 4
    fwd1 = (lane1 // quarter) % 2 == 0
    fwd = (lane // quarter) % 2 == 0

    def one_step(t, sa):
        tr = tt - 1 - t
        vt = jnp.where(fwd, vf_ref[t], vr_ref[tr])
        yacc = jnp.zeros((RW_HEAD, LANES), F32)
        sn = jnp.zeros((RW_HEAD, LANES), F32)
        for k in range(SCAN_KH):
            row = lambda q: jnp.where(fwd1, kvf_ref[t, q, pl.ds(k, 1), :], kvr_ref[tr, q, pl.ds(k, 1), :])
            w, b, ke, r, an = row(0), row(1), row(2), row(3), row(4)
            s = s_ref[k] * w + (sa * b + vt * ke)
            s_ref[k] = s
            yacc = yacc + s * r
            sn = sn + s * an
        y = yacc + pltpu.roll(yacc, LANES // 2, 1)
        sa_new = sn + pltpu.roll(sn, LANES // 2, 1)
        mean = jnp.mean(y, axis=0, keepdims=True)
        dlt = y - mean
        var = jnp.mean(dlt * dlt, axis=0, keepdims=True)
        return sa_new, dlt * lax.rsqrt(var + RW_GN_EPS)

    def body(j, sa):
        sa, y0 = one_step(2 * j, sa)
        sa, y1 = one_step(2 * j + 1, sa)
        low = lane < LANES // 2
        yf_ref[j] = jnp.where(low, y0, y1)
        yr_ref[tt // 2 - 1 - j] = jnp.where(low, y1, y0)
        return sa

    sa_ref[...] = lax.fori_loop(0, tt // 2, body, sa_ref[...])


def _scan(kv, vv, l, tt=32):
    t = kv.shape[0]
    assert t % tt == 0 and tt % 2 == 0 and l % tt == 0
    nb, lb = t // tt, l // tt
    cb = nb - lb
    fwd = lambda i: jnp.where(i < cb, lb + i, i - cb)
    rev = lambda i: jnp.where(i < cb, lb + (cb - 1 - i), lb - 1 - (i - cb))
    kv_spec = lambda f: pl.BlockSpec((tt, 5, SCAN_KH, LANES), lambda i: (f(i), 0, 0, 0))
    v_spec = lambda f: pl.BlockSpec((tt, RW_HEAD, LANES), lambda i: (f(i), 0, 0))
    y_spec = lambda f: pl.BlockSpec((tt // 2, RW_HEAD, LANES), lambda i: (f(i), 0, 0))
    y_shape = jax.ShapeDtypeStruct((t // 2, RW_HEAD, LANES), F32)
    return pl.pallas_call(
        functools.partial(_scan_kernel, tt=tt),
        grid=(nb,),
        in_specs=[kv_spec(fwd), kv_spec(rev), v_spec(fwd), v_spec(rev)],
        out_specs=[y_spec(fwd), y_spec(rev)],
        out_shape=[y_shape, y_shape],
        scratch_shapes=[pltpu.VMEM((SCAN_KH, RW_HEAD, LANES), F32), pltpu.VMEM((RW_HEAD, LANES), F32)],
        compiler_params=_cparams(("arbitrary",)),
    )(kv, kv, vv, vv)


def _to_scan_k(x0, x1):
    b, t, _ = x0.shape
    s = jnp.stack([x0, x1], 0).reshape(2, b, t, RW_HEADS, 2, SCAN_KH)
    return s.transpose(2, 5, 4, 0, 1, 3).reshape(t, SCAN_KH, 4 * b * RW_HEADS)


def _to_scan_v(x):
    b, t, _ = x.shape
    s = x.reshape(b, t, RW_HEADS, RW_HEAD).transpose(1, 3, 0, 2).reshape(t, RW_HEAD, b * RW_HEADS)
    return jnp.concatenate([s, s, s, s], -1)


def _next_step_rows(x, l):
    lat, ctx = x[:, :l], x[:, l:]
    z = jnp.zeros_like(x[:, :1])
    nxt_f = jnp.concatenate([lat[:, 1:], z, ctx[:, 1:], lat[:, :1]], 1)
    nxt_r = jnp.concatenate([z, lat[:, :-1], lat[:, l - 1:l], ctx[:, :-1]], 1)
    return nxt_f, nxt_r


def _from_scan_y(yf, yr, b):
    t2 = yf.shape[0]
    q = b * RW_HEADS
    outs = []
    for d, y in enumerate((yf, yr)):
        s = y.reshape(t2, RW_HEAD, 2, 2 * q)[..., d * q:(d + 1) * q]
        outs.append(s.reshape(t2, RW_HEAD, 2, b, RW_HEADS).transpose(3, 0, 2, 4, 1).reshape(b, 2 * t2, RW_WIDTH))
    return outs[0], outs[1]


def _readout_kernel(y0_ref, y1_ref, bon0_ref, bon1_ref, g0_ref, g1_ref, lw_ref, lb_ref, o_ref):
    lw, lb = lw_ref[...], lb_ref[...]
    o = (y0_ref[...] * lw + lb + bon0_ref[...]) * g0_ref[...]
    o = o + (y1_ref[...] * lw + lb + bon1_ref[...]) * g1_ref[...]
    o_ref[...] = o.astype(o_ref.dtype)


def _readout(y0, y1, bon0, bon1, g0, g1, ln_w, ln_b, tm=256):
    m, w_ = y0.shape
    row = pl.BlockSpec((tm, w_), lambda i: (i, 0))
    vec = pl.BlockSpec((1, w_), lambda i: (0, 0))
    return pl.pallas_call(
        _readout_kernel, grid=(m // tm,), in_specs=[row] * 6 + [vec, vec], out_specs=row,
        out_shape=jax.ShapeDtypeStruct((m, w_), BF16), compiler_params=_cparams(("parallel",)),
    )(y0, y1, bon0, bon1, g0, g1, ln_w, ln_b)


def _mla_prep_kernel(p_ref, c_ref, s_ref, qn_ref, kn_ref, wqa_ref, wqs_ref, wk_ref, wv_ref, q_ref, k_ref, v_ref):
    p = p_ref[...]
    cos, sin = c_ref[...], s_ref[...]
    qc = _rms(p[:, :MLA_Q_RANK], qn_ref[...]).astype(BF16)
    kvc = _rms(p[:, MLA_Q_RANK:MLA_Q_RANK + MLA_KV_RANK], kn_ref[...]).astype(BF16)
    kr = p[:, MLA_Q_RANK + MLA_KV_RANK:MLA_Q_RANK + MLA_KV_RANK + LANES]
    krs = p[:, MLA_Q_RANK + MLA_KV_RANK + LANES:MLA_Q_RANK + MLA_KV_RANK + 2 * LANES]
    krope = kr * cos + krs * sin
    qa = jnp.dot(qc, wqa_ref[...], preferred_element_type=F32)
    qs = jnp.dot(qc, wqs_ref[...], preferred_element_type=F32)
    kk = jnp.dot(kvc, wk_ref[...], preferred_element_type=F32)
    v_ref[...] = jnp.dot(kvc, wv_ref[...], preferred_element_type=F32).astype(v_ref.dtype)
    scale = MLA_QK ** -0.5
    for h in range(MLA_HEADS):
        sl = slice(h * LANES, (h + 1) * LANES)
        q_ref[:, sl] = ((qa[:, sl] * cos + qs[:, sl] * sin) * scale).astype(q_ref.dtype)
        k_ref[:, sl] = (kk[:, sl] + krope).astype(k_ref.dtype)


def _mla_prep(p, cos_t, sin_t, q_norm, kv_norm, wqa, wqs, wk, wv, dims, tm=256):
    m, pc = p.shape
    tpb = dims["T"] // tm
    full = lambda arr: pl.BlockSpec(arr.shape, lambda i: (0,) * arr.ndim)
    tab = pl.BlockSpec((tm, LANES), lambda i: (i % tpb, 0))
    hw = MLA_HEADS * LANES
    return pl.pallas_call(
        _mla_prep_kernel, grid=(m // tm,),
        in_specs=[pl.BlockSpec((tm, pc), lambda i: (i, 0)), tab, tab, full(q_norm), full(kv_norm),
                  full(wqa), full(wqs), full(wk), full(wv)],
        out_specs=[pl.BlockSpec((tm, hw), lambda i: (i, 0)), pl.BlockSpec((tm, hw), lambda i: (i, 0)),
                   pl.BlockSpec((tm, MLA_HEADS * MLA_V), lambda i: (i, 0))],
        out_shape=[jax.ShapeDtypeStruct((m, hw), BF16), jax.ShapeDtypeStruct((m, hw), BF16),
                   jax.ShapeDtypeStruct((m, MLA_HEADS * MLA_V), BF16)],
        compiler_params=_cparams(("parallel",)),
    )(p, cos_t, sin_t, q_norm, kv_norm, wqa, wqs, wk, wv)


def _flash_kernel(q_ref, k_ref, v_ref, *rest, nk):
    o_ref = rest[-7]
    state = (rest[-6:-3], rest[-3:])
    ki = pl.program_id(3)

    @pl.when(ki == 0)
    def _():
        for m_ref, l_ref, acc_ref in state:
            m_ref[...] = jnp.full_like(m_ref, -jnp.inf)
            l_ref[...] = jnp.zeros_like(l_ref)
            acc_ref[...] = jnp.zeros_like(acc_ref)

    v = v_ref[...]
    for hh, (m_ref, l_ref, acc_ref) in enumerate(state):
        q = q_ref[:, hh * LANES:(hh + 1) * LANES]
        k = k_ref[:, hh * LANES:(hh + 1) * LANES]
        s = lax.dot_general(q, k, (((1,), (1,)), ((), ())), preferred_element_type=F32)
        m_prev = m_ref[...]
        m_new = jnp.maximum(m_prev, jnp.max(s, -1, keepdims=True))
        alpha = jnp.exp(m_prev - m_new)
        pr = jnp.exp(s - m_new)
        l_ref[...] = alpha * l_ref[...] + jnp.sum(pr, -1, keepdims=True)
        acc_ref[...] = alpha * acc_ref[...] + jnp.dot(pr.astype(BF16), v, preferred_element_type=F32)
        m_ref[...] = m_new

    @pl.when(ki == nk - 1)
    def _():
        (_, l0, acc0), (_, l1, acc1) = state
        lane = lax.broadcasted_iota(I32, acc0.shape, 1)
        o_ref[...] = jnp.where(lane < MLA_V, acc0[...] / l0[...], acc1[...] / l1[...]).astype(o_ref.dtype)


def _flash(q3, k3, v3, *, tq, tk, q_off, nq, k_off, nk, prev_out=None):
    b, t, _ = q3.shape
    hp = MLA_HEADS // 2
    in_specs = [pl.BlockSpec((None, tq, 2 * LANES), lambda bi, h, qi, ki: (bi, q_off + qi, h)),
                pl.BlockSpec((None, tk, 2 * LANES), lambda bi, h, qi, ki: (bi, k_off + ki, h)),
                pl.BlockSpec((None, tk, LANES), lambda bi, h, qi, ki: (bi, k_off + ki, h))]
    args = [q3, k3, v3]
    aliases = {}
    if prev_out is not None:
        in_specs.append(pl.BlockSpec(memory_space=pl.ANY))
        args.append(prev_out)
        aliases = {3: 0}
    return pl.pallas_call(
        functools.partial(_flash_kernel, nk=nk),
        grid=(b, hp, nq, nk),
        in_specs=in_specs,
        out_specs=pl.BlockSpec((None, tq, LANES), lambda bi, h, qi, ki: (bi, q_off + qi, h)),
        out_shape=jax.ShapeDtypeStruct((b, t, MLA_HEADS * MLA_V), BF16),
        scratch_shapes=[pltpu.VMEM((tq, 1), F32), pltpu.VMEM((tq, 1), F32), pltpu.VMEM((tq, LANES), F32)] * 2,
        input_output_aliases=aliases,
        compiler_params=_cparams(("parallel", "parallel", "parallel", "arbitrary")),
    )(*args)


def _topk_rows(s, kk, payload=None):
    nrow = s.shape[0]
    rid = lax.broadcasted_iota(I32, s.shape, 0)
    vals, sel = [], []
    for _ in range(kk):
        m = jnp.max(s, axis=0, keepdims=True)
        pos = jnp.min(jnp.where(s == m, rid, nrow), axis=0, keepdims=True)
        hit = rid == pos
        vals.append(m)
        sel.append(pos if payload is None else jnp.max(jnp.where(hit, payload, -1), axis=0, keepdims=True))
        s = jnp.where(hit, -jnp.inf, s)
    return jnp.concatenate(vals, 0), jnp.concatenate(sel, 0)


def _peer_topk_kernel(q0_ref, q1_ref, keys_ref, idx_ref, gate_ref):
    nt = (((1,), (1,)), ((), ()))
    s0 = lax.dot_general(keys_ref[0].astype(BF16), q0_ref[...].astype(BF16), nt, preferred_element_type=F32)
    s1 = lax.dot_general(keys_ref[1].astype(BF16), q1_ref[...].astype(BF16), nt, preferred_element_type=F32)
    v1, i1 = _topk_rows(s0, PEER_TOPK)
    v2, i2 = _topk_rows(s1, PEER_TOPK)
    cand_s = jnp.concatenate([v1[a:a + 1, :] + v2 for a in range(PEER_TOPK)], 0)
    cand_i = jnp.concatenate([i1[a:a + 1, :] * PEER_NKEYS + i2 for a in range(PEER_TOPK)], 0)
    top_s, top_i = _topk_rows(cand_s, PEER_TOPK, payload=cand_i)
    e = jnp.exp(top_s - top_s[0:1, :])
    idx_ref[...] = top_i
    gate_ref[...] = e / jnp.sum(e, axis=0, keepdims=True)


def _peer_topk(q, keys, tm=256):
    m = q.shape[0]
    out_blk = pl.BlockSpec((None, PEER_TOPK, tm), lambda i, h: (h, 0, i))
    return pl.pallas_call(
        _peer_topk_kernel, grid=(m // tm, PEER_HEADS),
        in_specs=[pl.BlockSpec((tm, PEER_HALF), lambda i, h: (i, 2 * h)),
                  pl.BlockSpec((tm, PEER_HALF), lambda i, h: (i, 2 * h + 1)),
                  pl.BlockSpec((None, 2, PEER_NKEYS, PEER_HALF), lambda i, h: (h, 0, 0, 0))],
        out_specs=[out_blk, out_blk],
        out_shape=[jax.ShapeDtypeStruct((PEER_HEADS, PEER_TOPK, m), I32),
                   jax.ShapeDtypeStruct((PEER_HEADS, PEER_TOPK, m), F32)],
        compiler_params=_cparams(("parallel", "parallel")),
    )(q, q, keys)


PEER_GROUP = 16
ROW_SUB = 16
PLAN_OFF_W = 16
PLAN_BATCH = 16
BITREV8 = (0, 4, 2, 6, 1, 5, 3, 7)


def _gelu(x):
    return 0.5 * x * (1.0 + lax.erf(x * (2.0 ** -0.5)))


def _fold_pair(xa, xb, k, sub):
    mask = (sub & (2 * k - 1)) < k
    a = jnp.where(mask, xa, xb)
    if 2 * k == SUBLANES:
        return a + pltpu.roll(jnp.where(mask, xb, xa), k, 0)
    return a + jnp.where(mask, pltpu.roll(xa, SUBLANES - k, 0), pltpu.roll(xb, k, 0))


def _expert_kernel(rows_ref, gates_ref, goff_ref, h_ref, u_ref, v_ref, oin_ref, o_ref, part_ref, *, tb, gpt):
    c = pl.program_id(0)
    g8 = SUBLANES
    spg = PEER_GROUP // g8
    sub = lax.broadcasted_iota(I32, (g8, LANES), 0)

    @pl.when(jnp.logical_and(c == 0, pl.program_id(1) == 0))
    def _():
        part_ref[...] = jnp.zeros_like(part_ref)

    def groups(n):
        return goff_ref[n * PLAN_OFF_W + c], goff_ref[n * PLAN_OFF_W + c + 1]

    def tok_a(n, slot):
        g0, g1 = groups(n)
        h = h_ref[n].astype(F32)

        def grp(g, slot):
            for s in range(spg):
                base = (n * gpt + g) * PEER_GROUP + s * g8
                xs = []
                for i in range(g8):
                    prod = h * u_ref[rows_ref[base + i]].astype(F32)
                    xs.append(prod[0:SUBLANES] + prod[SUBLANES:ROW_SUB])
                xs = [xs[BITREV8[j]] for j in range(g8)]
                k = g8 // 2
                while k >= 1:
                    xs = [_fold_pair(xs[2 * j], xs[2 * j + 1], k, sub) for j in range(len(xs) // 2)]
                    k //= 2
                part_ref[slot + s] = xs[0]
            return slot + spg

        return lax.fori_loop(g0, g1, grp, slot)

    nslot = lax.fori_loop(0, tb, tok_a, 0)

    def slots_b(j, carry):
        sl = pl.ds(pl.multiple_of(j * PLAN_BATCH, PLAN_BATCH), PLAN_BATCH)
        act = jnp.sum(part_ref[sl], axis=-1, keepdims=True)
        part_ref[sl] = jnp.broadcast_to(_gelu(act), (PLAN_BATCH, g8, LANES))
        return carry

    lax.fori_loop(0, (nslot + PLAN_BATCH - 1) // PLAN_BATCH, slots_b, 0)

    def tok_c(n, slot):
        g0, g1 = groups(n)

        def grp(g, carry):
            slot, acc = carry
            for s in range(spg):
                base = (n * gpt + g) * PEER_GROUP + s * g8
                w8 = part_ref[slot + s]
                for i in range(g8):
                    w = w8[i:i + 1, :] * gates_ref[base + i]
                    acc = acc + w * v_ref[rows_ref[base + i]].astype(F32)
            return slot + spg, acc

        slot, acc = lax.fori_loop(g0, g1, grp, (slot, jnp.zeros((ROW_SUB, LANES), F32)))
        o_ref[n] = oin_ref[n] + acc
        return slot

    lax.fori_loop(0, tb, tok_c, 0)


def _expert_plan(idx, gate, n_exp, ch):
    m, nsel = idx.shape
    nch = n_exp // ch
    g8 = PEER_GROUP
    nfill = g8 - 1
    slots = -(-(nsel + nfill * nch) // g8) * g8
    cidx = idx // ch
    local = idx - cidx * ch
    chunks = jnp.arange(nch, dtype=I32)
    cnt = jnp.sum((cidx[:, :, None] == chunks[None, None, :]).astype(I32), axis=1)
    pad = (-cnt) % g8
    goff = jnp.cumsum((cnt + pad) // g8, axis=1)
    goff = jnp.concatenate([jnp.zeros((m, 1), I32), goff, jnp.zeros((m, PLAN_OFF_W - nch - 1), I32)], 1)
    span = 2 * ch
    fill_c = jnp.repeat(chunks, nfill)
    fill_r = jnp.tile(jnp.arange(nfill, dtype=I32), nch)
    active = fill_r[None, :] < pad[:, fill_c]
    fill_key = jnp.where(active, fill_c[None, :] * span + ch + fill_r[None, :], nch * span)
    ndead = slots - nsel - nfill * nch
    keys = jnp.concatenate([cidx * span + local, fill_key, jnp.full((m, ndead), nch * span, I32)], 1)
    zeros = jnp.zeros((m, slots - nsel), I32)
    rows = jnp.concatenate([local, zeros], 1)
    gates = jnp.concatenate([gate, zeros.astype(F32)], 1)
    _, rows, gates = lax.sort((keys, rows, gates), dimension=1, num_keys=1)
    return rows.reshape(-1), gates.reshape(-1), goff.reshape(-1), slots


def _experts(rows, gates, goff, slots, h3, u3, v3, *, tb, ch):
    m = h3.shape[0]
    e = u3.shape[0]
    gpt = slots // PEER_GROUP
    assert h3.shape[-1] == LANES and e % ch == 0 and m % tb == 0
    smem = lambda w: pl.BlockSpec((tb * w,), lambda c, t: (t,), memory_space=pltpu.SMEM)
    tok_blk = pl.BlockSpec((tb, ROW_SUB, LANES), lambda c, t: (t, 0, 0))
    tab_blk = pl.BlockSpec((ch, ROW_SUB, LANES), lambda c, t: (c, 0, 0), pipeline_mode=pl.Buffered(1))
    out0 = jnp.zeros((m, ROW_SUB, LANES), F32)
    return pl.pallas_call(
        functools.partial(_expert_kernel, tb=tb, gpt=gpt),
        grid=(e // ch, m // tb),
        in_specs=[smem(slots), smem(slots), smem(PLAN_OFF_W), tok_blk, tab_blk, tab_blk, tok_blk],
        out_specs=tok_blk,
        out_shape=jax.ShapeDtypeStruct((m, ROW_SUB, LANES), F32),
        scratch_shapes=[pltpu.VMEM((tb * slots // SUBLANES + PLAN_BATCH, SUBLANES, LANES), F32)],
        input_output_aliases={6: 0},
        compiler_params=_cparams(("arbitrary", "arbitrary")),
    )(rows, gates, goff, h3, u3, v3, out0)


def _peer_gates_kernel(idx_ref, gate_ref, g_ref, *, tm, unroll):
    rid = lax.broadcasted_iota(I32, (PEER_NKEYS, PEER_SEL), 0)
    nt = (((1,), (1,)), ((), ()))

    def body(t, carry):
        for u in range(unroll):
            n = t * unroll + u
            idx = idx_ref[pl.ds(n, 1), :]
            gate = gate_ref[pl.ds(n, 1), :]
            i1 = idx // PEER_NKEYS
            i2 = idx - i1 * PEER_NKEYS
            a = jnp.where(rid == i1, gate, 0.0).astype(BF16)
            b = jnp.where(rid == i2, 1.0, 0.0).astype(BF16)
            g_ref[n] = lax.dot_general(a, b, nt, preferred_element_type=F32).astype(g_ref.dtype)
        return carry

    lax.fori_loop(0, tm // unroll, body, 0)


def _peer_gates(idx, gate, tm=128, unroll=4):
    m = idx.shape[0]
    sel = pl.BlockSpec((tm, PEER_SEL), lambda i: (i, 0))
    return pl.pallas_call(
        functools.partial(_peer_gates_kernel, tm=tm, unroll=unroll), grid=(m // tm,),
        in_specs=[sel, sel],
        out_specs=pl.BlockSpec((tm, PEER_NKEYS, PEER_NKEYS), lambda i: (i, 0, 0)),
        out_shape=jax.ShapeDtypeStruct((m, PEER_NKEYS, PEER_NKEYS), BF16),
        compiler_params=_cparams(("parallel",)),
    )(idx, gate)


def _peer_dense_kernel(h_ref, u_ref, v_ref, g_ref, oin_ref, o_ref):
    act = lax.dot_general(h_ref[...], u_ref[...], (((1,), (1,)), ((), ())), preferred_element_type=F32)
    g = g_ref[...].astype(F32)
    wt = jnp.where(g != 0.0, _gelu(act) * g, 0.0).astype(BF16)
    o_ref[...] = oin_ref[...] + jnp.dot(wt, v_ref[...], preferred_element_type=F32)


def _peer_dense(h2, u_tab, v_tab, g, *, tm, te):
    m, d = h2.shape
    e = v_tab.shape[0]
    assert m % tm == 0 and e % te == 0
    tok = pl.BlockSpec((tm, d), lambda j, i: (i, 0))
    tab = pl.BlockSpec((te, d), lambda j, i: (j, 0), pipeline_mode=pl.Buffered(1))
    return pl.pallas_call(
        _peer_dense_kernel, grid=(e // te, m // tm),
        in_specs=[tok, tab, tab, pl.BlockSpec((tm, te), lambda j, i: (i, j)), tok],
        out_specs=tok,
        out_shape=jax.ShapeDtypeStruct((m, d), F32),
        input_output_aliases={4: 0},
        compiler_params=_cparams(("arbitrary", "arbitrary")),
    )(h2, u_tab, v_tab, g, jnp.zeros((m, d), F32))


def _block_diag2(w):
    z = jnp.zeros_like(w[0])
    return jnp.concatenate([jnp.concatenate([w[0], z], 1), jnp.concatenate([z, w[1]], 1)], 0)


def _per_head_pad(w, width, pad_to):
    k = w.shape[0]
    w = w.reshape(k, -1, width)
    return jnp.pad(w, ((0, 0), (0, 0), (0, pad_to - width))).reshape(k, -1)


def _rope_swap_cols(w_rope):
    half = MLA_ROPE // 2
    return jnp.concatenate([-w_rope[..., half:], w_rope[..., :half]], -1)


def _prep_layer_weights(l, w_in, mla_w_uq, mla_w_ukv, rk_w_up, rk_a_up, rk_g_up):
    d = w_in.shape[1]
    wi = w_in[l]
    lo, hi = RW_COLS, RW_COLS + MLA_Q_RANK + MLA_KV_RANK + MLA_ROPE
    w_rw = jnp.pad(wi[:, :lo], ((0, 0), (0, RW_COLS_PAD - RW_COLS))).astype(BF16)
    w_kr = wi[:, hi - MLA_ROPE:hi]
    place = lambda w: jnp.pad(w, ((0, 0), (MLA_NOPE, LANES - MLA_NOPE - MLA_ROPE)))
    w_mla = jnp.concatenate([wi[:, lo:hi - MLA_ROPE], place(w_kr), place(_rope_swap_cols(w_kr))], 1).astype(BF16)
    w_gate = wi[:, hi:].astype(BF16)

    uq = mla_w_uq[l].reshape(MLA_Q_RANK, MLA_HEADS, MLA_QK)
    zeros = jnp.zeros((MLA_Q_RANK, MLA_HEADS, LANES - MLA_QK), F32)
    wqa = jnp.concatenate([uq, zeros], -1).reshape(MLA_Q_RANK, -1).astype(BF16)
    zn = jnp.zeros((MLA_Q_RANK, MLA_HEADS, MLA_NOPE), F32)
    wqs = jnp.concatenate([zn, _rope_swap_cols(uq[..., MLA_NOPE:]), zeros], -1).reshape(MLA_Q_RANK, -1).astype(BF16)
    ukv = mla_w_ukv[l].reshape(MLA_KV_RANK, MLA_HEADS, MLA_NOPE + MLA_V)
    wk = jnp.pad(ukv[..., :MLA_NOPE], ((0, 0), (0, 0), (0, LANES - MLA_NOPE))).reshape(MLA_KV_RANK, -1).astype(BF16)
    wv = ukv[..., MLA_NOPE:].reshape(MLA_KV_RANK, -1).astype(BF16)

    wup = _block_diag2(rk_w_up[l]).astype(BF16)
    aup = _block_diag2(rk_a_up[l]).astype(BF16)
    gup = jnp.pad(_block_diag2(rk_g_up[l]), ((0, LANES - 2 * RW_GATE_LORA), (0, 0))).astype(BF16)
    return dict(w_rw=w_rw, w_mla=w_mla, w_gate=w_gate, wqa=wqa, wqs=wqs, wk=wk, wv=wv, wup=wup, aup=aup, gup=gup)


def _rope_lane_tables(l, c):
    rows = l // GRID_W
    row = jnp.repeat(jnp.arange(rows, dtype=F32), GRID_W)
    col = jnp.tile(jnp.arange(GRID_W, dtype=F32), rows)
    n_freq = MLA_ROPE // 4
    freqs = ROPE_BASE ** (-jnp.arange(n_freq, dtype=F32) / n_freq)
    ang = jnp.concatenate([row[:, None] * freqs, col[:, None] * freqs], -1)
    cos, sin = jnp.cos(ang), jnp.sin(ang)
    tail = LANES - MLA_NOPE - MLA_ROPE
    cos_t = jnp.concatenate([jnp.ones((l, MLA_NOPE), F32), cos, cos, jnp.ones((l, tail), F32)], -1)
    sin_t = jnp.concatenate([jnp.zeros((l, MLA_NOPE), F32), sin, sin, jnp.zeros((l, tail), F32)], -1)
    cos_t = jnp.concatenate([cos_t, jnp.ones((c, LANES), F32)], 0)
    sin_t = jnp.concatenate([sin_t, jnp.zeros((c, LANES), F32)], 0)
    return cos_t, sin_t


def _peer(h2, w_q, keys, u_tab, v_tab, *, te, tm=256):
    m, d = h2.shape
    q = _mm(h2, w_q, tm=256, tn=_pick_tile(w_q.shape[1], 1024))
    idx_t, gate_t = _peer_topk(q, keys)
    idx = idx_t.transpose(2, 0, 1).reshape(m, PEER_SEL)
    gate = gate_t.transpose(2, 0, 1).reshape(m, PEER_SEL)
    g = _peer_gates(idx, gate).reshape(m, PEER_NKEYS * PEER_NKEYS)
    return _peer_dense(h2, u_tab, v_tab, g, tm=tm, te=te)


def _forward(x, c, ctx, c_ctx, ada_w, ada_b, norm_mix_g, w_in, gate_b, rk_mu, rk_w0, rk_w_up, rk_a0,
             rk_a_up, rk_g_up, rk_k_k, rk_k_a, rk_r_k, rk_ln_w, rk_ln_b, mla_q_norm, mla_w_uq, mla_kv_norm,
             mla_w_ukv, w_branch_a, w_branch_b, w_out, norm_ffn_g, peer_w_q, peer_keys, peer_u, peer_v,
             final_norm_g, *, flash_tq=512, flash_tk=2816, scan_tt=32, peer_te=2048):
    nb, l, d = x.shape
    cl = ctx.shape[1]
    t = l + cl
    m = nb * t
    depth = w_in.shape[0]
    dims = dict(B=nb, L=l, C=cl, T=t)
    assert nb == 2 and 4 * nb * RW_HEADS == LANES and l % 256 == 0 and cl % 256 == 0

    xa = jnp.concatenate([x, ctx], 1).reshape(m, d)
    cos_t, sin_t = _rope_lane_tables(l, cl)
    seg = jnp.arange(RW_WIDTH, dtype=I32) // RW_HEAD
    bd = (seg[:, None] == seg[None, :]).astype(BF16)
    cvec = jnp.zeros((SUBLANES, d), F32).at[:nb].set(jax.nn.silu(c)).at[nb].set(jax.nn.silu(c_ctx))

    delta, mods_prev = None, None
    out = None
    for li in range(depth):
        wl = _prep_layer_weights(li, w_in, mla_w_uq, mla_w_ukv, rk_w_up, rk_a_up, rk_g_up)
        tn_ada = _pick_tile(6 * d, 768)
        mods = _mm(cvec, ada_w[li], tm=SUBLANES, tn=tn_ada, epilogue=lambda acc, bias: acc + bias,
                   extras=[(ada_b[li][None, :], pl.BlockSpec((1, tn_ada), lambda j, i: (0, j)))])
        mods = mods[:nb + 1].reshape((nb + 1) * 6, 1, d)
        g_mix = norm_mix_g[li][None, :]
        if li == 0:
            h = _norm_mod(xa, g_mix, mods, 0, 1, dims)
        else:
            xa, h = _resid_norm_mod(xa, delta, mods_prev, 5, g_mix, mods, 0, 1, dims)

        p_rw = _mm(h, wl["w_rw"], tm=256, tn=_pick_tile(RW_COLS_PAD, 1152))
        p_mla = _mm(h, wl["w_mla"], tm=256, tn=MLA_P_COLS)
        p_gate = _mm(h, wl["w_gate"], tm=256, tn=1024)

        mu = jnp.pad(rk_mu[li], (0, RW_COLS_PAD - RW_COLS))[None, :]
        feats = _rwfeat(p_rw.reshape(nb, t, RW_COLS_PAD), mu, rk_w0[li], wl["wup"], rk_a0[li], wl["aup"], wl["gup"],
                        rk_k_k[li][None, :], rk_k_a[li][None, :], rk_r_k[li].reshape(1, RW_WIDTH), bd, dims)
        r, nkk, v, dec0, dec1, b0, b1, ke0, ke1, bon0, bon1, g0, g1 = feats
        kv = jnp.stack([_to_scan_k(dec0, dec1), _to_scan_k(b0, b1), _to_scan_k(ke0, ke1),
                        _to_scan_k(r, r), _to_scan_k(*_next_step_rows(nkk, l))], 1)
        ynf, ynr = _scan(kv, _to_scan_v(v), l, tt=scan_tt)
        yn0, yn1 = _from_scan_y(ynf, ynr, nb)
        flat = lambda z: z.reshape(m, RW_WIDTH)
        ya = _readout(flat(yn0), flat(yn1), flat(bon0), flat(bon1), flat(g0), flat(g1),
                      rk_ln_w[li][None, :], rk_ln_b[li][None, :])

        q, k, vv = _mla_prep(p_mla, cos_t, sin_t, mla_q_norm[li][None, :], mla_kv_norm[li][None, :],
                             wl["wqa"], wl["wqs"], wl["wk"], wl["wv"], dims)
        q3, k3, v3 = (z.reshape(nb, t, -1) for z in (q, k, vv))
        yb = _flash(q3, k3, v3, tq=flash_tq, tk=flash_tk, q_off=0, nq=l // flash_tq, k_off=0, nk=t // flash_tk)
        yb = _flash(q3, k3, v3, tq=cl, tk=cl, q_off=l // cl, nq=1, k_off=l // cl, nk=1, prev_out=yb)
        yb = yb.reshape(m, MLA_HEADS * MLA_V)

        tpb, lt = t // 256, l // 256
        br_a = _mm(ya, w_branch_a[li].astype(BF16), tm=256, tn=1024)
        gb = gate_b[li]
        merged = _mm(
            yb, w_branch_b[li].astype(BF16), tm=256, tn=1024, out_dtype=BF16,
            epilogue=lambda acc, ga, gbb, ba, bb, a: _sigmoid(ga + ba) * a + _sigmoid(gbb + bb) * acc,
            extras=[(p_gate, pl.BlockSpec((256, 1024), lambda j, i: (i, j))),
                    (p_gate, pl.BlockSpec((256, 1024), lambda j, i: (i, j + d // 1024))),
                    (gb[0][None, :], pl.BlockSpec((1, 1024), lambda j, i: (0, j))),
                    (gb[1][None, :], pl.BlockSpec((1, 1024), lambda j, i: (0, j))),
                    (br_a, pl.BlockSpec((256, 1024), lambda j, i: (i, j)))])
        x1 = _mm(merged, w_out[li].astype(BF16), tm=256, tn=1024,
                 epilogue=lambda acc, xr, gt: xr + gt * acc,
                 extras=[(xa, pl.BlockSpec((256, 1024), lambda j, i: (i, j))),
                         (mods, pl.BlockSpec((None, 1, 1024), lambda j, i: (_mod_row(i, tpb, lt, 2, nb), 0, j)))])

        h2 = _norm_mod(x1, norm_ffn_g[li][None, :], mods, 3, 4, dims)
        delta = _peer(h2, peer_w_q[li].astype(BF16), peer_keys[li], peer_u[li].astype(BF16),
                      peer_v[li].astype(BF16), te=peer_te)
        xa, mods_prev = x1, mods

    out = _resid_final_norm(xa.reshape(nb, t, d), delta.reshape(nb, t, d), mods_prev, 5,
                            final_norm_g[None, :], dims)
    return out


def kernel(x, c, ctx, c_ctx, ada_w, ada_b, norm_mix_g, w_in, gate_b, rk_mu, rk_w0, rk_w_up, rk_a0, rk_a_up, rk_g_up, rk_k_k, rk_k_a, rk_r_k, rk_ln_w, rk_ln_b, mla_q_norm, mla_w_uq, mla_kv_norm, mla_w_ukv, w_branch_a, w_branch_b, w_out, norm_ffn_g, peer_w_q, peer_keys, peer_u, peer_v, final_norm_g):
    return _forward(x, c, ctx, c_ctx, ada_w, ada_b, norm_mix_g, w_in, gate_b, rk_mu, rk_w0, rk_w_up, rk_a0,
                    rk_a_up, rk_g_up, rk_k_k, rk_k_a, rk_r_k, rk_ln_w, rk_ln_b, mla_q_norm, mla_w_uq,
                    mla_kv_norm, mla_w_ukv, w_branch_a, w_branch_b, w_out, norm_ffn_g, peer_w_q, peer_keys,
                    peer_u, peer_v, final_norm_g)
```

```python
import functools
import math

import jax
import jax.numpy as jnp
from jax import lax
from jax.experimental import pallas as pl
from jax.experimental.pallas import tpu as pltpu

F32 = jnp.float32
BF16 = jnp.bfloat16
I32 = jnp.int32

NORM_EPS = 1e-6
GRID_W = 64
ROPE_BASE = 10000.0

RW_HEADS = 16
RW_HEAD = 64
RW_WIDTH = RW_HEADS * RW_HEAD
RW_DECAY_LORA = 64
RW_A_LORA = 64
RW_GATE_LORA = 32
RW_GN_EPS = 64e-5
RW_COLS = 3 * RW_WIDTH + 2 * (RW_DECAY_LORA + RW_A_LORA + RW_GATE_LORA)
RW_COLS_PAD = 3456

MLA_HEADS = 16
MLA_NOPE = 64
MLA_ROPE = 32
MLA_QK = MLA_NOPE + MLA_ROPE
MLA_V = 64
MLA_Q_RANK = 512
MLA_KV_RANK = 256
MLA_HEAD_PAD = 128
MLA_P_COLS = MLA_Q_RANK + MLA_KV_RANK + 2 * MLA_HEAD_PAD

PEER_HEADS = 8
PEER_NKEYS = 128
PEER_HALF = 128
PEER_TOPK = 16
PEER_SEL = PEER_HEADS * PEER_TOPK

LANES = 128
SUBLANES = 8
VMEM_LIMIT = 56 * 1024 * 1024


def _cparams(sem):
    return pltpu.CompilerParams(dimension_semantics=sem, vmem_limit_bytes=VMEM_LIMIT)


def _pick_tile(n, cap, mult=LANES):
    best = None
    t = mult
    while t <= min(n, cap):
        if n % t == 0:
            best = t
        t += mult
    assert best is not None, (n, cap)
    return best


def _mm_kernel(*refs, n_extra, epilogue):
    a_ref, w_ref = refs[0], refs[1]
    extra = refs[2:2 + n_extra]
    o_ref = refs[2 + n_extra]
    acc = jnp.dot(a_ref[...].astype(BF16), w_ref[...].astype(BF16), preferred_element_type=F32)
    if epilogue is not None:
        acc = epilogue(acc, *[e[...] for e in extra])
    o_ref[...] = acc.astype(o_ref.dtype)


def _mm(a, w, *, tm, tn, out_dtype=F32, epilogue=None, extras=()):
    m, k = a.shape
    k2, n = w.shape
    assert k == k2 and m % tm == 0 and n % tn == 0, (a.shape, w.shape, tm, tn)
    in_specs = [pl.BlockSpec((tm, k), lambda j, i: (i, 0)),
                pl.BlockSpec((k, tn), lambda j, i: (0, j))] + [s for _, s in extras]
    return pl.pallas_call(
        functools.partial(_mm_kernel, n_extra=len(extras), epilogue=epilogue),
        grid=(n // tn, m // tm),
        in_specs=in_specs,
        out_specs=pl.BlockSpec((tm, tn), lambda j, i: (i, j)),
        out_shape=jax.ShapeDtypeStruct((m, n), out_dtype),
        compiler_params=_cparams(("parallel", "parallel")),
    )(a, w, *[x for x, _ in extras])


def _sigmoid(x):
    return 1.0 / (1.0 + jnp.exp(-x))


def _mod_row(i, tiles_per_batch, lat_tiles, part, n_batch):
    which = jnp.where(i % tiles_per_batch >= lat_tiles, n_batch, i // tiles_per_batch)
    return which * 6 + part


def _rms(x, g):
    return x * lax.rsqrt(jnp.mean(x * x, -1, keepdims=True) + NORM_EPS) * g


def _norm_mod_kernel(x_ref, g_ref, sh_ref, sc_ref, h_ref):
    y = _rms(x_ref[...], g_ref[...])
    h_ref[...] = (y * (1.0 + sc_ref[...]) + sh_ref[...]).astype(h_ref.dtype)


def _resid_norm_mod_kernel(x_ref, d_ref, gt_ref, g_ref, sh_ref, sc_ref, xo_ref, h_ref):
    x = x_ref[...] + gt_ref[...] * d_ref[...]
    xo_ref[...] = x
    y = _rms(x, g_ref[...])
    h_ref[...] = (y * (1.0 + sc_ref[...]) + sh_ref[...]).astype(h_ref.dtype)


def _resid_final_norm_kernel(x_ref, d_ref, gt_ref, g_ref, o_ref):
    x = x_ref[...] + gt_ref[...] * d_ref[...]
    o_ref[...] = _rms(x, g_ref[...])


def _mod_spec(d, part, dims, tm):
    tpb, lt, nb = dims["T"] // tm, dims["L"] // tm, dims["B"]
    return pl.BlockSpec((None, 1, d), lambda i: (_mod_row(i, tpb, lt, part, nb), 0, 0))


def _norm_mod(x, g, mods, part_sh, part_sc, dims, tm=256):
    m, d = x.shape
    row = pl.BlockSpec((tm, d), lambda i: (i, 0))
    return pl.pallas_call(
        _norm_mod_kernel, grid=(m // tm,),
        in_specs=[row, pl.BlockSpec((1, d), lambda i: (0, 0)),
                  _mod_spec(d, part_sh, dims, tm), _mod_spec(d, part_sc, dims, tm)],
        out_specs=row, out_shape=jax.ShapeDtypeStruct((m, d), BF16),
        compiler_params=_cparams(("parallel",)),
    )(x, g, mods, mods)


def _resid_norm_mod(x, delta, mods_gt, part_gt, g, mods, part_sh, part_sc, dims, tm=256):
    m, d = x.shape
    row = pl.BlockSpec((tm, d), lambda i: (i, 0))
    return pl.pallas_call(
        _resid_norm_mod_kernel, grid=(m // tm,),
        in_specs=[row, row, _mod_spec(d, part_gt, dims, tm), pl.BlockSpec((1, d), lambda i: (0, 0)),
                  _mod_spec(d, part_sh, dims, tm), _mod_spec(d, part_sc, dims, tm)],
        out_specs=[row, row],
        out_shape=[jax.ShapeDtypeStruct((m, d), F32), jax.ShapeDtypeStruct((m, d), BF16)],
        compiler_params=_cparams(("parallel",)),
    )(x, delta, mods_gt, g, mods, mods)


def _resid_final_norm(x3, delta3, mods, part_gt, g, dims, tm=256):
    b, t, d = x3.shape
    lt = dims["L"] // tm
    blk = pl.BlockSpec((None, tm, d), lambda bi, i: (bi, i, 0))
    return pl.pallas_call(
        _resid_final_norm_kernel, grid=(b, lt),
        in_specs=[blk, blk, pl.BlockSpec((None, 1, d), lambda bi, i: (bi * 6 + part_gt, 0, 0)),
                  pl.BlockSpec((1, d), lambda bi, i: (0, 0))],
        out_specs=blk, out_shape=jax.ShapeDtypeStruct((b, dims["L"], d), F32),
        compiler_params=_cparams(("parallel", "parallel")),
    )(x3, delta3, mods, g)


def _segsum(x, bd):
    hi = x.astype(BF16)
    lo = (x - hi.astype(F32)).astype(BF16)
    return (jnp.dot(hi, bd, preferred_element_type=F32) + jnp.dot(lo, bd, preferred_element_type=F32))


def _rwfeat_kernel(p_ref, pp_ref, pn_ref, mu_ref, w0_ref, wup_ref, a0_ref, aup_ref, gup_ref, kk_ref, ka_ref,
                   rk_ref, bd_ref,
                   r_ref, nkk_ref, v_ref, dec0_ref, dec1_ref, b0_ref, b1_ref, ke0_ref, ke1_ref,
                   bon0_ref, bon1_ref, g0_ref, g1_ref, *, tm, lat_tiles, all_tiles):
    i = pl.program_id(1)
    p = p_ref[...]
    prev_ok = jnp.logical_and(i != 0, i != lat_tiles)
    next_ok = jnp.logical_and(i != lat_tiles - 1, i != all_tiles - 1)
    prow = jnp.where(prev_ok, pp_ref[SUBLANES - 1:SUBLANES, :], 0.0)
    nrow = jnp.where(next_ok, pn_ref[0:1, :], 0.0)
    rid = lax.broadcasted_iota(I32, (tm, 1), 0)
    prev = jnp.where(rid == 0, prow, pltpu.roll(p, 1, 0))
    nxt = jnp.where(rid == tm - 1, nrow, pltpu.roll(p, tm - 1, 0))
    ps = p + (0.5 * (prev + nxt) - p) * mu_ref[...]

    w_ = RW_WIDTH
    r = ps[:, 0:w_]
    k = ps[:, w_:2 * w_]
    v = ps[:, 2 * w_:3 * w_]
    wd = ps[:, 3 * w_:3 * w_ + LANES]
    ad = ps[:, 3 * w_ + LANES:3 * w_ + 2 * LANES]
    gd = ps[:, 3 * w_ + 2 * LANES:3 * w_ + 3 * LANES]
    bd = bd_ref[...]

    lw = jnp.dot(jnp.tanh(wd).astype(BF16), wup_ref[...], preferred_element_type=F32)
    la = jnp.dot(ad.astype(BF16), aup_ref[...], preferred_element_type=F32)
    lg = jnp.dot(_sigmoid(gd).astype(BF16), gup_ref[...], preferred_element_type=F32)

    kk = k * kk_ref[...]
    kk = kk * lax.rsqrt(_segsum(kk * kk, bd) + 1e-12)
    r_ref[...] = r
    nkk_ref[...] = -kk
    v_ref[...] = v
    rk = rk_ref[...]
    ka = ka_ref[...]
    outs = ((dec0_ref, b0_ref, ke0_ref, bon0_ref, g0_ref), (dec1_ref, b1_ref, ke1_ref, bon1_ref, g1_ref))
    for d in range(2):
        dec_ref, b_ref, ke_ref, bon_ref, g_ref = outs[d]
        z = w0_ref[d:d + 1, :] + lw[:, d * w_:(d + 1) * w_]
        nz = -z
        softplus = jnp.maximum(nz, 0.0) + jnp.log(1.0 + jnp.exp(-jnp.abs(nz)))
        wlog = -softplus - 0.5
        dec_ref[...] = jnp.exp(-jnp.exp(wlog))
        a = _sigmoid(a0_ref[d:d + 1, :] + la[:, d * w_:(d + 1) * w_])
        b_ref[...] = kk * a
        ke = k * (1.0 + (a - 1.0) * ka)
        ke_ref[...] = ke
        bon_ref[...] = _segsum(r * ke * rk, bd) * v
        g_ref[...] = lg[:, d * w_:(d + 1) * w_]


def _rwfeat(p3, mu, w0, wup, a0, aup, gup, k_k, k_a, r_k, bd, dims, tm=128):
    b, t, wc = p3.shape
    lt, at = dims["L"] // tm, t // tm
    hb = tm // SUBLANES
    nblk8 = t // SUBLANES
    w_ = RW_WIDTH
    full = lambda arr: pl.BlockSpec(arr.shape, lambda bi, i: (0,) * arr.ndim)
    out_blk = pl.BlockSpec((None, tm, w_), lambda bi, i: (bi, i, 0))
    n_out = 13
    return pl.pallas_call(
        functools.partial(_rwfeat_kernel, tm=tm, lat_tiles=lt, all_tiles=at),
        grid=(b, at),
        in_specs=[pl.BlockSpec((None, tm, wc), lambda bi, i: (bi, i, 0)),
                  pl.BlockSpec((None, SUBLANES, wc), lambda bi, i: (bi, jnp.maximum(i * hb - 1, 0), 0)),
                  pl.BlockSpec((None, SUBLANES, wc), lambda bi, i: (bi, jnp.minimum((i + 1) * hb, nblk8 - 1), 0)),
                  full(mu), full(w0), full(wup), full(a0), full(aup), full(gup), full(k_k), full(k_a),
                  full(r_k), full(bd)],
        out_specs=[out_blk] * n_out,
        out_shape=[jax.ShapeDtypeStruct((b, t, w_), F32)] * n_out,
        compiler_params=_cparams(("parallel", "parallel")),
    )(p3, p3, p3, mu, w0, wup, a0, aup, gup, k_k, k_a, r_k, bd)


SCAN_KH = RW_HEAD // 2


SCAN_NK = 5


def _scan_kernel(*refs, tt):
    kf_refs, kr_refs = refs[:SCAN_NK], refs[SCAN_NK:2 * SCAN_NK]
    vf_ref, vr_ref, yf_ref, yr_ref, s_ref, sa_ref = refs[2 * SCAN_NK:]

    @pl.when(pl.program_id(0) == 0)
    def _():
        s_ref[...] = jnp.zeros_like(s_ref)
        sa_ref[...] = jnp.zeros_like(sa_ref)

    lane = lax.broadcasted_iota(I32, (RW_HEAD, LANES), 1)
    lane1 = lax.broadcasted_iota(I32, (1, LANES), 1)
    quarter = LANES // 4
    fwd1 = (lane1 // quarter) % 2 == 0
    fwd = (lane // quarter) % 2 == 0

    def one_step(t, sa):
        tr = tt - 1 - t
        vt = jnp.where(fwd, vf_ref[t], vr_ref[tr])
        yacc = jnp.zeros((RW_HEAD, LANES), F32)
        sn = jnp.zeros((RW_HEAD, LANES), F32)
        for k in range(SCAN_KH):
            row = lambda q: jnp.where(fwd1, kf_refs[q][t, pl.ds(k, 1), :], kr_refs[q][tr, pl.ds(k, 1), :])
            w, b, ke, r, an = row(0), row(1), row(2), row(3), row(4)
            s = s_ref[k] * w + (sa * b + vt * ke)
            s_ref[k] = s
            yacc = yacc + s * r
            sn = sn + s * an
        y = yacc + pltpu.roll(yacc, LANES // 2, 1)
        sa_new = sn + pltpu.roll(sn, LANES // 2, 1)
        mean = jnp.mean(y, axis=0, keepdims=True)
        dlt = y - mean
        var = jnp.mean(dlt * dlt, axis=0, keepdims=True)
        return sa_new, dlt * lax.rsqrt(var + RW_GN_EPS)

    def body(j, sa):
        sa, y0 = one_step(2 * j, sa)
        sa, y1 = one_step(2 * j + 1, sa)
        low = lane < LANES // 2
        yf_ref[j] = jnp.where(low, y0, y1)
        yr_ref[tt // 2 - 1 - j] = jnp.where(low, y1, y0)
        return sa

    sa_ref[...] = lax.fori_loop(0, tt // 2, body, sa_ref[...])


def _scan(ks, vv, l, tt=32):
    t = vv.shape[0]
    assert len(ks) == SCAN_NK and t % tt == 0 and tt % 2 == 0 and l % tt == 0
    nb, lb = t // tt, l // tt
    cb = nb - lb
    fwd = lambda i: jnp.where(i < cb, lb + i, i - cb)
    rev = lambda i: jnp.where(i < cb, lb + (cb - 1 - i), lb - 1 - (i - cb))
    kv_spec = lambda f: pl.BlockSpec((tt, SCAN_KH, LANES), lambda i: (f(i), 0, 0))
    v_spec = lambda f: pl.BlockSpec((tt, RW_HEAD, LANES), lambda i: (f(i), 0, 0))
    y_spec = lambda f: pl.BlockSpec((tt // 2, RW_HEAD, LANES), lambda i: (f(i), 0, 0))
    y_shape = jax.ShapeDtypeStruct((t // 2, RW_HEAD, LANES), F32)
    return pl.pallas_call(
        functools.partial(_scan_kernel, tt=tt),
        grid=(nb,),
        in_specs=[kv_spec(fwd)] * SCAN_NK + [kv_spec(rev)] * SCAN_NK + [v_spec(fwd), v_spec(rev)],
        out_specs=[y_spec(fwd), y_spec(rev)],
        out_shape=[y_shape, y_shape],
        scratch_shapes=[pltpu.VMEM((SCAN_KH, RW_HEAD, LANES), F32), pltpu.VMEM((RW_HEAD, LANES), F32)],
        compiler_params=_cparams(("arbitrary",)),
    )(*ks, *ks, vv, vv)


def _to_scan_k(x0, x1):
    b, t, _ = x0.shape
    s = jnp.stack([x0, x1], 0).reshape(2, b, t, RW_HEADS, 2, SCAN_KH)
    return s.transpose(2, 5, 4, 0, 1, 3).reshape(t, SCAN_KH, 4 * b * RW_HEADS)


def _to_scan_v(x):
    b, t, _ = x.shape
    s = x.reshape(b, t, RW_HEADS, RW_HEAD).transpose(1, 3, 0, 2).reshape(t, RW_HEAD, b * RW_HEADS)
    return jnp.concatenate([s, s, s, s], -1)


def _next_step_rows(x, l):
    lat, ctx = x[:, :l], x[:, l:]
    z = jnp.zeros_like(x[:, :1])
    nxt_f = jnp.concatenate([lat[:, 1:], z, ctx[:, 1:], lat[:, :1]], 1)
    nxt_r = jnp.concatenate([z, lat[:, :-1], lat[:, l - 1:l], ctx[:, :-1]], 1)
    return nxt_f, nxt_r


def _from_scan_y(yf, yr, b):
    t2 = yf.shape[0]
    q = b * RW_HEADS
    outs = []
    for d, y in enumerate((yf, yr)):
        s = y.reshape(t2, RW_HEAD, 2, 2 * q)[..., d * q:(d + 1) * q]
        outs.append(s.reshape(t2, RW_HEAD, 2, b, RW_HEADS).transpose(3, 0, 2, 4, 1).reshape(b, 2 * t2, RW_WIDTH))
    return outs[0], outs[1]


def _readout_kernel(y0_ref, y1_ref, bon0_ref, bon1_ref, g0_ref, g1_ref, lw_ref, lb_ref, o_ref):
    lw, lb = lw_ref[...], lb_ref[...]
    o = (y0_ref[...] * lw + lb + bon0_ref[...]) * g0_ref[...]
    o = o + (y1_ref[...] * lw + lb + bon1_ref[...]) * g1_ref[...]
    o_ref[...] = o.astype(o_ref.dtype)


def _readout(y0, y1, bon0, bon1, g0, g1, ln_w, ln_b, tm=256):
    m, w_ = y0.shape
    row = pl.BlockSpec((tm, w_), lambda i: (i, 0))
    vec = pl.BlockSpec((1, w_), lambda i: (0, 0))
    return pl.pallas_call(
        _readout_kernel, grid=(m // tm,), in_specs=[row] * 6 + [vec, vec], out_specs=row,
        out_shape=jax.ShapeDtypeStruct((m, w_), BF16), compiler_params=_cparams(("parallel",)),
    )(y0, y1, bon0, bon1, g0, g1, ln_w, ln_b)


def _mla_prep_kernel(p_ref, c_ref, s_ref, qn_ref, kn_ref, wqa_ref, wqs_ref, wk_ref, wv_ref, q_ref, k_ref, v_ref):
    p = p_ref[...]
    cos, sin = c_ref[...], s_ref[...]
    qc = _rms(p[:, :MLA_Q_RANK], qn_ref[...]).astype(BF16)
    kvc = _rms(p[:, MLA_Q_RANK:MLA_Q_RANK + MLA_KV_RANK], kn_ref[...]).astype(BF16)
    kr = p[:, MLA_Q_RANK + MLA_KV_RANK:MLA_Q_RANK + MLA_KV_RANK + LANES]
    krs = p[:, MLA_Q_RANK + MLA_KV_RANK + LANES:MLA_Q_RANK + MLA_KV_RANK + 2 * LANES]
    krope = kr * cos + krs * sin
    qa = jnp.dot(qc, wqa_ref[...], preferred_element_type=F32)
    qs = jnp.dot(qc, wqs_ref[...], preferred_element_type=F32)
    kk = jnp.dot(kvc, wk_ref[...], preferred_element_type=F32)
    v_ref[...] = jnp.dot(kvc, wv_ref[...], preferred_element_type=F32).astype(v_ref.dtype)
    scale = MLA_QK ** -0.5
    for h in range(MLA_HEADS):
        sl = slice(h * LANES, (h + 1) * LANES)
        q_ref[:, sl] = ((qa[:, sl] * cos + qs[:, sl] * sin) * scale).astype(q_ref.dtype)
        k_ref[:, sl] = (kk[:, sl] + krope).astype(k_ref.dtype)


def _mla_prep(p, cos_t, sin_t, q_norm, kv_norm, wqa, wqs, wk, wv, dims, tm=256):
    m, pc = p.shape
    tpb = dims["T"] // tm
    full = lambda arr: pl.BlockSpec(arr.shape, lambda i: (0,) * arr.ndim)
    tab = pl.BlockSpec((tm, LANES), lambda i: (i % tpb, 0))
    hw = MLA_HEADS * LANES
    return pl.pallas_call(
        _mla_prep_kernel, grid=(m // tm,),
        in_specs=[pl.BlockSpec((tm, pc), lambda i: (i, 0)), tab, tab, full(q_norm), full(kv_norm),
                  full(wqa), full(wqs), full(wk), full(wv)],
        out_specs=[pl.BlockSpec((tm, hw), lambda i: (i, 0)), pl.BlockSpec((tm, hw), lambda i: (i, 0)),
                   pl.BlockSpec((tm, MLA_HEADS * MLA_V), lambda i: (i, 0))],
        out_shape=[jax.ShapeDtypeStruct((m, hw), BF16), jax.ShapeDtypeStruct((m, hw), BF16),
                   jax.ShapeDtypeStruct((m, MLA_HEADS * MLA_V), BF16)],
        compiler_params=_cparams(("parallel",)),
    )(p, cos_t, sin_t, q_norm, kv_norm, wqa, wqs, wk, wv)


def _flash_kernel(q_ref, k_ref, v_ref, *rest, nk):
    o_ref = rest[-7]
    state = (rest[-6:-3], rest[-3:])
    ki = pl.program_id(3)

    @pl.when(ki == 0)
    def _():
        for m_ref, l_ref, acc_ref in state:
            m_ref[...] = jnp.full_like(m_ref, -jnp.inf)
            l_ref[...] = jnp.zeros_like(l_ref)
            acc_ref[...] = jnp.zeros_like(acc_ref)

    v = v_ref[...]
    for hh, (m_ref, l_ref, acc_ref) in enumerate(state):
        q = q_ref[:, hh * LANES:(hh + 1) * LANES]
        k = k_ref[:, hh * LANES:(hh + 1) * LANES]
        s = lax.dot_general(q, k, (((1,), (1,)), ((), ())), preferred_element_type=F32)
        m_prev = m_ref[...]
        m_new = jnp.maximum(m_prev, jnp.max(s, -1, keepdims=True))
        alpha = jnp.exp(m_prev - m_new)
        pr = jnp.exp(s - m_new)
        l_ref[...] = alpha * l_ref[...] + jnp.sum(pr, -1, keepdims=True)
        acc_ref[...] = alpha * acc_ref[...] + jnp.dot(pr.astype(BF16), v, preferred_element_type=F32)
        m_ref[...] = m_new

    @pl.when(ki == nk - 1)
    def _():
        (_, l0, acc0), (_, l1, acc1) = state
        lane = lax.broadcasted_iota(I32, acc0.shape, 1)
        o_ref[...] = jnp.where(lane < MLA_V, acc0[...] / l0[...], acc1[...] / l1[...]).astype(o_ref.dtype)


def _flash(q3, k3, v3, *, tq, tk, q_off, nq, k_off, nk, prev_out=None):
    b, t, _ = q3.shape
    hp = MLA_HEADS // 2
    in_specs = [pl.BlockSpec((None, tq, 2 * LANES), lambda bi, h, qi, ki: (bi, q_off + qi, h)),
                pl.BlockSpec((None, tk, 2 * LANES), lambda bi, h, qi, ki: (bi, k_off + ki, h)),
                pl.BlockSpec((None, tk, LANES), lambda bi, h, qi, ki: (bi, k_off + ki, h))]
    args = [q3, k3, v3]
    aliases = {}
    if prev_out is not None:
        in_specs.append(pl.BlockSpec(memory_space=pl.ANY))
        args.append(prev_out)
        aliases = {3: 0}
    return pl.pallas_call(
        functools.partial(_flash_kernel, nk=nk),
        grid=(b, hp, nq, nk),
        in_specs=in_specs,
        out_specs=pl.BlockSpec((None, tq, LANES), lambda bi, h, qi, ki: (bi, q_off + qi, h)),
        out_shape=jax.ShapeDtypeStruct((b, t, MLA_HEADS * MLA_V), BF16),
        scratch_shapes=[pltpu.VMEM((tq, 1), F32), pltpu.VMEM((tq, 1), F32), pltpu.VMEM((tq, LANES), F32)] * 2,
        input_output_aliases=aliases,
        compiler_params=_cparams(("parallel", "parallel", "parallel", "arbitrary")),
    )(*args)


def _topk_rows(s, kk, payload=None):
    nrow = s.shape[0]
    rid = lax.broadcasted_iota(I32, s.shape, 0)
    vals, sel = [], []
    for _ in range(kk):
        m = jnp.max(s, axis=0, keepdims=True)
        pos = jnp.min(jnp.where(s == m, rid, nrow), axis=0, keepdims=True)
        hit = rid == pos
        vals.append(m)
        sel.append(pos if payload is None else jnp.max(jnp.where(hit, payload, -1), axis=0, keepdims=True))
        s = jnp.where(hit, -jnp.inf, s)
    return jnp.concatenate(vals, 0), jnp.concatenate(sel, 0)


def _peer_topk_kernel(q0_ref, q1_ref, keys_ref, idx_ref, gate_ref):
    nt = (((1,), (1,)), ((), ()))
    s0 = lax.dot_general(keys_ref[0].astype(BF16), q0_ref[...].astype(BF16), nt, preferred_element_type=F32)
    s1 = lax.dot_general(keys_ref[1].astype(BF16), q1_ref[...].astype(BF16), nt, preferred_element_type=F32)
    v1, i1 = _topk_rows(s0, PEER_TOPK)
    v2, i2 = _topk_rows(s1, PEER_TOPK)
    cand_s = jnp.concatenate([v1[a:a + 1, :] + v2 for a in range(PEER_TOPK)], 0)
    cand_i = jnp.concatenate([i1[a:a + 1, :] * PEER_NKEYS + i2 for a in range(PEER_TOPK)], 0)
    top_s, top_i = _topk_rows(cand_s, PEER_TOPK, payload=cand_i)
    e = jnp.exp(top_s - top_s[0:1, :])
    idx_ref[...] = top_i
    gate_ref[...] = e / jnp.sum(e, axis=0, keepdims=True)


def _peer_topk(q, keys, tm=256):
    m = q.shape[0]
    out_blk = pl.BlockSpec((None, PEER_TOPK, tm), lambda i, h: (h, 0, i))
    return pl.pallas_call(
        _peer_topk_kernel, grid=(m // tm, PEER_HEADS),
        in_specs=[pl.BlockSpec((tm, PEER_HALF), lambda i, h: (i, 2 * h)),
                  pl.BlockSpec((tm, PEER_HALF), lambda i, h: (i, 2 * h + 1)),
                  pl.BlockSpec((None, 2, PEER_NKEYS, PEER_HALF), lambda i, h: (h, 0, 0, 0))],
        out_specs=[out_blk, out_blk],
        out_shape=[jax.ShapeDtypeStruct((PEER_HEADS, PEER_TOPK, m), I32),
                   jax.ShapeDtypeStruct((PEER_HEADS, PEER_TOPK, m), F32)],
        compiler_params=_cparams(("parallel", "parallel")),
    )(q, q, keys)


PEER_GROUP = 16
ROW_SUB = 16
PLAN_OFF_W = 16
PLAN_BATCH = 16
BITREV8 = (0, 4, 2, 6, 1, 5, 3, 7)


def _gelu(x):
    return 0.5 * x * (1.0 + lax.erf(x * (2.0 ** -0.5)))


def _fold_pair(xa, xb, k, sub):
    mask = (sub & (2 * k - 1)) < k
    a = jnp.where(mask, xa, xb)
    if 2 * k == SUBLANES:
        return a + pltpu.roll(jnp.where(mask, xb, xa), k, 0)
    return a + jnp.where(mask, pltpu.roll(xa, SUBLANES - k, 0), pltpu.roll(xb, k, 0))


def _expert_kernel(rows_ref, gates_ref, goff_ref, h_ref, u_ref, v_ref, oin_ref, o_ref, part_ref, *, tb, gpt):
    c = pl.program_id(0)
    g8 = SUBLANES
    spg = PEER_GROUP // g8
    sub = lax.broadcasted_iota(I32, (g8, LANES), 0)

    @pl.when(jnp.logical_and(c == 0, pl.program_id(1) == 0))
    def _():
        part_ref[...] = jnp.zeros_like(part_ref)

    def groups(n):
        return goff_ref[n * PLAN_OFF_W + c], goff_ref[n * PLAN_OFF_W + c + 1]

    def tok_a(n, slot):
        g0, g1 = groups(n)
        h = h_ref[n].astype(F32)

        def grp(g, slot):
            for s in range(spg):
                base = (n * gpt + g) * PEER_GROUP + s * g8
                xs = []
                for i in range(g8):
                    prod = h * u_ref[rows_ref[base + i]].astype(F32)
                    xs.append(prod[0:SUBLANES] + prod[SUBLANES:ROW_SUB])
                xs = [xs[BITREV8[j]] for j in range(g8)]
                k = g8 // 2
                while k >= 1:
                    xs = [_fold_pair(xs[2 * j], xs[2 * j + 1], k, sub) for j in range(len(xs) // 2)]
                    k //= 2
                part_ref[slot + s] = xs[0]
            return slot + spg

        return lax.fori_loop(g0, g1, grp, slot)

    nslot = lax.fori_loop(0, tb, tok_a, 0)

    def slots_b(j, carry):
        sl = pl.ds(pl.multiple_of(j * PLAN_BATCH, PLAN_BATCH), PLAN_BATCH)
        act = jnp.sum(part_ref[sl], axis=-1, keepdims=True)
        part_ref[sl] = jnp.broadcast_to(_gelu(act), (PLAN_BATCH, g8, LANES))
        return carry

    lax.fori_loop(0, (nslot + PLAN_BATCH - 1) // PLAN_BATCH, slots_b, 0)

    def tok_c(n, slot):
        g0, g1 = groups(n)

        def grp(g, carry):
            slot, acc = carry
            for s in range(spg):
                base = (n * gpt + g) * PEER_GROUP + s * g8
                w8 = part_ref[slot + s]
                for i in range(g8):
                    w = w8[i:i + 1, :] * gates_ref[base + i]
                    acc = acc + w * v_ref[rows_ref[base + i]].astype(F32)
            return slot + spg, acc

        slot, acc = lax.fori_loop(g0, g1, grp, (slot, jnp.zeros((ROW_SUB, LANES), F32)))
        o_ref[n] = oin_ref[n] + acc
        return slot

    lax.fori_loop(0, tb, tok_c, 0)


def _expert_plan(idx, gate, n_exp, ch):
    m, nsel = idx.shape
    nch = n_exp // ch
    g8 = PEER_GROUP
    nfill = g8 - 1
    slots = -(-(nsel + nfill * nch) // g8) * g8
    cidx = idx // ch
    local = idx - cidx * ch
    chunks = jnp.arange(nch, dtype=I32)
    cnt = jnp.sum((cidx[:, :, None] == chunks[None, None, :]).astype(I32), axis=1)
    pad = (-cnt) % g8
    goff = jnp.cumsum((cnt + pad) // g8, axis=1)
    goff = jnp.concatenate([jnp.zeros((m, 1), I32), goff, jnp.zeros((m, PLAN_OFF_W - nch - 1), I32)], 1)
    span = 2 * ch
    fill_c = jnp.repeat(chunks, nfill)
    fill_r = jnp.tile(jnp.arange(nfill, dtype=I32), nch)
    active = fill_r[None, :] < pad[:, fill_c]
    fill_key = jnp.where(active, fill_c[None, :] * span + ch + fill_r[None, :], nch * span)
    ndead = slots - nsel - nfill * nch
    keys = jnp.concatenate([cidx * span + local, fill_key, jnp.full((m, ndead), nch * span, I32)], 1)
    zeros = jnp.zeros((m, slots - nsel), I32)
    rows = jnp.concatenate([local, zeros], 1)
    gates = jnp.concatenate([gate, zeros.astype(F32)], 1)
    _, rows, gates = lax.sort((keys, rows, gates), dimension=1, num_keys=1)
    return rows.reshape(-1), gates.reshape(-1), goff.reshape(-1), slots


def _experts(rows, gates, goff, slots, h3, u3, v3, *, tb, ch):
    m = h3.shape[0]
    e = u3.shape[0]
    gpt = slots // PEER_GROUP
    assert h3.shape[-1] == LANES and e % ch == 0 and m % tb == 0
    smem = lambda w: pl.BlockSpec((tb * w,), lambda c, t: (t,), memory_space=pltpu.SMEM)
    tok_blk = pl.BlockSpec((tb, ROW_SUB, LANES), lambda c, t: (t, 0, 0))
    tab_blk = pl.BlockSpec((ch, ROW_SUB, LANES), lambda c, t: (c, 0, 0), pipeline_mode=pl.Buffered(1))
    out0 = jnp.zeros((m, ROW_SUB, LANES), F32)
    return pl.pallas_call(
        functools.partial(_expert_kernel, tb=tb, gpt=gpt),
        grid=(e // ch, m // tb),
        in_specs=[smem(slots), smem(slots), smem(PLAN_OFF_W), tok_blk, tab_blk, tab_blk, tok_blk],
        out_specs=tok_blk,
        out_shape=jax.ShapeDtypeStruct((m, ROW_SUB, LANES), F32),
        scratch_shapes=[pltpu.VMEM((tb * slots // SUBLANES + PLAN_BATCH, SUBLANES, LANES), F32)],
        input_output_aliases={6: 0},
        compiler_params=_cparams(("arbitrary", "arbitrary")),
    )(rows, gates, goff, h3, u3, v3, out0)


def _peer_gates_kernel(idx_ref, gate_ref, g_ref, *, tm, unroll):
    rid = lax.broadcasted_iota(I32, (PEER_NKEYS, PEER_SEL), 0)
    nt = (((1,), (1,)), ((), ()))

    def body(t, carry):
        for u in range(unroll):
            n = t * unroll + u
            idx = idx_ref[pl.ds(n, 1), :]
            gate = gate_ref[pl.ds(n, 1), :]
            i1 = idx // PEER_NKEYS
            i2 = idx - i1 * PEER_NKEYS
            a = jnp.where(rid == i1, gate, 0.0).astype(BF16)
            b = jnp.where(rid == i2, 1.0, 0.0).astype(BF16)
            g_ref[n] = lax.dot_general(a, b, nt, preferred_element_type=F32).astype(g_ref.dtype)
        return carry

    lax.fori_loop(0, tm // unroll, body, 0)


def _peer_gates(idx, gate, tm=128, unroll=4):
    m = idx.shape[0]
    sel = pl.BlockSpec((tm, PEER_SEL), lambda i: (i, 0))
    return pl.pallas_call(
        functools.partial(_peer_gates_kernel, tm=tm, unroll=unroll), grid=(m // tm,),
        in_specs=[sel, sel],
        out_specs=pl.BlockSpec((tm, PEER_NKEYS, PEER_NKEYS), lambda i: (i, 0, 0)),
        out_shape=jax.ShapeDtypeStruct((m, PEER_NKEYS, PEER_NKEYS), BF16),
        compiler_params=_cparams(("parallel",)),
    )(idx, gate)


def _peer_dense_kernel(h_ref, u_ref, v_ref, g_ref, oin_ref, o_ref):
    act = lax.dot_general(h_ref[...], u_ref[...], (((1,), (1,)), ((), ())), preferred_element_type=F32)
    g = g_ref[...].astype(F32).reshape(act.shape)
    wt = jnp.where(g != 0.0, _gelu(act) * g, 0.0).astype(BF16)
    o_ref[...] = oin_ref[...] + jnp.dot(wt, v_ref[...], preferred_element_type=F32)


def _peer_dense(h2, u_tab, v_tab, g, *, tm, te):
    m, d = h2.shape
    e = v_tab.shape[0]
    assert m % tm == 0 and e % te == 0
    tok = pl.BlockSpec((tm, d), lambda j, i: (i, 0))
    tab = pl.BlockSpec((te, d), lambda j, i: (j, 0), pipeline_mode=pl.Buffered(1))
    return pl.pallas_call(
        _peer_dense_kernel, grid=(e // te, m // tm),
        in_specs=[tok, tab, tab, pl.BlockSpec((tm, te // PEER_NKEYS, PEER_NKEYS), lambda j, i: (i, j, 0)), tok],
        out_specs=tok,
        out_shape=jax.ShapeDtypeStruct((m, d), F32),
        input_output_aliases={4: 0},
        compiler_params=_cparams(("arbitrary", "arbitrary")),
    )(h2, u_tab, v_tab, g, jnp.zeros((m, d), F32))


def _block_diag2(w):
    z = jnp.zeros_like(w[0])
    return jnp.concatenate([jnp.concatenate([w[0], z], 1), jnp.concatenate([z, w[1]], 1)], 0)


def _per_head_pad(w, width, pad_to):
    k = w.shape[0]
    w = w.reshape(k, -1, width)
    return jnp.pad(w, ((0, 0), (0, 0), (0, pad_to - width))).reshape(k, -1)


def _rope_swap_cols(w_rope):
    half = MLA_ROPE // 2
    return jnp.concatenate([-w_rope[..., half:], w_rope[..., :half]], -1)


def _prep_layer_weights(l, w_in, mla_w_uq, mla_w_ukv, rk_w_up, rk_a_up, rk_g_up):
    d = w_in.shape[1]
    wi = w_in[l]
    lo, hi = RW_COLS, RW_COLS + MLA_Q_RANK + MLA_KV_RANK + MLA_ROPE
    w_rw = jnp.pad(wi[:, :lo], ((0, 0), (0, RW_COLS_PAD - RW_COLS))).astype(BF16)
    w_kr = wi[:, hi - MLA_ROPE:hi]
    place = lambda w: jnp.pad(w, ((0, 0), (MLA_NOPE, LANES - MLA_NOPE - MLA_ROPE)))
    w_mla = jnp.concatenate([wi[:, lo:hi - MLA_ROPE], place(w_kr), place(_rope_swap_cols(w_kr))], 1).astype(BF16)
    w_gate = wi[:, hi:].astype(BF16)

    uq = mla_w_uq[l].reshape(MLA_Q_RANK, MLA_HEADS, MLA_QK)
    zeros = jnp.zeros((MLA_Q_RANK, MLA_HEADS, LANES - MLA_QK), F32)
    wqa = jnp.concatenate([uq, zeros], -1).reshape(MLA_Q_RANK, -1).astype(BF16)
    zn = jnp.zeros((MLA_Q_RANK, MLA_HEADS, MLA_NOPE), F32)
    wqs = jnp.concatenate([zn, _rope_swap_cols(uq[..., MLA_NOPE:]), zeros], -1).reshape(MLA_Q_RANK, -1).astype(BF16)
    ukv = mla_w_ukv[l].reshape(MLA_KV_RANK, MLA_HEADS, MLA_NOPE + MLA_V)
    wk = jnp.pad(ukv[..., :MLA_NOPE], ((0, 0), (0, 0), (0, LANES - MLA_NOPE))).reshape(MLA_KV_RANK, -1).astype(BF16)
    wv = ukv[..., MLA_NOPE:].reshape(MLA_KV_RANK, -1).astype(BF16)

    wup = _block_diag2(rk_w_up[l]).astype(BF16)
    aup = _block_diag2(rk_a_up[l]).astype(BF16)
    gup = jnp.pad(_block_diag2(rk_g_up[l]), ((0, LANES - 2 * RW_GATE_LORA), (0, 0))).astype(BF16)
    return dict(w_rw=w_rw, w_mla=w_mla, w_gate=w_gate, wqa=wqa, wqs=wqs, wk=wk, wv=wv, wup=wup, aup=aup, gup=gup)


def _rope_lane_tables(l, c):
    rows = l // GRID_W
    row = jnp.repeat(jnp.arange(rows, dtype=F32), GRID_W)
    col = jnp.tile(jnp.arange(GRID_W, dtype=F32), rows)
    n_freq = MLA_ROPE // 4
    freqs = ROPE_BASE ** (-jnp.arange(n_freq, dtype=F32) / n_freq)
    ang = jnp.concatenate([row[:, None] * freqs, col[:, None] * freqs], -1)
    cos, sin = jnp.cos(ang), jnp.sin(ang)
    tail = LANES - MLA_NOPE - MLA_ROPE
    cos_t = jnp.concatenate([jnp.ones((l, MLA_NOPE), F32), cos, cos, jnp.ones((l, tail), F32)], -1)
    sin_t = jnp.concatenate([jnp.zeros((l, MLA_NOPE), F32), sin, sin, jnp.zeros((l, tail), F32)], -1)
    cos_t = jnp.concatenate([cos_t, jnp.ones((c, LANES), F32)], 0)
    sin_t = jnp.concatenate([sin_t, jnp.zeros((c, LANES), F32)], 0)
    return cos_t, sin_t


def _peer(h2, w_q, keys, u_tab, v_tab, *, te, tm=256):
    m, d = h2.shape
    q = _mm(h2, w_q, tm=256, tn=_pick_tile(w_q.shape[1], 1024))
    idx_t, gate_t = _peer_topk(q, keys)
    idx = idx_t.transpose(2, 0, 1).reshape(m, PEER_SEL)
    gate = gate_t.transpose(2, 0, 1).reshape(m, PEER_SEL)
    g = _peer_gates(idx, gate)
    return _peer_dense(h2, u_tab, v_tab, g, tm=tm, te=te)


def _forward(x, c, ctx, c_ctx, ada_w, ada_b, norm_mix_g, w_in, gate_b, rk_mu, rk_w0, rk_w_up, rk_a0,
             rk_a_up, rk_g_up, rk_k_k, rk_k_a, rk_r_k, rk_ln_w, rk_ln_b, mla_q_norm, mla_w_uq, mla_kv_norm,
             mla_w_ukv, w_branch_a, w_branch_b, w_out, norm_ffn_g, peer_w_q, peer_keys, peer_u, peer_v,
             final_norm_g, *, flash_tq=512, flash_tk=2816, scan_tt=32, peer_te=2048):
    nb, l, d = x.shape
    cl = ctx.shape[1]
    t = l + cl
    m = nb * t
    depth = w_in.shape[0]
    dims = dict(B=nb, L=l, C=cl, T=t)
    assert nb == 2 and 4 * nb * RW_HEADS == LANES and l % 256 == 0 and cl % 256 == 0

    xa = jnp.concatenate([x, ctx], 1).reshape(m, d)
    cos_t, sin_t = _rope_lane_tables(l, cl)
    seg = jnp.arange(RW_WIDTH, dtype=I32) // RW_HEAD
    bd = (seg[:, None] == seg[None, :]).astype(BF16)
    cvec = jnp.zeros((SUBLANES, d), F32).at[:nb].set(jax.nn.silu(c)).at[nb].set(jax.nn.silu(c_ctx))

    delta, mods_prev = None, None
    out = None
    for li in range(depth):
        wl = _prep_layer_weights(li, w_in, mla_w_uq, mla_w_ukv, rk_w_up, rk_a_up, rk_g_up)
        tn_ada = _pick_tile(6 * d, 768)
        mods = _mm(cvec, ada_w[li], tm=SUBLANES, tn=tn_ada, epilogue=lambda acc, bias: acc + bias,
                   extras=[(ada_b[li][None, :], pl.BlockSpec((1, tn_ada), lambda j, i: (0, j)))])
        mods = mods[:nb + 1].reshape((nb + 1) * 6, 1, d)
        g_mix = norm_mix_g[li][None, :]
        if li == 0:
            h = _norm_mod(xa, g_mix, mods, 0, 1, dims)
        else:
            xa, h = _resid_norm_mod(xa, delta, mods_prev, 5, g_mix, mods, 0, 1, dims)

        p_rw = _mm(h, wl["w_rw"], tm=256, tn=_pick_tile(RW_COLS_PAD, 1152))
        p_mla = _mm(h, wl["w_mla"], tm=256, tn=MLA_P_COLS)
        p_gate = _mm(h, wl["w_gate"], tm=256, tn=1024)

        mu = jnp.pad(rk_mu[li], (0, RW_COLS_PAD - RW_COLS))[None, :]
        feats = _rwfeat(p_rw.reshape(nb, t, RW_COLS_PAD), mu, rk_w0[li], wl["wup"], rk_a0[li], wl["aup"], wl["gup"],
                        rk_k_k[li][None, :], rk_k_a[li][None, :], rk_r_k[li].reshape(1, RW_WIDTH), bd, dims)
        r, nkk, v, dec0, dec1, b0, b1, ke0, ke1, bon0, bon1, g0, g1 = feats
        ks = (_to_scan_k(dec0, dec1), _to_scan_k(b0, b1), _to_scan_k(ke0, ke1), _to_scan_k(r, r),
              _to_scan_k(*_next_step_rows(nkk, l)))
        ynf, ynr = _scan(ks, _to_scan_v(v), l, tt=scan_tt)
        yn0, yn1 = _from_scan_y(ynf, ynr, nb)
        flat = lambda z: z.reshape(m, RW_WIDTH)
        ya = _readout(flat(yn0), flat(yn1), flat(bon0), flat(bon1), flat(g0), flat(g1),
                      rk_ln_w[li][None, :], rk_ln_b[li][None, :])

        q, k, vv = _mla_prep(p_mla, cos_t, sin_t, mla_q_norm[li][None, :], mla_kv_norm[li][None, :],
                             wl["wqa"], wl["wqs"], wl["wk"], wl["wv"], dims)
        q3, k3, v3 = (z.reshape(nb, t, -1) for z in (q, k, vv))
        yb = _flash(q3, k3, v3, tq=flash_tq, tk=flash_tk, q_off=0, nq=l // flash_tq, k_off=0, nk=t // flash_tk)
        yb = _flash(q3, k3, v3, tq=cl, tk=cl, q_off=l // cl, nq=1, k_off=l // cl, nk=1, prev_out=yb)
        yb = yb.reshape(m, MLA_HEADS * MLA_V)

        tpb, lt = t // 256, l // 256
        br_a = _mm(ya, w_branch_a[li].astype(BF16), tm=256, tn=1024)
        gb = gate_b[li]
        merged = _mm(
            yb, w_branch_b[li].astype(BF16), tm=256, tn=1024, out_dtype=BF16,
            epilogue=lambda acc, ga, gbb, ba, bb, a: _sigmoid(ga + ba) * a + _sigmoid(gbb + bb) * acc,
            extras=[(p_gate, pl.BlockSpec((256, 1024), lambda j, i: (i, j))),
                    (p_gate, pl.BlockSpec((256, 1024), lambda j, i: (i, j + d // 1024))),
                    (gb[0][None, :], pl.BlockSpec((1, 1024), lambda j, i: (0, j))),
                    (gb[1][None, :], pl.BlockSpec((1, 1024), lambda j, i: (0, j))),
                    (br_a, pl.BlockSpec((256, 1024), lambda j, i: (i, j)))])
        x1 = _mm(merged, w_out[li].astype(BF16), tm=256, tn=1024,
                 epilogue=lambda acc, xr, gt: xr + gt * acc,
                 extras=[(xa, pl.BlockSpec((256, 1024), lambda j, i: (i, j))),
                         (mods, pl.BlockSpec((None, 1, 1024), lambda j, i: (_mod_row(i, tpb, lt, 2, nb), 0, j)))])

        h2 = _norm_mod(x1, norm_ffn_g[li][None, :], mods, 3, 4, dims)
        delta = _peer(h2, peer_w_q[li].astype(BF16), peer_keys[li], peer_u[li].astype(BF16),
                      peer_v[li].astype(BF16), te=peer_te)
        xa, mods_prev = x1, mods

    out = _resid_final_norm(xa.reshape(nb, t, d), delta.reshape(nb, t, d), mods_prev, 5,
                            final_norm_g[None, :], dims)
    return out


def kernel(x, c, ctx, c_ctx, ada_w, ada_b, norm_mix_g, w_in, gate_b, rk_mu, rk_w0, rk_w_up, rk_a0, rk_a_up, rk_g_up, rk_k_k, rk_k_a, rk_r_k, rk_ln_w, rk_ln_b, mla_q_norm, mla_w_uq, mla_kv_norm, mla_w_ukv, w_branch_a, w_branch_b, w_out, norm_ffn_g, peer_w_q, peer_keys, peer_u, peer_v, final_norm_g):
    return _forward(x, c, ctx, c_ctx, ada_w, ada_b, norm_mix_g, w_in, gate_b, rk_mu, rk_w0, rk_w_up, rk_a0,
                    rk_a_up, rk_g_up, rk_k_k, rk_k_a, rk_r_k, rk_ln_w, rk_ln_b, mla_q_norm, mla_w_uq,
                    mla_kv_norm, mla_w_ukv, w_branch_a, w_branch_b, w_out, norm_ffn_g, peer_w_q, peer_keys,
                    peer_u, peer_v, final_norm_g)
```

```python
import functools

import jax
import jax.numpy as jnp
from jax import lax
from jax.experimental import pallas as pl
from jax.experimental.pallas import tpu as pltpu

F32 = jnp.float32
BF16 = jnp.bfloat16
I32 = jnp.int32

NORM_EPS = 1e-6
GRID_W = 64
ROPE_BASE = 10000.0

RW_HEADS = 16
RW_HEAD = 64
RW_WIDTH = RW_HEADS * RW_HEAD
RW_DECAY_LORA = 64
RW_A_LORA = 64
RW_GATE_LORA = 32
RW_GN_EPS = 64e-5
RW_COLS = 3 * RW_WIDTH + 2 * (RW_DECAY_LORA + RW_A_LORA + RW_GATE_LORA)
RW_COLS_PAD = 3456

MLA_HEADS = 16
MLA_NOPE = 64
MLA_ROPE = 32
MLA_QK = MLA_NOPE + MLA_ROPE
MLA_V = 64
MLA_Q_RANK = 512
MLA_KV_RANK = 256
MLA_HEAD_PAD = 128
MLA_P_COLS = MLA_Q_RANK + MLA_KV_RANK + 2 * MLA_HEAD_PAD

PEER_HEADS = 8
PEER_NKEYS = 128
PEER_HALF = 128
PEER_TOPK = 16
PEER_SEL = PEER_HEADS * PEER_TOPK

LANES = 128
SUBLANES = 8
VMEM_LIMIT = 56 * 1024 * 1024


def _cparams(sem):
    return pltpu.CompilerParams(dimension_semantics=sem, vmem_limit_bytes=VMEM_LIMIT)


def _pick_tile(n, cap, mult=LANES):
    best = None
    t = mult
    while t <= min(n, cap):
        if n % t == 0:
            best = t
        t += mult
    assert best is not None, (n, cap)
    return best


def _mm_kernel(*refs, n_extra, epilogue):
    a_ref, w_ref = refs[0], refs[1]
    extra = refs[2:2 + n_extra]
    o_ref = refs[2 + n_extra]
    acc = jnp.dot(a_ref[...].astype(BF16), w_ref[...].astype(BF16), preferred_element_type=F32)
    if epilogue is not None:
        acc = epilogue(acc, *[e[...] for e in extra])
    o_ref[...] = acc.astype(o_ref.dtype)


def _mm(a, w, *, tm, tn, out_dtype=F32, epilogue=None, extras=()):
    m, k = a.shape
    k2, n = w.shape
    assert k == k2 and m % tm == 0 and n % tn == 0, (a.shape, w.shape, tm, tn)
    in_specs = [pl.BlockSpec((tm, k), lambda j, i: (i, 0)),
                pl.BlockSpec((k, tn), lambda j, i: (0, j))] + [s for _, s in extras]
    return pl.pallas_call(
        functools.partial(_mm_kernel, n_extra=len(extras), epilogue=epilogue),
        grid=(n // tn, m // tm),
        in_specs=in_specs,
        out_specs=pl.BlockSpec((tm, tn), lambda j, i: (i, j)),
        out_shape=jax.ShapeDtypeStruct((m, n), out_dtype),
        compiler_params=_cparams(("parallel", "parallel")),
    )(a, w, *[x for x, _ in extras])


def _sigmoid(x):
    return 1.0 / (1.0 + jnp.exp(-x))


def _mod_row(i, tiles_per_batch, lat_tiles, part, n_batch):
    which = jnp.where(i % tiles_per_batch >= lat_tiles, n_batch, i // tiles_per_batch)
    return which * 6 + part


def _rms(x, g):
    return x * lax.rsqrt(jnp.mean(x * x, -1, keepdims=True) + NORM_EPS) * g


def _norm_mod_kernel(x_ref, g_ref, sh_ref, sc_ref, h_ref):
    y = _rms(x_ref[...], g_ref[...])
    h_ref[...] = (y * (1.0 + sc_ref[...]) + sh_ref[...]).astype(h_ref.dtype)


def _resid_norm_mod_kernel(x_ref, d_ref, gt_ref, g_ref, sh_ref, sc_ref, xo_ref, h_ref):
    x = x_ref[...] + gt_ref[...] * d_ref[...]
    xo_ref[...] = x
    y = _rms(x, g_ref[...])
    h_ref[...] = (y * (1.0 + sc_ref[...]) + sh_ref[...]).astype(h_ref.dtype)


def _resid_final_norm_kernel(x_ref, d_ref, gt_ref, g_ref, o_ref):
    x = x_ref[...] + gt_ref[...] * d_ref[...]
    o_ref[...] = _rms(x, g_ref[...])


def _mod_spec(d, part, dims, tm):
    tpb, lt, nb = dims["T"] // tm, dims["L"] // tm, dims["B"]
    return pl.BlockSpec((None, 1, d), lambda i: (_mod_row(i, tpb, lt, part, nb), 0, 0))


def _norm_mod(x, g, mods, part_sh, part_sc, dims, tm=256):
    m, d = x.shape
    row = pl.BlockSpec((tm, d), lambda i: (i, 0))
    return pl.pallas_call(
        _norm_mod_kernel, grid=(m // tm,),
        in_specs=[row, pl.BlockSpec((1, d), lambda i: (0, 0)),
                  _mod_spec(d, part_sh, dims, tm), _mod_spec(d, part_sc, dims, tm)],
        out_specs=row, out_shape=jax.ShapeDtypeStruct((m, d), BF16),
        compiler_params=_cparams(("parallel",)),
    )(x, g, mods, mods)


def _resid_norm_mod(x, delta, mods_gt, part_gt, g, mods, part_sh, part_sc, dims, tm=256):
    m, d = x.shape
    row = pl.BlockSpec((tm, d), lambda i: (i, 0))
    return pl.pallas_call(
        _resid_norm_mod_kernel, grid=(m // tm,),
        in_specs=[row, row, _mod_spec(d, part_gt, dims, tm), pl.BlockSpec((1, d), lambda i: (0, 0)),
                  _mod_spec(d, part_sh, dims, tm), _mod_spec(d, part_sc, dims, tm)],
        out_specs=[row, row],
        out_shape=[jax.ShapeDtypeStruct((m, d), F32), jax.ShapeDtypeStruct((m, d), BF16)],
        compiler_params=_cparams(("parallel",)),
    )(x, delta, mods_gt, g, mods, mods)


def _resid_final_norm(x3, delta3, mods, part_gt, g, dims, tm=256):
    b, t, d = x3.shape
    lt = dims["L"] // tm
    blk = pl.BlockSpec((None, tm, d), lambda bi, i: (bi, i, 0))
    return pl.pallas_call(
        _resid_final_norm_kernel, grid=(b, lt),
        in_specs=[blk, blk, pl.BlockSpec((None, 1, d), lambda bi, i: (bi * 6 + part_gt, 0, 0)),
                  pl.BlockSpec((1, d), lambda bi, i: (0, 0))],
        out_specs=blk, out_shape=jax.ShapeDtypeStruct((b, dims["L"], d), F32),
        compiler_params=_cparams(("parallel", "parallel")),
    )(x3, delta3, mods, g)


def _segsum(x, bd):
    hi = x.astype(BF16)
    lo = (x - hi.astype(F32)).astype(BF16)
    return (jnp.dot(hi, bd, preferred_element_type=F32) + jnp.dot(lo, bd, preferred_element_type=F32))


def _rwfeat_kernel(p_ref, pp_ref, pn_ref, mu_ref, w0_ref, wup_ref, a0_ref, aup_ref, gup_ref, kk_ref, ka_ref,
                   rk_ref, bd_ref,
                   r_ref, nkk_ref, v_ref, dec0_ref, dec1_ref, b0_ref, b1_ref, ke0_ref, ke1_ref,
                   bon0_ref, bon1_ref, g0_ref, g1_ref, *, tm, lat_tiles, all_tiles):
    i = pl.program_id(1)
    p = p_ref[...]
    prev_ok = jnp.logical_and(i != 0, i != lat_tiles)
    next_ok = jnp.logical_and(i != lat_tiles - 1, i != all_tiles - 1)
    prow = jnp.where(prev_ok, pp_ref[SUBLANES - 1:SUBLANES, :], 0.0)
    nrow = jnp.where(next_ok, pn_ref[0:1, :], 0.0)
    rid = lax.broadcasted_iota(I32, (tm, 1), 0)
    prev = jnp.where(rid == 0, prow, pltpu.roll(p, 1, 0))
    nxt = jnp.where(rid == tm - 1, nrow, pltpu.roll(p, tm - 1, 0))
    ps = p + (0.5 * (prev + nxt) - p) * mu_ref[...]

    w_ = RW_WIDTH
    r = ps[:, 0:w_]
    k = ps[:, w_:2 * w_]
    v = ps[:, 2 * w_:3 * w_]
    wd = ps[:, 3 * w_:3 * w_ + LANES]
    ad = ps[:, 3 * w_ + LANES:3 * w_ + 2 * LANES]
    gd = ps[:, 3 * w_ + 2 * LANES:3 * w_ + 3 * LANES]
    bd = bd_ref[...]

    lw = jnp.dot(jnp.tanh(wd).astype(BF16), wup_ref[...], preferred_element_type=F32)
    la = jnp.dot(ad.astype(BF16), aup_ref[...], preferred_element_type=F32)
    lg = jnp.dot(_sigmoid(gd).astype(BF16), gup_ref[...], preferred_element_type=F32)

    kk = k * kk_ref[...]
    kk = kk * lax.rsqrt(_segsum(kk * kk, bd) + 1e-12)
    r_ref[...] = r
    nkk_ref[...] = -kk
    v_ref[...] = v
    rk = rk_ref[...]
    ka = ka_ref[...]
    outs = ((dec0_ref, b0_ref, ke0_ref, bon0_ref, g0_ref), (dec1_ref, b1_ref, ke1_ref, bon1_ref, g1_ref))
    for d in range(2):
        dec_ref, b_ref, ke_ref, bon_ref, g_ref = outs[d]
        z = w0_ref[d:d + 1, :] + lw[:, d * w_:(d + 1) * w_]
        nz = -z
        softplus = jnp.maximum(nz, 0.0) + jnp.log(1.0 + jnp.exp(-jnp.abs(nz)))
        wlog = -softplus - 0.5
        dec_ref[...] = jnp.exp(-jnp.exp(wlog))
        a = _sigmoid(a0_ref[d:d + 1, :] + la[:, d * w_:(d + 1) * w_])
        b_ref[...] = kk * a
        ke = k * (1.0 + (a - 1.0) * ka)
        ke_ref[...] = ke
        bon_ref[...] = _segsum(r * ke * rk, bd) * v
        g_ref[...] = lg[:, d * w_:(d + 1) * w_]


def _rwfeat(p3, mu, w0, wup, a0, aup, gup, k_k, k_a, r_k, bd, dims, tm=128):
    b, t, wc = p3.shape
    lt, at = dims["L"] // tm, t // tm
    hb = tm // SUBLANES
    nblk8 = t // SUBLANES
    w_ = RW_WIDTH
    full = lambda arr: pl.BlockSpec(arr.shape, lambda bi, i: (0,) * arr.ndim)
    out_blk = pl.BlockSpec((None, tm, w_), lambda bi, i: (bi, i, 0))
    n_out = 13
    return pl.pallas_call(
        functools.partial(_rwfeat_kernel, tm=tm, lat_tiles=lt, all_tiles=at),
        grid=(b, at),
        in_specs=[pl.BlockSpec((None, tm, wc), lambda bi, i: (bi, i, 0)),
                  pl.BlockSpec((None, SUBLANES, wc), lambda bi, i: (bi, jnp.maximum(i * hb - 1, 0), 0)),
                  pl.BlockSpec((None, SUBLANES, wc), lambda bi, i: (bi, jnp.minimum((i + 1) * hb, nblk8 - 1), 0)),
                  full(mu), full(w0), full(wup), full(a0), full(aup), full(gup), full(k_k), full(k_a),
                  full(r_k), full(bd)],
        out_specs=[out_blk] * n_out,
        out_shape=[jax.ShapeDtypeStruct((b, t, w_), F32)] * n_out,
        compiler_params=_cparams(("parallel", "parallel")),
    )(p3, p3, p3, mu, w0, wup, a0, aup, gup, k_k, k_a, r_k, bd)


SCAN_KH = RW_HEAD // 2


SCAN_NK = 5


def _scan_kernel(*refs, tt):
    kf_refs, kr_refs = refs[:SCAN_NK], refs[SCAN_NK:2 * SCAN_NK]
    vf_ref, vr_ref, yf_ref, yr_ref, s_ref, sa_ref, mrg_ref, acc_ref, yraw_ref = refs[2 * SCAN_NK:]

    @pl.when(pl.program_id(0) == 0)
    def _():
        s_ref[...] = jnp.zeros_like(s_ref)
        sa_ref[...] = jnp.zeros_like(sa_ref)

    lane = lax.broadcasted_iota(I32, (RW_HEAD, LANES), 1)
    lanek = lax.broadcasted_iota(I32, (SCAN_KH, LANES), 1)
    quarter = LANES // 4
    fwdk = (lanek // quarter) % 2 == 0
    laneh = lax.broadcasted_iota(I32, (RW_HEAD // 2, LANES), 1)
    fwdh = (laneh // quarter) % 2 == 0

    half = RW_HEAD // 2

    def one_step(t):
        tr = tt - 1 - t
        for q in range(SCAN_NK):
            mrg_ref[q] = jnp.where(fwdk, kf_refs[q][t], kr_refs[q][tr])

        def half_pass(hv, carry):
            sl = pl.ds(pl.multiple_of(hv * half, half), half)
            vt = jnp.where(fwdh, vf_ref[t, sl, :], vr_ref[tr, sl, :])
            sah = sa_ref[sl, :]
            yacc = jnp.zeros((half, LANES), F32)
            sn = jnp.zeros((half, LANES), F32)
            for k in range(SCAN_KH):
                row = lambda q: mrg_ref[q, pl.ds(k, 1), :]
                w, b, ke, r, an = row(0), row(1), row(2), row(3), row(4)
                s = s_ref[k, sl, :] * w + (sah * b + vt * ke)
                s_ref[k, sl, :] = s
                yacc = yacc + s * r
                sn = sn + s * an
            acc_ref[0, sl, :] = yacc
            acc_ref[1, sl, :] = sn
            return carry

        lax.fori_loop(0, 2, half_pass, 0)
        sn = acc_ref[1]
        sa_ref[...] = sn + pltpu.roll(sn, LANES // 2, 1)
        yraw_ref[t] = acc_ref[0]

    def step(t, carry):
        one_step(t)
        return carry

    lax.fori_loop(0, tt, step, 0)

    def normed(t):
        yacc = yraw_ref[t]
        y = yacc + pltpu.roll(yacc, LANES // 2, 1)
        mean = jnp.mean(y, axis=0, keepdims=True)
        dlt = y - mean
        var = jnp.mean(dlt * dlt, axis=0, keepdims=True)
        return dlt * lax.rsqrt(var + RW_GN_EPS)

    low = lane < LANES // 2
    for j in range(tt // 2):
        y0, y1 = normed(2 * j), normed(2 * j + 1)
        yf_ref[j] = jnp.where(low, y0, y1)
        yr_ref[tt // 2 - 1 - j] = jnp.where(low, y1, y0)


def _scan(ks, vv, l, tt=32):
    t = vv.shape[0]
    assert len(ks) == SCAN_NK and t % tt == 0 and tt % 2 == 0 and l % tt == 0
    nb, lb = t // tt, l // tt
    cb = nb - lb
    fwd = lambda i: jnp.where(i < cb, lb + i, i - cb)
    rev = lambda i: jnp.where(i < cb, lb + (cb - 1 - i), lb - 1 - (i - cb))
    kv_spec = lambda f: pl.BlockSpec((tt, SCAN_KH, LANES), lambda i: (f(i), 0, 0))
    v_spec = lambda f: pl.BlockSpec((tt, RW_HEAD, LANES), lambda i: (f(i), 0, 0))
    y_spec = lambda f: pl.BlockSpec((tt // 2, RW_HEAD, LANES), lambda i: (f(i), 0, 0))
    y_shape = jax.ShapeDtypeStruct((t // 2, RW_HEAD, LANES), F32)
    return pl.pallas_call(
        functools.partial(_scan_kernel, tt=tt),
        grid=(nb,),
        in_specs=[kv_spec(fwd)] * SCAN_NK + [kv_spec(rev)] * SCAN_NK + [v_spec(fwd), v_spec(rev)],
        out_specs=[y_spec(fwd), y_spec(rev)],
        out_shape=[y_shape, y_shape],
        scratch_shapes=[pltpu.VMEM((SCAN_KH, RW_HEAD, LANES), F32), pltpu.VMEM((RW_HEAD, LANES), F32),
                        pltpu.VMEM((SCAN_NK, SCAN_KH, LANES), F32), pltpu.VMEM((2, RW_HEAD, LANES), F32),
                        pltpu.VMEM((tt, RW_HEAD, LANES), F32)],
        compiler_params=_cparams(("arbitrary",)),
    )(*ks, *ks, vv, vv)


def _to_scan_k(x0, x1):
    b, t, _ = x0.shape
    s = jnp.stack([x0, x1], 0).reshape(2, b, t, RW_HEADS, 2, SCAN_KH)
    return s.transpose(2, 5, 4, 0, 1, 3).reshape(t, SCAN_KH, 4 * b * RW_HEADS)


def _to_scan_v(x):
    b, t, _ = x.shape
    s = x.reshape(b, t, RW_HEADS, RW_HEAD).transpose(1, 3, 0, 2).reshape(t, RW_HEAD, b * RW_HEADS)
    return jnp.concatenate([s, s, s, s], -1)


def _next_step_rows(x, l):
    lat, ctx = x[:, :l], x[:, l:]
    z = jnp.zeros_like(x[:, :1])
    nxt_f = jnp.concatenate([lat[:, 1:], z, ctx[:, 1:], lat[:, :1]], 1)
    nxt_r = jnp.concatenate([z, lat[:, :-1], lat[:, l - 1:l], ctx[:, :-1]], 1)
    return nxt_f, nxt_r


def _from_scan_y(yf, yr, b):
    t2 = yf.shape[0]
    q = b * RW_HEADS
    outs = []
    for d, y in enumerate((yf, yr)):
        s = y.reshape(t2, RW_HEAD, 2, 2 * q)[..., d * q:(d + 1) * q]
        outs.append(s.reshape(t2, RW_HEAD, 2, b, RW_HEADS).transpose(3, 0, 2, 4, 1).reshape(b, 2 * t2, RW_WIDTH))
    return outs[0], outs[1]


def _readout_kernel(y0_ref, y1_ref, bon0_ref, bon1_ref, g0_ref, g1_ref, lw_ref, lb_ref, o_ref):
    lw, lb = lw_ref[...], lb_ref[...]
    o = (y0_ref[...] * lw + lb + bon0_ref[...]) * g0_ref[...]
    o = o + (y1_ref[...] * lw + lb + bon1_ref[...]) * g1_ref[...]
    o_ref[...] = o.astype(o_ref.dtype)


def _readout(y0, y1, bon0, bon1, g0, g1, ln_w, ln_b, tm=256):
    m, w_ = y0.shape
    row = pl.BlockSpec((tm, w_), lambda i: (i, 0))
    vec = pl.BlockSpec((1, w_), lambda i: (0, 0))
    return pl.pallas_call(
        _readout_kernel, grid=(m // tm,), in_specs=[row] * 6 + [vec, vec], out_specs=row,
        out_shape=jax.ShapeDtypeStruct((m, w_), BF16), compiler_params=_cparams(("parallel",)),
    )(y0, y1, bon0, bon1, g0, g1, ln_w, ln_b)


def _mla_prep_kernel(p_ref, c_ref, s_ref, qn_ref, kn_ref, wqa_ref, wqs_ref, wk_ref, wv_ref, q_ref, k_ref, v_ref):
    p = p_ref[...]
    cos, sin = c_ref[...], s_ref[...]
    qc = _rms(p[:, :MLA_Q_RANK], qn_ref[...]).astype(BF16)
    kvc = _rms(p[:, MLA_Q_RANK:MLA_Q_RANK + MLA_KV_RANK], kn_ref[...]).astype(BF16)
    kr = p[:, MLA_Q_RANK + MLA_KV_RANK:MLA_Q_RANK + MLA_KV_RANK + LANES]
    krs = p[:, MLA_Q_RANK + MLA_KV_RANK + LANES:MLA_Q_RANK + MLA_KV_RANK + 2 * LANES]
    krope = kr * cos + krs * sin
    qa = jnp.dot(qc, wqa_ref[...], preferred_element_type=F32)
    qs = jnp.dot(qc, wqs_ref[...], preferred_element_type=F32)
    kk = jnp.dot(kvc, wk_ref[...], preferred_element_type=F32)
    v_ref[...] = jnp.dot(kvc, wv_ref[...], preferred_element_type=F32).astype(v_ref.dtype)
    scale = MLA_QK ** -0.5
    for h in range(MLA_HEADS):
        sl = slice(h * LANES, (h + 1) * LANES)
        q_ref[:, sl] = ((qa[:, sl] * cos + qs[:, sl] * sin) * scale).astype(q_ref.dtype)
        k_ref[:, sl] = (kk[:, sl] + krope).astype(k_ref.dtype)


def _mla_prep(p, cos_t, sin_t, q_norm, kv_norm, wqa, wqs, wk, wv, dims, tm=256):
    m, pc = p.shape
    tpb = dims["T"] // tm
    full = lambda arr: pl.BlockSpec(arr.shape, lambda i: (0,) * arr.ndim)
    tab = pl.BlockSpec((tm, LANES), lambda i: (i % tpb, 0))
    hw = MLA_HEADS * LANES
    return pl.pallas_call(
        _mla_prep_kernel, grid=(m // tm,),
        in_specs=[pl.BlockSpec((tm, pc), lambda i: (i, 0)), tab, tab, full(q_norm), full(kv_norm),
                  full(wqa), full(wqs), full(wk), full(wv)],
        out_specs=[pl.BlockSpec((tm, hw), lambda i: (i, 0)), pl.BlockSpec((tm, hw), lambda i: (i, 0)),
                   pl.BlockSpec((tm, MLA_HEADS * MLA_V), lambda i: (i, 0))],
        out_shape=[jax.ShapeDtypeStruct((m, hw), BF16), jax.ShapeDtypeStruct((m, hw), BF16),
                   jax.ShapeDtypeStruct((m, MLA_HEADS * MLA_V), BF16)],
        compiler_params=_cparams(("parallel",)),
    )(p, cos_t, sin_t, q_norm, kv_norm, wqa, wqs, wk, wv)


def _flash_kernel(q_ref, k_ref, v_ref, *rest, nk):
    o_ref = rest[-7]
    state = (rest[-6:-3], rest[-3:])
    ki = pl.program_id(3)

    @pl.when(ki == 0)
    def _():
        for m_ref, l_ref, acc_ref in state:
            m_ref[...] = jnp.full_like(m_ref, -jnp.inf)
            l_ref[...] = jnp.zeros_like(l_ref)
            acc_ref[...] = jnp.zeros_like(acc_ref)

    v = v_ref[...]
    for hh, (m_ref, l_ref, acc_ref) in enumerate(state):
        q = q_ref[:, hh * LANES:(hh + 1) * LANES]
        k = k_ref[:, hh * LANES:(hh + 1) * LANES]
        s = lax.dot_general(q, k, (((1,), (1,)), ((), ())), preferred_element_type=F32)
        m_prev = m_ref[...]
        m_new = jnp.maximum(m_prev, jnp.max(s, -1, keepdims=True))
        alpha = jnp.exp(m_prev - m_new)
        pr = jnp.exp(s - m_new)
        l_ref[...] = alpha * l_ref[...] + jnp.sum(pr, -1, keepdims=True)
        acc_ref[...] = alpha * acc_ref[...] + jnp.dot(pr.astype(BF16), v, preferred_element_type=F32)
        m_ref[...] = m_new

    @pl.when(ki == nk - 1)
    def _():
        (_, l0, acc0), (_, l1, acc1) = state
        lane = lax.broadcasted_iota(I32, acc0.shape, 1)
        o_ref[...] = jnp.where(lane < MLA_V, acc0[...] / l0[...], acc1[...] / l1[...]).astype(o_ref.dtype)


def _flash(q3, k3, v3, *, tq, tk, q_off, nq, k_off, nk, prev_out=None):
    b, t, _ = q3.shape
    hp = MLA_HEADS // 2
    in_specs = [pl.BlockSpec((None, tq, 2 * LANES), lambda bi, h, qi, ki: (bi, q_off + qi, h)),
                pl.BlockSpec((None, tk, 2 * LANES), lambda bi, h, qi, ki: (bi, k_off + ki, h)),
                pl.BlockSpec((None, tk, LANES), lambda bi, h, qi, ki: (bi, k_off + ki, h))]
    args = [q3, k3, v3]
    aliases = {}
    if prev_out is not None:
        in_specs.append(pl.BlockSpec(memory_space=pl.ANY))
        args.append(prev_out)
        aliases = {3: 0}
    return pl.pallas_call(
        functools.partial(_flash_kernel, nk=nk),
        grid=(b, hp, nq, nk),
        in_specs=in_specs,
        out_specs=pl.BlockSpec((None, tq, LANES), lambda bi, h, qi, ki: (bi, q_off + qi, h)),
        out_shape=jax.ShapeDtypeStruct((b, t, MLA_HEADS * MLA_V), BF16),
        scratch_shapes=[pltpu.VMEM((tq, 1), F32), pltpu.VMEM((tq, 1), F32), pltpu.VMEM((tq, LANES), F32)] * 2,
        input_output_aliases=aliases,
        compiler_params=_cparams(("parallel", "parallel", "parallel", "arbitrary")),
    )(*args)


def _topk_rows(s, kk, payload=None):
    nrow = s.shape[0]
    rid = lax.broadcasted_iota(I32, s.shape, 0)
    vals, sel = [], []
    for _ in range(kk):
        m = jnp.max(s, axis=0, keepdims=True)
        pos = jnp.min(jnp.where(s == m, rid, nrow), axis=0, keepdims=True)
        hit = rid == pos
        vals.append(m)
        sel.append(pos if payload is None else jnp.max(jnp.where(hit, payload, -1), axis=0, keepdims=True))
        s = jnp.where(hit, -jnp.inf, s)
    return jnp.concatenate(vals, 0), jnp.concatenate(sel, 0)


def _peer_topk_kernel(q0_ref, q1_ref, keys_ref, idx_ref, gate_ref):
    nt = (((1,), (1,)), ((), ()))
    s0 = lax.dot_general(keys_ref[0].astype(BF16), q0_ref[...].astype(BF16), nt, preferred_element_type=F32)
    s1 = lax.dot_general(keys_ref[1].astype(BF16), q1_ref[...].astype(BF16), nt, preferred_element_type=F32)
    v1, i1 = _topk_rows(s0, PEER_TOPK)
    v2, i2 = _topk_rows(s1, PEER_TOPK)
    cand_s = jnp.concatenate([v1[a:a + 1, :] + v2 for a in range(PEER_TOPK)], 0)
    cand_i = jnp.concatenate([i1[a:a + 1, :] * PEER_NKEYS + i2 for a in range(PEER_TOPK)], 0)
    top_s, top_i = _topk_rows(cand_s, PEER_TOPK, payload=cand_i)
    e = jnp.exp(top_s - top_s[0:1, :])
    idx_ref[...] = top_i
    gate_ref[...] = e / jnp.sum(e, axis=0, keepdims=True)


def _peer_topk(q, keys, tm=256):
    m = q.shape[0]
    out_blk = pl.BlockSpec((None, PEER_TOPK, tm), lambda i, h: (h, 0, i))
    return pl.pallas_call(
        _peer_topk_kernel, grid=(m // tm, PEER_HEADS),
        in_specs=[pl.BlockSpec((tm, PEER_HALF), lambda i, h: (i, 2 * h)),
                  pl.BlockSpec((tm, PEER_HALF), lambda i, h: (i, 2 * h + 1)),
                  pl.BlockSpec((None, 2, PEER_NKEYS, PEER_HALF), lambda i, h: (h, 0, 0, 0))],
        out_specs=[out_blk, out_blk],
        out_shape=[jax.ShapeDtypeStruct((PEER_HEADS, PEER_TOPK, m), I32),
                   jax.ShapeDtypeStruct((PEER_HEADS, PEER_TOPK, m), F32)],
        compiler_params=_cparams(("parallel", "parallel")),
    )(q, q, keys)


def _gelu(x):
    return 0.5 * x * (1.0 + lax.erf(x * (2.0 ** -0.5)))


def _peer_gates_kernel(idx_ref, gate_ref, g_ref, *, tm, unroll):
    rid = lax.broadcasted_iota(I32, (PEER_NKEYS, PEER_SEL), 0)
    nt = (((1,), (1,)), ((), ()))

    def body(t, carry):
        for u in range(unroll):
            n = t * unroll + u
            idx = idx_ref[pl.ds(n, 1), :]
            gate = gate_ref[pl.ds(n, 1), :]
            i1 = idx // PEER_NKEYS
            i2 = idx - i1 * PEER_NKEYS
            a = jnp.where(rid == i1, gate, 0.0).astype(BF16)
            b = jnp.where(rid == i2, 1.0, 0.0).astype(BF16)
            g_ref[n] = lax.dot_general(a, b, nt, preferred_element_type=F32).astype(g_ref.dtype)
        return carry

    lax.fori_loop(0, tm // unroll, body, 0)


def _peer_gates(idx, gate, tm=128, unroll=4):
    m = idx.shape[0]
    sel = pl.BlockSpec((tm, PEER_SEL), lambda i: (i, 0))
    return pl.pallas_call(
        functools.partial(_peer_gates_kernel, tm=tm, unroll=unroll), grid=(m // tm,),
        in_specs=[sel, sel],
        out_specs=pl.BlockSpec((tm, PEER_NKEYS, PEER_NKEYS), lambda i: (i, 0, 0)),
        out_shape=jax.ShapeDtypeStruct((m, PEER_NKEYS, PEER_NKEYS), BF16),
        compiler_params=_cparams(("parallel",)),
    )(idx, gate)


def _peer_dense_kernel(h_ref, u_ref, v_ref, g_ref, oin_ref, o_ref):
    act = lax.dot_general(h_ref[...], u_ref[...], (((1,), (1,)), ((), ())), preferred_element_type=F32)
    g = g_ref[...].astype(F32).reshape(act.shape)
    wt = jnp.where(g != 0.0, _gelu(act) * g, 0.0).astype(BF16)
    o_ref[...] = oin_ref[...] + jnp.dot(wt, v_ref[...], preferred_element_type=F32)


def _peer_dense(h2, u_tab, v_tab, g, *, tm, te):
    m, d = h2.shape
    e = v_tab.shape[0]
    assert m % tm == 0 and e % te == 0
    tok = pl.BlockSpec((tm, d), lambda j, i: (i, 0))
    tab = pl.BlockSpec((te, d), lambda j, i: (j, 0), pipeline_mode=pl.Buffered(1))
    return pl.pallas_call(
        _peer_dense_kernel, grid=(e // te, m // tm),
        in_specs=[tok, tab, tab, pl.BlockSpec((tm, te // PEER_NKEYS, PEER_NKEYS), lambda j, i: (i, j, 0)), tok],
        out_specs=tok,
        out_shape=jax.ShapeDtypeStruct((m, d), F32),
        input_output_aliases={4: 0},
        compiler_params=_cparams(("arbitrary", "arbitrary")),
    )(h2, u_tab, v_tab, g, jnp.zeros((m, d), F32))


def _block_diag2(w):
    z = jnp.zeros_like(w[0])
    return jnp.concatenate([jnp.concatenate([w[0], z], 1), jnp.concatenate([z, w[1]], 1)], 0)


def _rope_swap_cols(w_rope):
    half = MLA_ROPE // 2
    return jnp.concatenate([-w_rope[..., half:], w_rope[..., :half]], -1)


def _prep_layer_weights(l, w_in, mla_w_uq, mla_w_ukv, rk_w_up, rk_a_up, rk_g_up):
    d = w_in.shape[1]
    wi = w_in[l]
    lo, hi = RW_COLS, RW_COLS + MLA_Q_RANK + MLA_KV_RANK + MLA_ROPE
    w_rw = jnp.pad(wi[:, :lo], ((0, 0), (0, RW_COLS_PAD - RW_COLS))).astype(BF16)
    w_kr = wi[:, hi - MLA_ROPE:hi]
    place = lambda w: jnp.pad(w, ((0, 0), (MLA_NOPE, LANES - MLA_NOPE - MLA_ROPE)))
    w_mla = jnp.concatenate([wi[:, lo:hi - MLA_ROPE], place(w_kr), place(_rope_swap_cols(w_kr))], 1).astype(BF16)
    w_gate = wi[:, hi:].astype(BF16)

    uq = mla_w_uq[l].reshape(MLA_Q_RANK, MLA_HEADS, MLA_QK)
    zeros = jnp.zeros((MLA_Q_RANK, MLA_HEADS, LANES - MLA_QK), F32)
    wqa = jnp.concatenate([uq, zeros], -1).reshape(MLA_Q_RANK, -1).astype(BF16)
    zn = jnp.zeros((MLA_Q_RANK, MLA_HEADS, MLA_NOPE), F32)
    wqs = jnp.concatenate([zn, _rope_swap_cols(uq[..., MLA_NOPE:]), zeros], -1).reshape(MLA_Q_RANK, -1).astype(BF16)
    ukv = mla_w_ukv[l].reshape(MLA_KV_RANK, MLA_HEADS, MLA_NOPE + MLA_V)
    wk = jnp.pad(ukv[..., :MLA_NOPE], ((0, 0), (0, 0), (0, LANES - MLA_NOPE))).reshape(MLA_KV_RANK, -1).astype(BF16)
    wv = ukv[..., MLA_NOPE:].reshape(MLA_KV_RANK, -1).astype(BF16)

    wup = _block_diag2(rk_w_up[l]).astype(BF16)
    aup = _block_diag2(rk_a_up[l]).astype(BF16)
    gup = jnp.pad(_block_diag2(rk_g_up[l]), ((0, LANES - 2 * RW_GATE_LORA), (0, 0))).astype(BF16)
    return dict(w_rw=w_rw, w_mla=w_mla, w_gate=w_gate, wqa=wqa, wqs=wqs, wk=wk, wv=wv, wup=wup, aup=aup, gup=gup)


def _rope_lane_tables(l, c):
    rows = l // GRID_W
    row = jnp.repeat(jnp.arange(rows, dtype=F32), GRID_W)
    col = jnp.tile(jnp.arange(GRID_W, dtype=F32), rows)
    n_freq = MLA_ROPE // 4
    freqs = ROPE_BASE ** (-jnp.arange(n_freq, dtype=F32) / n_freq)
    ang = jnp.concatenate([row[:, None] * freqs, col[:, None] * freqs], -1)
    cos, sin = jnp.cos(ang), jnp.sin(ang)
    tail = LANES - MLA_NOPE - MLA_ROPE
    cos_t = jnp.concatenate([jnp.ones((l, MLA_NOPE), F32), cos, cos, jnp.ones((l, tail), F32)], -1)
    sin_t = jnp.concatenate([jnp.zeros((l, MLA_NOPE), F32), sin, sin, jnp.zeros((l, tail), F32)], -1)
    cos_t = jnp.concatenate([cos_t, jnp.ones((c, LANES), F32)], 0)
    sin_t = jnp.concatenate([sin_t, jnp.zeros((c, LANES), F32)], 0)
    return cos_t, sin_t


def _peer(h2, w_q, keys, u_tab, v_tab, *, te, tm=256):
    m, d = h2.shape
    q = _mm(h2, w_q, tm=256, tn=_pick_tile(w_q.shape[1], 1024))
    idx_t, gate_t = _peer_topk(q, keys)
    idx = idx_t.transpose(2, 0, 1).reshape(m, PEER_SEL)
    gate = gate_t.transpose(2, 0, 1).reshape(m, PEER_SEL)
    g = _peer_gates(idx, gate)
    return _peer_dense(h2, u_tab, v_tab, g, tm=tm, te=te)


def _forward(x, c, ctx, c_ctx, ada_w, ada_b, norm_mix_g, w_in, gate_b, rk_mu, rk_w0, rk_w_up, rk_a0,
             rk_a_up, rk_g_up, rk_k_k, rk_k_a, rk_r_k, rk_ln_w, rk_ln_b, mla_q_norm, mla_w_uq, mla_kv_norm,
             mla_w_ukv, w_branch_a, w_branch_b, w_out, norm_ffn_g, peer_w_q, peer_keys, peer_u, peer_v,
             final_norm_g, *, flash_tq=512, flash_tk=2816, scan_tt=32, peer_te=2048):
    nb, l, d = x.shape
    cl = ctx.shape[1]
    t = l + cl
    m = nb * t
    depth = w_in.shape[0]
    dims = dict(B=nb, L=l, C=cl, T=t)
    assert nb == 2 and 4 * nb * RW_HEADS == LANES and l % 256 == 0 and cl % 256 == 0

    xa = jnp.concatenate([x, ctx], 1).reshape(m, d)
    cos_t, sin_t = _rope_lane_tables(l, cl)
    seg = jnp.arange(RW_WIDTH, dtype=I32) // RW_HEAD
    bd = (seg[:, None] == seg[None, :]).astype(BF16)
    cvec = jnp.zeros((SUBLANES, d), F32).at[:nb].set(jax.nn.silu(c)).at[nb].set(jax.nn.silu(c_ctx))

    delta, mods_prev = None, None
    out = None
    for li in range(depth):
        wl = _prep_layer_weights(li, w_in, mla_w_uq, mla_w_ukv, rk_w_up, rk_a_up, rk_g_up)
        tn_ada = _pick_tile(6 * d, 768)
        mods = _mm(cvec, ada_w[li], tm=SUBLANES, tn=tn_ada, epilogue=lambda acc, bias: acc + bias,
                   extras=[(ada_b[li][None, :], pl.BlockSpec((1, tn_ada), lambda j, i: (0, j)))])
        mods = mods[:nb + 1].reshape((nb + 1) * 6, 1, d)
        g_mix = norm_mix_g[li][None, :]
        if li == 0:
            h = _norm_mod(xa, g_mix, mods, 0, 1, dims)
        else:
            xa, h = _resid_norm_mod(xa, delta, mods_prev, 5, g_mix, mods, 0, 1, dims)

        p_rw = _mm(h, wl["w_rw"], tm=256, tn=_pick_tile(RW_COLS_PAD, 1152))
        p_mla = _mm(h, wl["w_mla"], tm=256, tn=MLA_P_COLS)
        p_gate = _mm(h, wl["w_gate"], tm=256, tn=1024)

        mu = jnp.pad(rk_mu[li], (0, RW_COLS_PAD - RW_COLS))[None, :]
        feats = _rwfeat(p_rw.reshape(nb, t, RW_COLS_PAD), mu, rk_w0[li], wl["wup"], rk_a0[li], wl["aup"], wl["gup"],
                        rk_k_k[li][None, :], rk_k_a[li][None, :], rk_r_k[li].reshape(1, RW_WIDTH), bd, dims)
        r, nkk, v, dec0, dec1, b0, b1, ke0, ke1, bon0, bon1, g0, g1 = feats
        ks = (_to_scan_k(dec0, dec1), _to_scan_k(b0, b1), _to_scan_k(ke0, ke1), _to_scan_k(r, r),
              _to_scan_k(*_next_step_rows(nkk, l)))
        ynf, ynr = _scan(ks, _to_scan_v(v), l, tt=scan_tt)
        yn0, yn1 = _from_scan_y(ynf, ynr, nb)
        flat = lambda z: z.reshape(m, RW_WIDTH)
        ya = _readout(flat(yn0), flat(yn1), flat(bon0), flat(bon1), flat(g0), flat(g1),
                      rk_ln_w[li][None, :], rk_ln_b[li][None, :])

        q, k, vv = _mla_prep(p_mla, cos_t, sin_t, mla_q_norm[li][None, :], mla_kv_norm[li][None, :],
                             wl["wqa"], wl["wqs"], wl["wk"], wl["wv"], dims)
        q3, k3, v3 = (z.reshape(nb, t, -1) for z in (q, k, vv))
        yb = _flash(q3, k3, v3, tq=flash_tq, tk=flash_tk, q_off=0, nq=l // flash_tq, k_off=0, nk=t // flash_tk)
        yb = _flash(q3, k3, v3, tq=cl, tk=cl, q_off=l // cl, nq=1, k_off=l // cl, nk=1, prev_out=yb)
        yb = yb.reshape(m, MLA_HEADS * MLA_V)

        tpb, lt = t // 256, l // 256
        br_a = _mm(ya, w_branch_a[li].astype(BF16), tm=256, tn=1024)
        gb = gate_b[li]
        merged = _mm(
            yb, w_branch_b[li].astype(BF16), tm=256, tn=1024, out_dtype=BF16,
            epilogue=lambda acc, ga, gbb, ba, bb, a: _sigmoid(ga + ba) * a + _sigmoid(gbb + bb) * acc,
            extras=[(p_gate, pl.BlockSpec((256, 1024), lambda j, i: (i, j))),
                    (p_gate, pl.BlockSpec((256, 1024), lambda j, i: (i, j + d // 1024))),
                    (gb[0][None, :], pl.BlockSpec((1, 1024), lambda j, i: (0, j))),
                    (gb[1][None, :], pl.BlockSpec((1, 1024), lambda j, i: (0, j))),
                    (br_a, pl.BlockSpec((256, 1024), lambda j, i: (i, j)))])
        x1 = _mm(merged, w_out[li].astype(BF16), tm=256, tn=1024,
                 epilogue=lambda acc, xr, gt: xr + gt * acc,
                 extras=[(xa, pl.BlockSpec((256, 1024), lambda j, i: (i, j))),
                         (mods, pl.BlockSpec((None, 1, 1024), lambda j, i: (_mod_row(i, tpb, lt, 2, nb), 0, j)))])

        h2 = _norm_mod(x1, norm_ffn_g[li][None, :], mods, 3, 4, dims)
        delta = _peer(h2, peer_w_q[li].astype(BF16), peer_keys[li], peer_u[li].astype(BF16),
                      peer_v[li].astype(BF16), te=peer_te)
        xa, mods_prev = x1, mods

    out = _resid_final_norm(xa.reshape(nb, t, d), delta.reshape(nb, t, d), mods_prev, 5,
                            final_norm_g[None, :], dims)
    return out


def kernel(x, c, ctx, c_ctx, ada_w, ada_b, norm_mix_g, w_in, gate_b, rk_mu, rk_w0, rk_w_up, rk_a0, rk_a_up, rk_g_up, rk_k_k, rk_k_a, rk_r_k, rk_ln_w, rk_ln_b, mla_q_norm, mla_w_uq, mla_kv_norm, mla_w_ukv, w_branch_a, w_branch_b, w_out, norm_ffn_g, peer_w_q, peer_keys, peer_u, peer_v, final_norm_g):
    return _forward(x, c, ctx, c_ctx, ada_w, ada_b, norm_mix_g, w_in, gate_b, rk_mu, rk_w0, rk_w_up, rk_a0,
                    rk_a_up, rk_g_up, rk_k_k, rk_k_a, rk_r_k, rk_ln_w, rk_ln_b, mla_q_norm, mla_w_uq,
                    mla_kv_norm, mla_w_ukv, w_branch_a, w_branch_b, w_out, norm_ffn_g, peer_w_q, peer_keys,
                    peer_u, peer_v, final_norm_g)
```

```python
import functools

import jax
import jax.numpy as jnp
from jax import lax
from jax.experimental import pallas as pl
from jax.experimental.pallas import tpu as pltpu

F32 = jnp.float32
BF16 = jnp.bfloat16
I32 = jnp.int32

NORM_EPS = 1e-6
GRID_W = 64
ROPE_BASE = 10000.0

RW_HEADS = 16
RW_HEAD = 64
RW_WIDTH = RW_HEADS * RW_HEAD
RW_DECAY_LORA = 64
RW_A_LORA = 64
RW_GATE_LORA = 32
RW_GN_EPS = 64e-5
RW_COLS = 3 * RW_WIDTH + 2 * (RW_DECAY_LORA + RW_A_LORA + RW_GATE_LORA)
RW_COLS_PAD = 3456

MLA_HEADS = 16
MLA_NOPE = 64
MLA_ROPE = 32
MLA_QK = MLA_NOPE + MLA_ROPE
MLA_V = 64
MLA_Q_RANK = 512
MLA_KV_RANK = 256
MLA_HEAD_PAD = 128
MLA_P_COLS = MLA_Q_RANK + MLA_KV_RANK + 2 * MLA_HEAD_PAD

PEER_HEADS = 8
PEER_NKEYS = 128
PEER_HALF = 128
PEER_TOPK = 16
PEER_SEL = PEER_HEADS * PEER_TOPK

LANES = 128
SUBLANES = 8
VMEM_LIMIT = 56 * 1024 * 1024


def _cparams(sem):
    return pltpu.CompilerParams(dimension_semantics=sem, vmem_limit_bytes=VMEM_LIMIT)


def _pick_tile(n, cap, mult=LANES):
    best = None
    t = mult
    while t <= min(n, cap):
        if n % t == 0:
            best = t
        t += mult
    assert best is not None, (n, cap)
    return best


def _mm_kernel(*refs, n_extra, epilogue):
    a_ref, w_ref = refs[0], refs[1]
    extra = refs[2:2 + n_extra]
    o_ref = refs[2 + n_extra]
    acc = jnp.dot(a_ref[...].astype(BF16), w_ref[...].astype(BF16), preferred_element_type=F32)
    if epilogue is not None:
        acc = epilogue(acc, *[e[...] for e in extra])
    o_ref[...] = acc.astype(o_ref.dtype)


def _mm(a, w, *, tm, tn, out_dtype=F32, epilogue=None, extras=()):
    m, k = a.shape
    k2, n = w.shape
    assert k == k2 and m % tm == 0 and n % tn == 0, (a.shape, w.shape, tm, tn)
    in_specs = [pl.BlockSpec((tm, k), lambda j, i: (i, 0)),
                pl.BlockSpec((k, tn), lambda j, i: (0, j))] + [s for _, s in extras]
    return pl.pallas_call(
        functools.partial(_mm_kernel, n_extra=len(extras), epilogue=epilogue),
        grid=(n // tn, m // tm),
        in_specs=in_specs,
        out_specs=pl.BlockSpec((tm, tn), lambda j, i: (i, j)),
        out_shape=jax.ShapeDtypeStruct((m, n), out_dtype),
        compiler_params=_cparams(("parallel", "parallel")),
    )(a, w, *[x for x, _ in extras])


def _sigmoid(x):
    return 1.0 / (1.0 + jnp.exp(-x))


def _mod_row(i, tiles_per_batch, lat_tiles, part, n_batch):
    which = jnp.where(i % tiles_per_batch >= lat_tiles, n_batch, i // tiles_per_batch)
    return which * 6 + part


def _rms(x, g):
    return x * lax.rsqrt(jnp.mean(x * x, -1, keepdims=True) + NORM_EPS) * g


def _norm_mod_kernel(x_ref, g_ref, sh_ref, sc_ref, h_ref):
    y = _rms(x_ref[...], g_ref[...])
    h_ref[...] = (y * (1.0 + sc_ref[...]) + sh_ref[...]).astype(h_ref.dtype)


def _resid_norm_mod_kernel(x_ref, d_ref, gt_ref, g_ref, sh_ref, sc_ref, xo_ref, h_ref):
    x = x_ref[...] + gt_ref[...] * d_ref[...]
    xo_ref[...] = x
    y = _rms(x, g_ref[...])
    h_ref[...] = (y * (1.0 + sc_ref[...]) + sh_ref[...]).astype(h_ref.dtype)


def _resid_final_norm_kernel(x_ref, d_ref, gt_ref, g_ref, o_ref):
    x = x_ref[...] + gt_ref[...] * d_ref[...]
    o_ref[...] = _rms(x, g_ref[...])


def _mod_spec(d, part, dims, tm):
    tpb, lt, nb = dims["T"] // tm, dims["L"] // tm, dims["B"]
    return pl.BlockSpec((None, 1, d), lambda i: (_mod_row(i, tpb, lt, part, nb), 0, 0))


def _norm_mod(x, g, mods, part_sh, part_sc, dims, tm=256):
    m, d = x.shape
    row = pl.BlockSpec((tm, d), lambda i: (i, 0))
    return pl.pallas_call(
        _norm_mod_kernel, grid=(m // tm,),
        in_specs=[row, pl.BlockSpec((1, d), lambda i: (0, 0)),
                  _mod_spec(d, part_sh, dims, tm), _mod_spec(d, part_sc, dims, tm)],
        out_specs=row, out_shape=jax.ShapeDtypeStruct((m, d), BF16),
        compiler_params=_cparams(("parallel",)),
    )(x, g, mods, mods)


def _resid_norm_mod(x, delta, mods_gt, part_gt, g, mods, part_sh, part_sc, dims, tm=256):
    m, d = x.shape
    row = pl.BlockSpec((tm, d), lambda i: (i, 0))
    return pl.pallas_call(
        _resid_norm_mod_kernel, grid=(m // tm,),
        in_specs=[row, row, _mod_spec(d, part_gt, dims, tm), pl.BlockSpec((1, d), lambda i: (0, 0)),
                  _mod_spec(d, part_sh, dims, tm), _mod_spec(d, part_sc, dims, tm)],
        out_specs=[row, row],
        out_shape=[jax.ShapeDtypeStruct((m, d), F32), jax.ShapeDtypeStruct((m, d), BF16)],
        compiler_params=_cparams(("parallel",)),
    )(x, delta, mods_gt, g, mods, mods)


def _resid_final_norm(x3, delta3, mods, part_gt, g, dims, tm=256):
    b, t, d = x3.shape
    lt = dims["L"] // tm
    blk = pl.BlockSpec((None, tm, d), lambda bi, i: (bi, i, 0))
    return pl.pallas_call(
        _resid_final_norm_kernel, grid=(b, lt),
        in_specs=[blk, blk, pl.BlockSpec((None, 1, d), lambda bi, i: (bi * 6 + part_gt, 0, 0)),
                  pl.BlockSpec((1, d), lambda bi, i: (0, 0))],
        out_specs=blk, out_shape=jax.ShapeDtypeStruct((b, dims["L"], d), F32),
        compiler_params=_cparams(("parallel", "parallel")),
    )(x3, delta3, mods, g)


def _segsum(x, bd):
    hi = x.astype(BF16)
    lo = (x - hi.astype(F32)).astype(BF16)
    return (jnp.dot(hi, bd, preferred_element_type=F32) + jnp.dot(lo, bd, preferred_element_type=F32))


def _rwfeat_kernel(p_ref, pp_ref, pn_ref, mu_ref, w0_ref, wup_ref, a0_ref, aup_ref, gup_ref, kk_ref, ka_ref,
                   rk_ref, bd_ref,
                   r_ref, nkk_ref, v_ref, dec0_ref, dec1_ref, b0_ref, b1_ref, ke0_ref, ke1_ref,
                   bon0_ref, bon1_ref, g0_ref, g1_ref, *, tm, lat_tiles, all_tiles):
    i = pl.program_id(1)
    p = p_ref[...]
    prev_ok = jnp.logical_and(i != 0, i != lat_tiles)
    next_ok = jnp.logical_and(i != lat_tiles - 1, i != all_tiles - 1)
    prow = jnp.where(prev_ok, pp_ref[SUBLANES - 1:SUBLANES, :], 0.0)
    nrow = jnp.where(next_ok, pn_ref[0:1, :], 0.0)
    rid = lax.broadcasted_iota(I32, (tm, 1), 0)
    prev = jnp.where(rid == 0, prow, pltpu.roll(p, 1, 0))
    nxt = jnp.where(rid == tm - 1, nrow, pltpu.roll(p, tm - 1, 0))
    ps = p + (0.5 * (prev + nxt) - p) * mu_ref[...]

    w_ = RW_WIDTH
    r = ps[:, 0:w_]
    k = ps[:, w_:2 * w_]
    v = ps[:, 2 * w_:3 * w_]
    wd = ps[:, 3 * w_:3 * w_ + LANES]
    ad = ps[:, 3 * w_ + LANES:3 * w_ + 2 * LANES]
    gd = ps[:, 3 * w_ + 2 * LANES:3 * w_ + 3 * LANES]
    bd = bd_ref[...]

    lw = jnp.dot(jnp.tanh(wd).astype(BF16), wup_ref[...], preferred_element_type=F32)
    la = jnp.dot(ad.astype(BF16), aup_ref[...], preferred_element_type=F32)
    lg = jnp.dot(_sigmoid(gd).astype(BF16), gup_ref[...], preferred_element_type=F32)

    kk = k * kk_ref[...]
    kk = kk * lax.rsqrt(_segsum(kk * kk, bd) + 1e-12)
    r_ref[...] = r
    nkk_ref[...] = -kk
    v_ref[...] = v
    rk = rk_ref[...]
    ka = ka_ref[...]
    outs = ((dec0_ref, b0_ref, ke0_ref, bon0_ref, g0_ref), (dec1_ref, b1_ref, ke1_ref, bon1_ref, g1_ref))
    for d in range(2):
        dec_ref, b_ref, ke_ref, bon_ref, g_ref = outs[d]
        z = w0_ref[d:d + 1, :] + lw[:, d * w_:(d + 1) * w_]
        nz = -z
        softplus = jnp.maximum(nz, 0.0) + jnp.log(1.0 + jnp.exp(-jnp.abs(nz)))
        wlog = -softplus - 0.5
        dec_ref[...] = jnp.exp(-jnp.exp(wlog))
        a = _sigmoid(a0_ref[d:d + 1, :] + la[:, d * w_:(d + 1) * w_])
        b_ref[...] = kk * a
        ke = k * (1.0 + (a - 1.0) * ka)
        ke_ref[...] = ke
        bon_ref[...] = _segsum(r * ke * rk, bd) * v
        g_ref[...] = lg[:, d * w_:(d + 1) * w_]


def _rwfeat(p3, mu, w0, wup, a0, aup, gup, k_k, k_a, r_k, bd, dims, tm=128):
    b, t, wc = p3.shape
    lt, at = dims["L"] // tm, t // tm
    hb = tm // SUBLANES
    nblk8 = t // SUBLANES
    w_ = RW_WIDTH
    full = lambda arr: pl.BlockSpec(arr.shape, lambda bi, i: (0,) * arr.ndim)
    out_blk = pl.BlockSpec((None, tm, w_), lambda bi, i: (bi, i, 0))
    n_out = 13
    return pl.pallas_call(
        functools.partial(_rwfeat_kernel, tm=tm, lat_tiles=lt, all_tiles=at),
        grid=(b, at),
        in_specs=[pl.BlockSpec((None, tm, wc), lambda bi, i: (bi, i, 0)),
                  pl.BlockSpec((None, SUBLANES, wc), lambda bi, i: (bi, jnp.maximum(i * hb - 1, 0), 0)),
                  pl.BlockSpec((None, SUBLANES, wc), lambda bi, i: (bi, jnp.minimum((i + 1) * hb, nblk8 - 1), 0)),
                  full(mu), full(w0), full(wup), full(a0), full(aup), full(gup), full(k_k), full(k_a),
                  full(r_k), full(bd)],
        out_specs=[out_blk] * n_out,
        out_shape=[jax.ShapeDtypeStruct((b, t, w_), F32)] * n_out,
        compiler_params=_cparams(("parallel", "parallel")),
    )(p3, p3, p3, mu, w0, wup, a0, aup, gup, k_k, k_a, r_k, bd)


SCAN_KH = RW_HEAD // 2


SCAN_NK = 5


def _scan_kernel(*refs, tt):
    kf_refs, kr_refs = refs[:SCAN_NK], refs[SCAN_NK:2 * SCAN_NK]
    vf_ref, vr_ref, yf_ref, yr_ref, s_ref, sa_ref, mrg_ref, acc_ref, yraw_ref = refs[2 * SCAN_NK:]

    @pl.when(pl.program_id(0) == 0)
    def _():
        s_ref[...] = jnp.zeros_like(s_ref)
        sa_ref[...] = jnp.zeros_like(sa_ref)

    lane = lax.broadcasted_iota(I32, (RW_HEAD, LANES), 1)
    lanek = lax.broadcasted_iota(I32, (SCAN_KH, LANES), 1)
    quarter = LANES // 4
    fwdk = (lanek // quarter) % 2 == 0
    laneh = lax.broadcasted_iota(I32, (RW_HEAD // 2, LANES), 1)
    fwdh = (laneh // quarter) % 2 == 0

    half = RW_HEAD // 2

    def merge_rows(t, slot):
        for q in range(SCAN_NK):
            mrg_ref[slot, q] = jnp.where(fwdk, kf_refs[q][t], kr_refs[q][tt - 1 - t])

    merge_rows(0, 0)

    def one_step(t):
        tr = tt - 1 - t
        slot = t % 2

        def half_pass(hv, carry):
            sl = pl.ds(pl.multiple_of(hv * half, half), half)
            vt = jnp.where(fwdh, vf_ref[t, sl, :], vr_ref[tr, sl, :])
            sah = sa_ref[sl, :]
            yacc = jnp.zeros((half, LANES), F32)
            sn = jnp.zeros((half, LANES), F32)
            for k in range(SCAN_KH):
                row = lambda q: mrg_ref[slot, q, pl.ds(k, 1), :]
                w, b, ke, r, an = row(0), row(1), row(2), row(3), row(4)
                s = s_ref[k, sl, :] * w + (sah * b + vt * ke)
                s_ref[k, sl, :] = s
                yacc = yacc + s * r
                sn = sn + s * an
            acc_ref[0, sl, :] = yacc
            acc_ref[1, sl, :] = sn
            return carry

        lax.fori_loop(0, 2, half_pass, 0)
        sn = acc_ref[1]
        sa_ref[...] = sn + pltpu.roll(sn, LANES // 2, 1)
        yraw_ref[t] = acc_ref[0]
        merge_rows(jnp.minimum(t + 1, tt - 1), 1 - slot)

    def step(t, carry):
        one_step(t)
        return carry

    lax.fori_loop(0, tt, step, 0)

    def normed(t):
        yacc = yraw_ref[t]
        y = yacc + pltpu.roll(yacc, LANES // 2, 1)
        mean = jnp.mean(y, axis=0, keepdims=True)
        dlt = y - mean
        var = jnp.mean(dlt * dlt, axis=0, keepdims=True)
        return dlt * lax.rsqrt(var + RW_GN_EPS)

    low = lane < LANES // 2
    for j in range(tt // 2):
        y0, y1 = normed(2 * j), normed(2 * j + 1)
        yf_ref[j] = jnp.where(low, y0, y1)
        yr_ref[tt // 2 - 1 - j] = jnp.where(low, y1, y0)


def _scan(ks, vv, l, tt=32):
    t = vv.shape[0]
    assert len(ks) == SCAN_NK and t % tt == 0 and tt % 2 == 0 and l % tt == 0
    nb, lb = t // tt, l // tt
    cb = nb - lb
    fwd = lambda i: jnp.where(i < cb, lb + i, i - cb)
    rev = lambda i: jnp.where(i < cb, lb + (cb - 1 - i), lb - 1 - (i - cb))
    kv_spec = lambda f: pl.BlockSpec((tt, SCAN_KH, LANES), lambda i: (f(i), 0, 0))
    v_spec = lambda f: pl.BlockSpec((tt, RW_HEAD, LANES), lambda i: (f(i), 0, 0))
    y_spec = lambda f: pl.BlockSpec((tt // 2, RW_HEAD, LANES), lambda i: (f(i), 0, 0))
    y_shape = jax.ShapeDtypeStruct((t // 2, RW_HEAD, LANES), F32)
    return pl.pallas_call(
        functools.partial(_scan_kernel, tt=tt),
        grid=(nb,),
        in_specs=[kv_spec(fwd)] * SCAN_NK + [kv_spec(rev)] * SCAN_NK + [v_spec(fwd), v_spec(rev)],
        out_specs=[y_spec(fwd), y_spec(rev)],
        out_shape=[y_shape, y_shape],
        scratch_shapes=[pltpu.VMEM((SCAN_KH, RW_HEAD, LANES), F32), pltpu.VMEM((RW_HEAD, LANES), F32),
                        pltpu.VMEM((2, SCAN_NK, SCAN_KH, LANES), F32), pltpu.VMEM((2, RW_HEAD, LANES), F32),
                        pltpu.VMEM((tt, RW_HEAD, LANES), F32)],
        compiler_params=_cparams(("arbitrary",)),
    )(*ks, *ks, vv, vv)


def _to_scan_k(x0, x1):
    b, t, _ = x0.shape
    s = jnp.stack([x0, x1], 0).reshape(2, b, t, RW_HEADS, 2, SCAN_KH)
    return s.transpose(2, 5, 4, 0, 1, 3).reshape(t, SCAN_KH, 4 * b * RW_HEADS)


def _to_scan_v(x):
    b, t, _ = x.shape
    s = x.reshape(b, t, RW_HEADS, RW_HEAD).transpose(1, 3, 0, 2).reshape(t, RW_HEAD, b * RW_HEADS)
    return jnp.concatenate([s, s, s, s], -1)


def _next_step_rows(x, l):
    lat, ctx = x[:, :l], x[:, l:]
    z = jnp.zeros_like(x[:, :1])
    nxt_f = jnp.concatenate([lat[:, 1:], z, ctx[:, 1:], lat[:, :1]], 1)
    nxt_r = jnp.concatenate([z, lat[:, :-1], lat[:, l - 1:l], ctx[:, :-1]], 1)
    return nxt_f, nxt_r


def _from_scan_y(yf, yr, b):
    t2 = yf.shape[0]
    q = b * RW_HEADS
    outs = []
    for d, y in enumerate((yf, yr)):
        s = y.reshape(t2, RW_HEAD, 2, 2 * q)[..., d * q:(d + 1) * q]
        outs.append(s.reshape(t2, RW_HEAD, 2, b, RW_HEADS).transpose(3, 0, 2, 4, 1).reshape(b, 2 * t2, RW_WIDTH))
    return outs[0], outs[1]


def _readout_kernel(y0_ref, y1_ref, bon0_ref, bon1_ref, g0_ref, g1_ref, lw_ref, lb_ref, o_ref):
    lw, lb = lw_ref[...], lb_ref[...]
    o = (y0_ref[...] * lw + lb + bon0_ref[...]) * g0_ref[...]
    o = o + (y1_ref[...] * lw + lb + bon1_ref[...]) * g1_ref[...]
    o_ref[...] = o.astype(o_ref.dtype)


def _readout(y0, y1, bon0, bon1, g0, g1, ln_w, ln_b, tm=256):
    m, w_ = y0.shape
    row = pl.BlockSpec((tm, w_), lambda i: (i, 0))
    vec = pl.BlockSpec((1, w_), lambda i: (0, 0))
    return pl.pallas_call(
        _readout_kernel, grid=(m // tm,), in_specs=[row] * 6 + [vec, vec], out_specs=row,
        out_shape=jax.ShapeDtypeStruct((m, w_), BF16), compiler_params=_cparams(("parallel",)),
    )(y0, y1, bon0, bon1, g0, g1, ln_w, ln_b)


def _mla_prep_kernel(p_ref, c_ref, s_ref, qn_ref, kn_ref, wqa_ref, wqs_ref, wk_ref, wv_ref, q_ref, k_ref, v_ref):
    p = p_ref[...]
    cos, sin = c_ref[...], s_ref[...]
    qc = _rms(p[:, :MLA_Q_RANK], qn_ref[...]).astype(BF16)
    kvc = _rms(p[:, MLA_Q_RANK:MLA_Q_RANK + MLA_KV_RANK], kn_ref[...]).astype(BF16)
    kr = p[:, MLA_Q_RANK + MLA_KV_RANK:MLA_Q_RANK + MLA_KV_RANK + LANES]
    krs = p[:, MLA_Q_RANK + MLA_KV_RANK + LANES:MLA_Q_RANK + MLA_KV_RANK + 2 * LANES]
    krope = kr * cos + krs * sin
    qa = jnp.dot(qc, wqa_ref[...], preferred_element_type=F32)
    qs = jnp.dot(qc, wqs_ref[...], preferred_element_type=F32)
    kk = jnp.dot(kvc, wk_ref[...], preferred_element_type=F32)
    v_ref[...] = jnp.dot(kvc, wv_ref[...], preferred_element_type=F32).astype(v_ref.dtype)
    scale = MLA_QK ** -0.5
    for h in range(MLA_HEADS):
        sl = slice(h * LANES, (h + 1) * LANES)
        q_ref[:, sl] = ((qa[:, sl] * cos + qs[:, sl] * sin) * scale).astype(q_ref.dtype)
        k_ref[:, sl] = (kk[:, sl] + krope).astype(k_ref.dtype)


def _mla_prep(p, cos_t, sin_t, q_norm, kv_norm, wqa, wqs, wk, wv, dims, tm=256):
    m, pc = p.shape
    tpb = dims["T"] // tm
    full = lambda arr: pl.BlockSpec(arr.shape, lambda i: (0,) * arr.ndim)
    tab = pl.BlockSpec((tm, LANES), lambda i: (i % tpb, 0))
    hw = MLA_HEADS * LANES
    return pl.pallas_call(
        _mla_prep_kernel, grid=(m // tm,),
        in_specs=[pl.BlockSpec((tm, pc), lambda i: (i, 0)), tab, tab, full(q_norm), full(kv_norm),
                  full(wqa), full(wqs), full(wk), full(wv)],
        out_specs=[pl.BlockSpec((tm, hw), lambda i: (i, 0)), pl.BlockSpec((tm, hw), lambda i: (i, 0)),
                   pl.BlockSpec((tm, MLA_HEADS * MLA_V), lambda i: (i, 0))],
        out_shape=[jax.ShapeDtypeStruct((m, hw), BF16), jax.ShapeDtypeStruct((m, hw), BF16),
                   jax.ShapeDtypeStruct((m, MLA_HEADS * MLA_V), BF16)],
        compiler_params=_cparams(("parallel",)),
    )(p, cos_t, sin_t, q_norm, kv_norm, wqa, wqs, wk, wv)


def _flash_kernel(q_ref, k_ref, v_ref, *rest, nk):
    o_ref = rest[-7]
    state = (rest[-6:-3], rest[-3:])
    ki = pl.program_id(3)

    @pl.when(ki == 0)
    def _():
        for m_ref, l_ref, acc_ref in state:
            m_ref[...] = jnp.full_like(m_ref, -jnp.inf)
            l_ref[...] = jnp.zeros_like(l_ref)
            acc_ref[...] = jnp.zeros_like(acc_ref)

    v = v_ref[...]
    for hh, (m_ref, l_ref, acc_ref) in enumerate(state):
        q = q_ref[:, hh * LANES:(hh + 1) * LANES]
        k = k_ref[:, hh * LANES:(hh + 1) * LANES]
        s = lax.dot_general(q, k, (((1,), (1,)), ((), ())), preferred_element_type=F32)
        m_prev = m_ref[...]
        m_new = jnp.maximum(m_prev, jnp.max(s, -1, keepdims=True))
        alpha = jnp.exp(m_prev - m_new)
        pr = jnp.exp(s - m_new)
        l_ref[...] = alpha * l_ref[...] + jnp.sum(pr, -1, keepdims=True)
        acc_ref[...] = alpha * acc_ref[...] + jnp.dot(pr.astype(BF16), v, preferred_element_type=F32)
        m_ref[...] = m_new

    @pl.when(ki == nk - 1)
    def _():
        (_, l0, acc0), (_, l1, acc1) = state
        lane = lax.broadcasted_iota(I32, acc0.shape, 1)
        o_ref[...] = jnp.where(lane < MLA_V, acc0[...] / l0[...], acc1[...] / l1[...]).astype(o_ref.dtype)


def _flash(q3, k3, v3, *, tq, tk, q_off, nq, k_off, nk, prev_out=None):
    b, t, _ = q3.shape
    hp = MLA_HEADS // 2
    in_specs = [pl.BlockSpec((None, tq, 2 * LANES), lambda bi, h, qi, ki: (bi, q_off + qi, h)),
                pl.BlockSpec((None, tk, 2 * LANES), lambda bi, h, qi, ki: (bi, k_off + ki, h)),
                pl.BlockSpec((None, tk, LANES), lambda bi, h, qi, ki: (bi, k_off + ki, h))]
    args = [q3, k3, v3]
    aliases = {}
    if prev_out is not None:
        in_specs.append(pl.BlockSpec(memory_space=pl.ANY))
        args.append(prev_out)
        aliases = {3: 0}
    return pl.pallas_call(
        functools.partial(_flash_kernel, nk=nk),
        grid=(b, hp, nq, nk),
        in_specs=in_specs,
        out_specs=pl.BlockSpec((None, tq, LANES), lambda bi, h, qi, ki: (bi, q_off + qi, h)),
        out_shape=jax.ShapeDtypeStruct((b, t, MLA_HEADS * MLA_V), BF16),
        scratch_shapes=[pltpu.VMEM((tq, 1), F32), pltpu.VMEM((tq, 1), F32), pltpu.VMEM((tq, LANES), F32)] * 2,
        input_output_aliases=aliases,
        compiler_params=_cparams(("parallel", "parallel", "parallel", "arbitrary")),
    )(*args)


def _topk_rows(s, kk, payload=None):
    nrow = s.shape[0]
    rid = lax.broadcasted_iota(I32, s.shape, 0)
    vals, sel = [], []
    for _ in range(kk):
        m = jnp.max(s, axis=0, keepdims=True)
        pos = jnp.min(jnp.where(s == m, rid, nrow), axis=0, keepdims=True)
        hit = rid == pos
        vals.append(m)
        sel.append(pos if payload is None else jnp.max(jnp.where(hit, payload, -1), axis=0, keepdims=True))
        s = jnp.where(hit, -jnp.inf, s)
    return jnp.concatenate(vals, 0), jnp.concatenate(sel, 0)


def _peer_topk_kernel(q0_ref, q1_ref, keys_ref, idx_ref, gate_ref):
    nt = (((1,), (1,)), ((), ()))
    s0 = lax.dot_general(keys_ref[0].astype(BF16), q0_ref[...].astype(BF16), nt, preferred_element_type=F32)
    s1 = lax.dot_general(keys_ref[1].astype(BF16), q1_ref[...].astype(BF16), nt, preferred_element_type=F32)
    v1, i1 = _topk_rows(s0, PEER_TOPK)
    v2, i2 = _topk_rows(s1, PEER_TOPK)
    assert PEER_TOPK == 2 * SUBLANES
    rid8 = lax.broadcasted_iota(I32, (SUBLANES, s0.shape[1]), 0)
    cs = [v1[0:1, :] + v2]
    ci = [i1[0:1, :] * PEER_NKEYS + i2]
    for a in range(1, SUBLANES):
        nb = PEER_TOPK // (a + 1)
        s = v1[a:a + 1, :] + v2[0:SUBLANES, :]
        cs.append(s if nb >= SUBLANES else jnp.where(rid8 < nb, s, -jnp.inf))
        ci.append(i1[a:a + 1, :] * PEER_NKEYS + i2[0:SUBLANES, :])
    cs.append(v1[SUBLANES:, :] + v2[0:1, :])
    ci.append(i1[SUBLANES:, :] * PEER_NKEYS + i2[0:1, :])
    top_s, top_i = _topk_rows(jnp.concatenate(cs, 0), PEER_TOPK, payload=jnp.concatenate(ci, 0))
    e = jnp.exp(top_s - top_s[0:1, :])
    idx_ref[...] = top_i
    gate_ref[...] = e / jnp.sum(e, axis=0, keepdims=True)


def _peer_topk(q, keys, tm=256):
    m = q.shape[0]
    out_blk = pl.BlockSpec((None, PEER_TOPK, tm), lambda i, h: (h, 0, i))
    return pl.pallas_call(
        _peer_topk_kernel, grid=(m // tm, PEER_HEADS),
        in_specs=[pl.BlockSpec((tm, PEER_HALF), lambda i, h: (i, 2 * h)),
                  pl.BlockSpec((tm, PEER_HALF), lambda i, h: (i, 2 * h + 1)),
                  pl.BlockSpec((None, 2, PEER_NKEYS, PEER_HALF), lambda i, h: (h, 0, 0, 0))],
        out_specs=[out_blk, out_blk],
        out_shape=[jax.ShapeDtypeStruct((PEER_HEADS, PEER_TOPK, m), I32),
                   jax.ShapeDtypeStruct((PEER_HEADS, PEER_TOPK, m), F32)],
        compiler_params=_cparams(("parallel", "parallel")),
    )(q, q, keys)


def _gelu(x):
    return 0.5 * x * (1.0 + lax.erf(x * (2.0 ** -0.5)))


def _peer_gates_kernel(idx_ref, gate_ref, g_ref, *, tm, unroll):
    rid = lax.broadcasted_iota(I32, (PEER_NKEYS, PEER_SEL), 0)
    nt = (((1,), (1,)), ((), ()))

    def body(t, carry):
        for u in range(unroll):
            n = t * unroll + u
            idx = idx_ref[pl.ds(n, 1), :]
            gate = gate_ref[pl.ds(n, 1), :]
            i1 = idx // PEER_NKEYS
            i2 = idx - i1 * PEER_NKEYS
            a = jnp.where(rid == i1, gate, 0.0).astype(BF16)
            b = jnp.where(rid == i2, 1.0, 0.0).astype(BF16)
            g_ref[n] = lax.dot_general(a, b, nt, preferred_element_type=F32).astype(g_ref.dtype)
        return carry

    lax.fori_loop(0, tm // unroll, body, 0)


def _peer_gates(idx, gate, tm=128, unroll=8):
    m = idx.shape[0]
    sel = pl.BlockSpec((tm, PEER_SEL), lambda i: (i, 0))
    return pl.pallas_call(
        functools.partial(_peer_gates_kernel, tm=tm, unroll=unroll), grid=(m // tm,),
        in_specs=[sel, sel],
        out_specs=pl.BlockSpec((tm, PEER_NKEYS, PEER_NKEYS), lambda i: (i, 0, 0)),
        out_shape=jax.ShapeDtypeStruct((m, PEER_NKEYS, PEER_NKEYS), BF16),
        compiler_params=_cparams(("parallel",)),
    )(idx, gate)


def _peer_dense_kernel(h_ref, u_ref, v_ref, g_ref, oin_ref, o_ref):
    act = lax.dot_general(h_ref[...], u_ref[...], (((1,), (1,)), ((), ())), preferred_element_type=F32)
    g = g_ref[...].astype(F32).reshape(act.shape)
    wt = jnp.where(g != 0.0, _gelu(act) * g, 0.0).astype(BF16)
    o_ref[...] = oin_ref[...] + jnp.dot(wt, v_ref[...], preferred_element_type=F32)


def _peer_dense(h2, u_tab, v_tab, g, *, tm, te):
    m, d = h2.shape
    e = v_tab.shape[0]
    assert m % tm == 0 and e % te == 0
    tok = pl.BlockSpec((tm, d), lambda j, i: (i, 0))
    tab = pl.BlockSpec((te, d), lambda j, i: (j, 0), pipeline_mode=pl.Buffered(1))
    return pl.pallas_call(
        _peer_dense_kernel, grid=(e // te, m // tm),
        in_specs=[tok, tab, tab, pl.BlockSpec((tm, te // PEER_NKEYS, PEER_NKEYS), lambda j, i: (i, j, 0)), tok],
        out_specs=tok,
        out_shape=jax.ShapeDtypeStruct((m, d), F32),
        input_output_aliases={4: 0},
        compiler_params=_cparams(("arbitrary", "arbitrary")),
    )(h2, u_tab, v_tab, g, jnp.zeros((m, d), F32))


def _block_diag2(w):
    z = jnp.zeros_like(w[0])
    return jnp.concatenate([jnp.concatenate([w[0], z], 1), jnp.concatenate([z, w[1]], 1)], 0)


def _rope_swap_cols(w_rope):
    half = MLA_ROPE // 2
    return jnp.concatenate([-w_rope[..., half:], w_rope[..., :half]], -1)


def _prep_layer_weights(l, w_in, mla_w_uq, mla_w_ukv, rk_w_up, rk_a_up, rk_g_up):
    d = w_in.shape[1]
    wi = w_in[l]
    lo, hi = RW_COLS, RW_COLS + MLA_Q_RANK + MLA_KV_RANK + MLA_ROPE
    w_rw = jnp.pad(wi[:, :lo], ((0, 0), (0, RW_COLS_PAD - RW_COLS))).astype(BF16)
    w_kr = wi[:, hi - MLA_ROPE:hi]
    place = lambda w: jnp.pad(w, ((0, 0), (MLA_NOPE, LANES - MLA_NOPE - MLA_ROPE)))
    w_mla = jnp.concatenate([wi[:, lo:hi - MLA_ROPE], place(w_kr), place(_rope_swap_cols(w_kr))], 1).astype(BF16)
    w_gate = wi[:, hi:].astype(BF16)

    uq = mla_w_uq[l].reshape(MLA_Q_RANK, MLA_HEADS, MLA_QK)
    zeros = jnp.zeros((MLA_Q_RANK, MLA_HEADS, LANES - MLA_QK), F32)
    wqa = jnp.concatenate([uq, zeros], -1).reshape(MLA_Q_RANK, -1).astype(BF16)
    zn = jnp.zeros((MLA_Q_RANK, MLA_HEADS, MLA_NOPE), F32)
    wqs = jnp.concatenate([zn, _rope_swap_cols(uq[..., MLA_NOPE:]), zeros], -1).reshape(MLA_Q_RANK, -1).astype(BF16)
    ukv = mla_w_ukv[l].reshape(MLA_KV_RANK, MLA_HEADS, MLA_NOPE + MLA_V)
    wk = jnp.pad(ukv[..., :MLA_NOPE], ((0, 0), (0, 0), (0, LANES - MLA_NOPE))).reshape(MLA_KV_RANK, -1).astype(BF16)
    wv = ukv[..., MLA_NOPE:].reshape(MLA_KV_RANK, -1).astype(BF16)

    wup = _block_diag2(rk_w_up[l]).astype(BF16)
    aup = _block_diag2(rk_a_up[l]).astype(BF16)
    gup = jnp.pad(_block_diag2(rk_g_up[l]), ((0, LANES - 2 * RW_GATE_LORA), (0, 0))).astype(BF16)
    return dict(w_rw=w_rw, w_mla=w_mla, w_gate=w_gate, wqa=wqa, wqs=wqs, wk=wk, wv=wv, wup=wup, aup=aup, gup=gup)


def _rope_lane_tables(l, c):
    rows = l // GRID_W
    row = jnp.repeat(jnp.arange(rows, dtype=F32), GRID_W)
    col = jnp.tile(jnp.arange(GRID_W, dtype=F32), rows)
    n_freq = MLA_ROPE // 4
    freqs = ROPE_BASE ** (-jnp.arange(n_freq, dtype=F32) / n_freq)
    ang = jnp.concatenate([row[:, None] * freqs, col[:, None] * freqs], -1)
    cos, sin = jnp.cos(ang), jnp.sin(ang)
    tail = LANES - MLA_NOPE - MLA_ROPE
    cos_t = jnp.concatenate([jnp.ones((l, MLA_NOPE), F32), cos, cos, jnp.ones((l, tail), F32)], -1)
    sin_t = jnp.concatenate([jnp.zeros((l, MLA_NOPE), F32), sin, sin, jnp.zeros((l, tail), F32)], -1)
    cos_t = jnp.concatenate([cos_t, jnp.ones((c, LANES), F32)], 0)
    sin_t = jnp.concatenate([sin_t, jnp.zeros((c, LANES), F32)], 0)
    return cos_t, sin_t


def _peer(h2, w_q, keys, u_tab, v_tab, *, te, tm=256):
    m, d = h2.shape
    q = _mm(h2, w_q, tm=256, tn=_pick_tile(w_q.shape[1], 1024))
    idx_t, gate_t = _peer_topk(q, keys)
    idx = idx_t.transpose(2, 0, 1).reshape(m, PEER_SEL)
    gate = gate_t.transpose(2, 0, 1).reshape(m, PEER_SEL)
    g = _peer_gates(idx, gate)
    return _peer_dense(h2, u_tab, v_tab, g, tm=tm, te=te)


def _forward(x, c, ctx, c_ctx, ada_w, ada_b, norm_mix_g, w_in, gate_b, rk_mu, rk_w0, rk_w_up, rk_a0,
             rk_a_up, rk_g_up, rk_k_k, rk_k_a, rk_r_k, rk_ln_w, rk_ln_b, mla_q_norm, mla_w_uq, mla_kv_norm,
             mla_w_ukv, w_branch_a, w_branch_b, w_out, norm_ffn_g, peer_w_q, peer_keys, peer_u, peer_v,
             final_norm_g, *, flash_tq=512, flash_tk=2816, scan_tt=32, peer_te=2048):
    nb, l, d = x.shape
    cl = ctx.shape[1]
    t = l + cl
    m = nb * t
    depth = w_in.shape[0]
    dims = dict(B=nb, L=l, C=cl, T=t)
    assert nb == 2 and 4 * nb * RW_HEADS == LANES and l % 256 == 0 and cl % 256 == 0

    xa = jnp.concatenate([x, ctx], 1).reshape(m, d)
    cos_t, sin_t = _rope_lane_tables(l, cl)
    seg = jnp.arange(RW_WIDTH, dtype=I32) // RW_HEAD
    bd = (seg[:, None] == seg[None, :]).astype(BF16)
    cvec = jnp.zeros((SUBLANES, d), F32).at[:nb].set(jax.nn.silu(c)).at[nb].set(jax.nn.silu(c_ctx))

    delta, mods_prev = None, None
    out = None
    for li in range(depth):
        wl = _prep_layer_weights(li, w_in, mla_w_uq, mla_w_ukv, rk_w_up, rk_a_up, rk_g_up)
        tn_ada = _pick_tile(6 * d, 768)
        mods = _mm(cvec, ada_w[li], tm=SUBLANES, tn=tn_ada, epilogue=lambda acc, bias: acc + bias,
                   extras=[(ada_b[li][None, :], pl.BlockSpec((1, tn_ada), lambda j, i: (0, j)))])
        mods = mods[:nb + 1].reshape((nb + 1) * 6, 1, d)
        g_mix = norm_mix_g[li][None, :]
        if li == 0:
            h = _norm_mod(xa, g_mix, mods, 0, 1, dims)
        else:
            xa, h = _resid_norm_mod(xa, delta, mods_prev, 5, g_mix, mods, 0, 1, dims)

        p_rw = _mm(h, wl["w_rw"], tm=256, tn=_pick_tile(RW_COLS_PAD, 1152))
        p_mla = _mm(h, wl["w_mla"], tm=256, tn=MLA_P_COLS)
        p_gate = _mm(h, wl["w_gate"], tm=256, tn=1024)

        mu = jnp.pad(rk_mu[li], (0, RW_COLS_PAD - RW_COLS))[None, :]
        feats = _rwfeat(p_rw.reshape(nb, t, RW_COLS_PAD), mu, rk_w0[li], wl["wup"], rk_a0[li], wl["aup"], wl["gup"],
                        rk_k_k[li][None, :], rk_k_a[li][None, :], rk_r_k[li].reshape(1, RW_WIDTH), bd, dims)
        r, nkk, v, dec0, dec1, b0, b1, ke0, ke1, bon0, bon1, g0, g1 = feats
        ks = (_to_scan_k(dec0, dec1), _to_scan_k(b0, b1), _to_scan_k(ke0, ke1), _to_scan_k(r, r),
              _to_scan_k(*_next_step_rows(nkk, l)))
        ynf, ynr = _scan(ks, _to_scan_v(v), l, tt=scan_tt)
        yn0, yn1 = _from_scan_y(ynf, ynr, nb)
        flat = lambda z: z.reshape(m, RW_WIDTH)
        ya = _readout(flat(yn0), flat(yn1), flat(bon0), flat(bon1), flat(g0), flat(g1),
                      rk_ln_w[li][None, :], rk_ln_b[li][None, :])

        q, k, vv = _mla_prep(p_mla, cos_t, sin_t, mla_q_norm[li][None, :], mla_kv_norm[li][None, :],
                             wl["wqa"], wl["wqs"], wl["wk"], wl["wv"], dims)
        q3, k3, v3 = (z.reshape(nb, t, -1) for z in (q, k, vv))
        yb = _flash(q3, k3, v3, tq=flash_tq, tk=flash_tk, q_off=0, nq=l // flash_tq, k_off=0, nk=t // flash_tk)
        yb = _flash(q3, k3, v3, tq=cl, tk=cl, q_off=l // cl, nq=1, k_off=l // cl, nk=1, prev_out=yb)
        yb = yb.reshape(m, MLA_HEADS * MLA_V)

        tpb, lt = t // 256, l // 256
        br_a = _mm(ya, w_branch_a[li].astype(BF16), tm=256, tn=1024)
        gb = gate_b[li]
        merged = _mm(
            yb, w_branch_b[li].astype(BF16), tm=256, tn=1024, out_dtype=BF16,
            epilogue=lambda acc, ga, gbb, ba, bb, a: _sigmoid(ga + ba) * a + _sigmoid(gbb + bb) * acc,
            extras=[(p_gate, pl.BlockSpec((256, 1024), lambda j, i: (i, j))),
                    (p_gate, pl.BlockSpec((256, 1024), lambda j, i: (i, j + d // 1024))),
                    (gb[0][None, :], pl.BlockSpec((1, 1024), lambda j, i: (0, j))),
                    (gb[1][None, :], pl.BlockSpec((1, 1024), lambda j, i: (0, j))),
                    (br_a, pl.BlockSpec((256, 1024), lambda j, i: (i, j)))])
        x1 = _mm(merged, w_out[li].astype(BF16), tm=256, tn=1024,
                 epilogue=lambda acc, xr, gt: xr + gt * acc,
                 extras=[(xa, pl.BlockSpec((256, 1024), lambda j, i: (i, j))),
                         (mods, pl.BlockSpec((None, 1, 1024), lambda j, i: (_mod_row(i, tpb, lt, 2, nb), 0, j)))])

        h2 = _norm_mod(x1, norm_ffn_g[li][None, :], mods, 3, 4, dims)
        delta = _peer(h2, peer_w_q[li].astype(BF16), peer_keys[li], peer_u[li].astype(BF16),
                      peer_v[li].astype(BF16), te=peer_te)
        xa, mods_prev = x1, mods

    out = _resid_final_norm(xa.reshape(nb, t, d), delta.reshape(nb, t, d), mods_prev, 5,
                            final_norm_g[None, :], dims)
    return out


def kernel(x, c, ctx, c_ctx, ada_w, ada_b, norm_mix_g, w_in, gate_b, rk_mu, rk_w0, rk_w_up, rk_a0, rk_a_up, rk_g_up, rk_k_k, rk_k_a, rk_r_k, rk_ln_w, rk_ln_b, mla_q_norm, mla_w_uq, mla_kv_norm, mla_w_ukv, w_branch_a, w_branch_b, w_out, norm_ffn_g, peer_w_q, peer_keys, peer_u, peer_v, final_norm_g):
    return _forward(x, c, ctx, c_ctx, ada_w, ada_b, norm_mix_g, w_in, gate_b, rk_mu, rk_w0, rk_w_up, rk_a0,
                    rk_a_up, rk_g_up, rk_k_k, rk_k_a, rk_r_k, rk_ln_w, rk_ln_b, mla_q_norm, mla_w_uq,
                    mla_kv_norm, mla_w_ukv, w_branch_a, w_branch_b, w_out, norm_ffn_g, peer_w_q, peer_keys,
                    peer_u, peer_v, final_norm_g)
```

```python
import functools

import jax
import jax.numpy as jnp
from jax import lax
from jax.experimental import pallas as pl
from jax.experimental.pallas import tpu as pltpu

F32 = jnp.float32
BF16 = jnp.bfloat16
I32 = jnp.int32

NORM_EPS = 1e-6
GRID_W = 64
ROPE_BASE = 10000.0

RW_HEADS = 16
RW_HEAD = 64
RW_WIDTH = RW_HEADS * RW_HEAD
RW_DECAY_LORA = 64
RW_A_LORA = 64
RW_GATE_LORA = 32
RW_GN_EPS = 64e-5
RW_COLS = 3 * RW_WIDTH + 2 * (RW_DECAY_LORA + RW_A_LORA + RW_GATE_LORA)
RW_COLS_PAD = 3456

MLA_HEADS = 16
MLA_NOPE = 64
MLA_ROPE = 32
MLA_QK = MLA_NOPE + MLA_ROPE
MLA_V = 64
MLA_Q_RANK = 512
MLA_KV_RANK = 256
MLA_HEAD_PAD = 128
MLA_P_COLS = MLA_Q_RANK + MLA_KV_RANK + 2 * MLA_HEAD_PAD

PEER_HEADS = 8
PEER_NKEYS = 128
PEER_HALF = 128
PEER_TOPK = 16
PEER_SEL = PEER_HEADS * PEER_TOPK

LANES = 128
SUBLANES = 8
VMEM_LIMIT = 56 * 1024 * 1024


def _cparams(sem):
    return pltpu.CompilerParams(dimension_semantics=sem, vmem_limit_bytes=VMEM_LIMIT)


def _pick_tile(n, cap, mult=LANES):
    best = None
    t = mult
    while t <= min(n, cap):
        if n % t == 0:
            best = t
        t += mult
    assert best is not None, (n, cap)
    return best


def _mm_kernel(*refs, n_extra, epilogue):
    a_ref, w_ref = refs[0], refs[1]
    extra = refs[2:2 + n_extra]
    o_ref = refs[2 + n_extra]
    acc = jnp.dot(a_ref[...].astype(BF16), w_ref[...].astype(BF16), preferred_element_type=F32)
    if epilogue is not None:
        acc = epilogue(acc, *[e[...] for e in extra])
    o_ref[...] = acc.astype(o_ref.dtype)


def _mm(a, w, *, tm, tn, out_dtype=F32, epilogue=None, extras=()):
    m, k = a.shape
    k2, n = w.shape
    assert k == k2 and m % tm == 0 and n % tn == 0, (a.shape, w.shape, tm, tn)
    in_specs = [pl.BlockSpec((tm, k), lambda j, i: (i, 0)),
                pl.BlockSpec((k, tn), lambda j, i: (0, j))] + [s for _, s in extras]
    return pl.pallas_call(
        functools.partial(_mm_kernel, n_extra=len(extras), epilogue=epilogue),
        grid=(n // tn, m // tm),
        in_specs=in_specs,
        out_specs=pl.BlockSpec((tm, tn), lambda j, i: (i, j)),
        out_shape=jax.ShapeDtypeStruct((m, n), out_dtype),
        compiler_params=_cparams(("parallel", "parallel")),
    )(a, w, *[x for x, _ in extras])


def _sigmoid(x):
    return 1.0 / (1.0 + jnp.exp(-x))


def _mod_row(i, tiles_per_batch, lat_tiles, part, n_batch):
    which = jnp.where(i % tiles_per_batch >= lat_tiles, n_batch, i // tiles_per_batch)
    return which * 6 + part


def _rms(x, g):
    return x * lax.rsqrt(jnp.mean(x * x, -1, keepdims=True) + NORM_EPS) * g


def _norm_mod_kernel(x_ref, g_ref, sh_ref, sc_ref, h_ref):
    y = _rms(x_ref[...], g_ref[...])
    h_ref[...] = (y * (1.0 + sc_ref[...]) + sh_ref[...]).astype(h_ref.dtype)


def _resid_norm_mod_kernel(x_ref, d_ref, gt_ref, g_ref, sh_ref, sc_ref, xo_ref, h_ref):
    x = x_ref[...] + gt_ref[...] * d_ref[...]
    xo_ref[...] = x
    y = _rms(x, g_ref[...])
    h_ref[...] = (y * (1.0 + sc_ref[...]) + sh_ref[...]).astype(h_ref.dtype)


def _resid_final_norm_kernel(x_ref, d_ref, gt_ref, g_ref, o_ref):
    x = x_ref[...] + gt_ref[...] * d_ref[...]
    o_ref[...] = _rms(x, g_ref[...])


def _mod_spec(d, part, dims, tm):
    tpb, lt, nb = dims["T"] // tm, dims["L"] // tm, dims["B"]
    return pl.BlockSpec((None, 1, d), lambda i: (_mod_row(i, tpb, lt, part, nb), 0, 0))


def _norm_mod(x, g, mods, part_sh, part_sc, dims, tm=256):
    m, d = x.shape
    row = pl.BlockSpec((tm, d), lambda i: (i, 0))
    return pl.pallas_call(
        _norm_mod_kernel, grid=(m // tm,),
        in_specs=[row, pl.BlockSpec((1, d), lambda i: (0, 0)),
                  _mod_spec(d, part_sh, dims, tm), _mod_spec(d, part_sc, dims, tm)],
        out_specs=row, out_shape=jax.ShapeDtypeStruct((m, d), BF16),
        compiler_params=_cparams(("parallel",)),
    )(x, g, mods, mods)


def _resid_norm_mod(x, delta, mods_gt, part_gt, g, mods, part_sh, part_sc, dims, tm=256):
    m, d = x.shape
    row = pl.BlockSpec((tm, d), lambda i: (i, 0))
    return pl.pallas_call(
        _resid_norm_mod_kernel, grid=(m // tm,),
        in_specs=[row, row, _mod_spec(d, part_gt, dims, tm), pl.BlockSpec((1, d), lambda i: (0, 0)),
                  _mod_spec(d, part_sh, dims, tm), _mod_spec(d, part_sc, dims, tm)],
        out_specs=[row, row],
        out_shape=[jax.ShapeDtypeStruct((m, d), F32), jax.ShapeDtypeStruct((m, d), BF16)],
        compiler_params=_cparams(("parallel",)),
    )(x, delta, mods_gt, g, mods, mods)


def _resid_final_norm(x3, delta3, mods, part_gt, g, dims, tm=256):
    b, t, d = x3.shape
    lt = dims["L"] // tm
    blk = pl.BlockSpec((None, tm, d), lambda bi, i: (bi, i, 0))
    return pl.pallas_call(
        _resid_final_norm_kernel, grid=(b, lt),
        in_specs=[blk, blk, pl.BlockSpec((None, 1, d), lambda bi, i: (bi * 6 + part_gt, 0, 0)),
                  pl.BlockSpec((1, d), lambda bi, i: (0, 0))],
        out_specs=blk, out_shape=jax.ShapeDtypeStruct((b, dims["L"], d), F32),
        compiler_params=_cparams(("parallel", "parallel")),
    )(x3, delta3, mods, g)


def _segsum(x, bd):
    hi = x.astype(BF16)
    lo = (x - hi.astype(F32)).astype(BF16)
    return (jnp.dot(hi, bd, preferred_element_type=F32) + jnp.dot(lo, bd, preferred_element_type=F32))


def _rwfeat_kernel(p_ref, pp_ref, pn_ref, mu_ref, w0_ref, wup_ref, a0_ref, aup_ref, gup_ref, kk_ref, ka_ref,
                   rk_ref, bd_ref,
                   r_ref, nkk_ref, v_ref, dec0_ref, dec1_ref, b0_ref, b1_ref, ke0_ref, ke1_ref,
                   bon0_ref, bon1_ref, g0_ref, g1_ref, *, tm, lat_tiles, all_tiles):
    i = pl.program_id(1)
    p = p_ref[...]
    prev_ok = jnp.logical_and(i != 0, i != lat_tiles)
    next_ok = jnp.logical_and(i != lat_tiles - 1, i != all_tiles - 1)
    prow = jnp.where(prev_ok, pp_ref[SUBLANES - 1:SUBLANES, :], 0.0)
    nrow = jnp.where(next_ok, pn_ref[0:1, :], 0.0)
    rid = lax.broadcasted_iota(I32, (tm, 1), 0)
    prev = jnp.where(rid == 0, prow, pltpu.roll(p, 1, 0))
    nxt = jnp.where(rid == tm - 1, nrow, pltpu.roll(p, tm - 1, 0))
    ps = p + (0.5 * (prev + nxt) - p) * mu_ref[...]

    w_ = RW_WIDTH
    r = ps[:, 0:w_]
    k = ps[:, w_:2 * w_]
    v = ps[:, 2 * w_:3 * w_]
    wd = ps[:, 3 * w_:3 * w_ + LANES]
    ad = ps[:, 3 * w_ + LANES:3 * w_ + 2 * LANES]
    gd = ps[:, 3 * w_ + 2 * LANES:3 * w_ + 3 * LANES]
    bd = bd_ref[...]

    lw = jnp.dot(jnp.tanh(wd).astype(BF16), wup_ref[...], preferred_element_type=F32)
    la = jnp.dot(ad.astype(BF16), aup_ref[...], preferred_element_type=F32)
    lg = jnp.dot(_sigmoid(gd).astype(BF16), gup_ref[...], preferred_element_type=F32)

    kk = k * kk_ref[...]
    kk = kk * lax.rsqrt(_segsum(kk * kk, bd) + 1e-12)
    r_ref[...] = r
    nkk_ref[...] = -kk
    v_ref[...] = v
    rk = rk_ref[...]
    ka = ka_ref[...]
    outs = ((dec0_ref, b0_ref, ke0_ref, bon0_ref, g0_ref), (dec1_ref, b1_ref, ke1_ref, bon1_ref, g1_ref))
    for d in range(2):
        dec_ref, b_ref, ke_ref, bon_ref, g_ref = outs[d]
        z = w0_ref[d:d + 1, :] + lw[:, d * w_:(d + 1) * w_]
        nz = -z
        softplus = jnp.maximum(nz, 0.0) + jnp.log(1.0 + jnp.exp(-jnp.abs(nz)))
        wlog = -softplus - 0.5
        dec_ref[...] = jnp.exp(-jnp.exp(wlog))
        a = _sigmoid(a0_ref[d:d + 1, :] + la[:, d * w_:(d + 1) * w_])
        b_ref[...] = kk * a
        ke = k * (1.0 + (a - 1.0) * ka)
        ke_ref[...] = ke
        bon_ref[...] = _segsum(r * ke * rk, bd) * v
        g_ref[...] = lg[:, d * w_:(d + 1) * w_]


def _rwfeat(p3, mu, w0, wup, a0, aup, gup, k_k, k_a, r_k, bd, dims, tm=128):
    b, t, wc = p3.shape
    lt, at = dims["L"] // tm, t // tm
    hb = tm // SUBLANES
    nblk8 = t // SUBLANES
    w_ = RW_WIDTH
    full = lambda arr: pl.BlockSpec(arr.shape, lambda bi, i: (0,) * arr.ndim)
    out_blk = pl.BlockSpec((None, tm, w_), lambda bi, i: (bi, i, 0))
    n_out = 13
    return pl.pallas_call(
        functools.partial(_rwfeat_kernel, tm=tm, lat_tiles=lt, all_tiles=at),
        grid=(b, at),
        in_specs=[pl.BlockSpec((None, tm, wc), lambda bi, i: (bi, i, 0)),
                  pl.BlockSpec((None, SUBLANES, wc), lambda bi, i: (bi, jnp.maximum(i * hb - 1, 0), 0)),
                  pl.BlockSpec((None, SUBLANES, wc), lambda bi, i: (bi, jnp.minimum((i + 1) * hb, nblk8 - 1), 0)),
                  full(mu), full(w0), full(wup), full(a0), full(aup), full(gup), full(k_k), full(k_a),
                  full(r_k), full(bd)],
        out_specs=[out_blk] * n_out,
        out_shape=[jax.ShapeDtypeStruct((b, t, w_), F32)] * n_out,
        compiler_params=_cparams(("parallel", "parallel")),
    )(p3, p3, p3, mu, w0, wup, a0, aup, gup, k_k, k_a, r_k, bd)


SCAN_KH = RW_HEAD // 2


SCAN_NK = 5


def _scan_kernel(*refs, tt):
    kf_refs, kr_refs = refs[:SCAN_NK], refs[SCAN_NK:2 * SCAN_NK]
    vf_ref, vr_ref, yf_ref, yr_ref, s_ref, sa_ref, mrg_ref, acc_ref, yraw_ref = refs[2 * SCAN_NK:]

    @pl.when(pl.program_id(0) == 0)
    def _():
        s_ref[...] = jnp.zeros_like(s_ref)
        sa_ref[...] = jnp.zeros_like(sa_ref)

    lane = lax.broadcasted_iota(I32, (RW_HEAD, LANES), 1)
    lanek = lax.broadcasted_iota(I32, (SCAN_KH, LANES), 1)
    quarter = LANES // 4
    fwdk = (lanek // quarter) % 2 == 0
    laneh = lax.broadcasted_iota(I32, (RW_HEAD // 2, LANES), 1)
    fwdh = (laneh // quarter) % 2 == 0

    half = RW_HEAD // 2

    def merge_rows(t, slot):
        for q in range(SCAN_NK):
            mrg_ref[slot, q] = jnp.where(fwdk, kf_refs[q][t], kr_refs[q][tt - 1 - t])

    merge_rows(0, 0)

    def one_step(t):
        tr = tt - 1 - t
        slot = t % 2

        def half_pass(hv, carry):
            sl = pl.ds(pl.multiple_of(hv * half, half), half)
            vt = jnp.where(fwdh, vf_ref[t, sl, :], vr_ref[tr, sl, :])
            sah = sa_ref[sl, :]
            yacc = jnp.zeros((half, LANES), F32)
            sn = jnp.zeros((half, LANES), F32)
            for k in range(SCAN_KH):
                row = lambda q: mrg_ref[slot, q, pl.ds(k, 1), :]
                w, b, ke, r, an = row(0), row(1), row(2), row(3), row(4)
                s = s_ref[k, sl, :] * w + (sah * b + vt * ke)
                s_ref[k, sl, :] = s
                yacc = yacc + s * r
                sn = sn + s * an
            acc_ref[0, sl, :] = yacc
            acc_ref[1, sl, :] = sn
            return carry

        lax.fori_loop(0, 2, half_pass, 0)
        sn = acc_ref[1]
        sa_ref[...] = sn + pltpu.roll(sn, LANES // 2, 1)
        yraw_ref[t] = acc_ref[0]
        merge_rows(jnp.minimum(t + 1, tt - 1), 1 - slot)

    def step(t, carry):
        one_step(t)
        return carry

    lax.fori_loop(0, tt, step, 0)

    def normed(t):
        yacc = yraw_ref[t]
        y = yacc + pltpu.roll(yacc, LANES // 2, 1)
        mean = jnp.mean(y, axis=0, keepdims=True)
        dlt = y - mean
        var = jnp.mean(dlt * dlt, axis=0, keepdims=True)
        return dlt * lax.rsqrt(var + RW_GN_EPS)

    low = lane < LANES // 2
    for j in range(tt // 2):
        y0, y1 = normed(2 * j), normed(2 * j + 1)
        yf_ref[j] = jnp.where(low, y0, y1)
        yr_ref[tt // 2 - 1 - j] = jnp.where(low, y1, y0)


def _scan(ks, vv, l, tt=32):
    t = vv.shape[0]
    assert len(ks) == SCAN_NK and t % tt == 0 and tt % 2 == 0 and l % tt == 0
    nb, lb = t // tt, l // tt
    cb = nb - lb
    fwd = lambda i: jnp.where(i < cb, lb + i, i - cb)
    rev = lambda i: jnp.where(i < cb, lb + (cb - 1 - i), lb - 1 - (i - cb))
    kv_spec = lambda f: pl.BlockSpec((tt, SCAN_KH, LANES), lambda i: (f(i), 0, 0))
    v_spec = lambda f: pl.BlockSpec((tt, RW_HEAD, LANES), lambda i: (f(i), 0, 0))
    y_spec = lambda f: pl.BlockSpec((tt // 2, RW_HEAD, LANES), lambda i: (f(i), 0, 0))
    y_shape = jax.ShapeDtypeStruct((t // 2, RW_HEAD, LANES), F32)
    return pl.pallas_call(
        functools.partial(_scan_kernel, tt=tt),
        grid=(nb,),
        in_specs=[kv_spec(fwd)] * SCAN_NK + [kv_spec(rev)] * SCAN_NK + [v_spec(fwd), v_spec(rev)],
        out_specs=[y_spec(fwd), y_spec(rev)],
        out_shape=[y_shape, y_shape],
        scratch_shapes=[pltpu.VMEM((SCAN_KH, RW_HEAD, LANES), F32), pltpu.VMEM((RW_HEAD, LANES), F32),
                        pltpu.VMEM((2, SCAN_NK, SCAN_KH, LANES), F32), pltpu.VMEM((2, RW_HEAD, LANES), F32),
                        pltpu.VMEM((tt, RW_HEAD, LANES), F32)],
        compiler_params=_cparams(("arbitrary",)),
    )(*ks, *ks, vv, vv)


def _to_scan_k(x0, x1):
    b, t, _ = x0.shape
    s = jnp.stack([x0, x1], 0).reshape(2, b, t, RW_HEADS, 2, SCAN_KH)
    return s.transpose(2, 5, 4, 0, 1, 3).reshape(t, SCAN_KH, 4 * b * RW_HEADS)


def _to_scan_v(x):
    b, t, _ = x.shape
    s = x.reshape(b, t, RW_HEADS, RW_HEAD).transpose(1, 3, 0, 2).reshape(t, RW_HEAD, b * RW_HEADS)
    return jnp.concatenate([s, s, s, s], -1)


def _next_step_rows(x, l):
    lat, ctx = x[:, :l], x[:, l:]
    z = jnp.zeros_like(x[:, :1])
    nxt_f = jnp.concatenate([lat[:, 1:], z, ctx[:, 1:], lat[:, :1]], 1)
    nxt_r = jnp.concatenate([z, lat[:, :-1], lat[:, l - 1:l], ctx[:, :-1]], 1)
    return nxt_f, nxt_r


def _from_scan_y(yf, yr, b):
    t2 = yf.shape[0]
    q = b * RW_HEADS
    outs = []
    for d, y in enumerate((yf, yr)):
        s = y.reshape(t2, RW_HEAD, 2, 2 * q)[..., d * q:(d + 1) * q]
        outs.append(s.reshape(t2, RW_HEAD, 2, b, RW_HEADS).transpose(3, 0, 2, 4, 1).reshape(b, 2 * t2, RW_WIDTH))
    return outs[0], outs[1]


def _readout_kernel(y0_ref, y1_ref, bon0_ref, bon1_ref, g0_ref, g1_ref, lw_ref, lb_ref, o_ref):
    lw, lb = lw_ref[...], lb_ref[...]
    o = (y0_ref[...] * lw + lb + bon0_ref[...]) * g0_ref[...]
    o = o + (y1_ref[...] * lw + lb + bon1_ref[...]) * g1_ref[...]
    o_ref[...] = o.astype(o_ref.dtype)


def _readout(y0, y1, bon0, bon1, g0, g1, ln_w, ln_b, tm=256):
    m, w_ = y0.shape
    row = pl.BlockSpec((tm, w_), lambda i: (i, 0))
    vec = pl.BlockSpec((1, w_), lambda i: (0, 0))
    return pl.pallas_call(
        _readout_kernel, grid=(m // tm,), in_specs=[row] * 6 + [vec, vec], out_specs=row,
        out_shape=jax.ShapeDtypeStruct((m, w_), BF16), compiler_params=_cparams(("parallel",)),
    )(y0, y1, bon0, bon1, g0, g1, ln_w, ln_b)


def _mla_prep_kernel(p_ref, c_ref, s_ref, qn_ref, kn_ref, wqa_ref, wqs_ref, wk_ref, wv_ref, q_ref, k_ref, v_ref):
    p = p_ref[...]
    cos, sin = c_ref[...], s_ref[...]
    qc = _rms(p[:, :MLA_Q_RANK], qn_ref[...]).astype(BF16)
    kvc = _rms(p[:, MLA_Q_RANK:MLA_Q_RANK + MLA_KV_RANK], kn_ref[...]).astype(BF16)
    kr = p[:, MLA_Q_RANK + MLA_KV_RANK:MLA_Q_RANK + MLA_KV_RANK + LANES]
    krs = p[:, MLA_Q_RANK + MLA_KV_RANK + LANES:MLA_Q_RANK + MLA_KV_RANK + 2 * LANES]
    krope = kr * cos + krs * sin
    qa = jnp.dot(qc, wqa_ref[...], preferred_element_type=F32)
    qs = jnp.dot(qc, wqs_ref[...], preferred_element_type=F32)
    kk = jnp.dot(kvc, wk_ref[...], preferred_element_type=F32)
    v_ref[...] = jnp.dot(kvc, wv_ref[...], preferred_element_type=F32).astype(v_ref.dtype)
    scale = MLA_QK ** -0.5
    for h in range(MLA_HEADS):
        sl = slice(h * LANES, (h + 1) * LANES)
        q_ref[:, sl] = ((qa[:, sl] * cos + qs[:, sl] * sin) * scale).astype(q_ref.dtype)
        k_ref[:, sl] = (kk[:, sl] + krope).astype(k_ref.dtype)


def _mla_prep(p, cos_t, sin_t, q_norm, kv_norm, wqa, wqs, wk, wv, dims, tm=256):
    m, pc = p.shape
    tpb = dims["T"] // tm
    full = lambda arr: pl.BlockSpec(arr.shape, lambda i: (0,) * arr.ndim)
    tab = pl.BlockSpec((tm, LANES), lambda i: (i % tpb, 0))
    hw = MLA_HEADS * LANES
    return pl.pallas_call(
        _mla_prep_kernel, grid=(m // tm,),
        in_specs=[pl.BlockSpec((tm, pc), lambda i: (i, 0)), tab, tab, full(q_norm), full(kv_norm),
                  full(wqa), full(wqs), full(wk), full(wv)],
        out_specs=[pl.BlockSpec((tm, hw), lambda i: (i, 0)), pl.BlockSpec((tm, hw), lambda i: (i, 0)),
                   pl.BlockSpec((tm, MLA_HEADS * MLA_V), lambda i: (i, 0))],
        out_shape=[jax.ShapeDtypeStruct((m, hw), BF16), jax.ShapeDtypeStruct((m, hw), BF16),
                   jax.ShapeDtypeStruct((m, MLA_HEADS * MLA_V), BF16)],
        compiler_params=_cparams(("parallel",)),
    )(p, cos_t, sin_t, q_norm, kv_norm, wqa, wqs, wk, wv)


def _flash_kernel(q_ref, k_ref, v_ref, *rest, nk):
    o_ref = rest[-5]
    state = (rest[-4:-2], rest[-2:])
    ki = pl.program_id(3)

    @pl.when(ki == 0)
    def _():
        for m_ref, acc_ref in state:
            m_ref[...] = jnp.full_like(m_ref, -jnp.inf)
            acc_ref[...] = jnp.zeros_like(acc_ref)

    v = v_ref[...]
    vlane = lax.broadcasted_iota(I32, v.shape, 1)
    ones = jnp.ones_like(v)
    for hh, (m_ref, acc_ref) in enumerate(state):
        own = (vlane < MLA_V) if hh == 0 else (vlane >= MLA_V)
        q = q_ref[:, hh * LANES:(hh + 1) * LANES]
        k = k_ref[:, hh * LANES:(hh + 1) * LANES]
        s = lax.dot_general(q, k, (((1,), (1,)), ((), ())), preferred_element_type=F32)
        m_prev = m_ref[...]
        m_new = jnp.maximum(m_prev, jnp.max(s, -1, keepdims=True))
        alpha = jnp.exp(m_prev - m_new)
        pr = jnp.exp(s - m_new).astype(BF16)
        acc_ref[...] = alpha * acc_ref[...] + jnp.dot(pr, jnp.where(own, v, ones), preferred_element_type=F32)
        m_ref[...] = m_new

    @pl.when(ki == nk - 1)
    def _():
        (_, acc0), (_, acc1) = state
        a0, a1 = acc0[...], acc1[...]
        lane = lax.broadcasted_iota(I32, a0.shape, 1)
        o = jnp.where(lane < MLA_V, a0 / pltpu.roll(a0, MLA_V, 1), a1 / pltpu.roll(a1, MLA_V, 1))
        o_ref[...] = o.astype(o_ref.dtype)


def _flash(q3, k3, v3, *, tq, tk, q_off, nq, k_off, nk, prev_out=None):
    b, t, _ = q3.shape
    hp = MLA_HEADS // 2
    in_specs = [pl.BlockSpec((None, tq, 2 * LANES), lambda bi, h, qi, ki: (bi, q_off + qi, h)),
                pl.BlockSpec((None, tk, 2 * LANES), lambda bi, h, qi, ki: (bi, k_off + ki, h)),
                pl.BlockSpec((None, tk, LANES), lambda bi, h, qi, ki: (bi, k_off + ki, h))]
    args = [q3, k3, v3]
    aliases = {}
    if prev_out is not None:
        in_specs.append(pl.BlockSpec(memory_space=pl.ANY))
        args.append(prev_out)
        aliases = {3: 0}
    return pl.pallas_call(
        functools.partial(_flash_kernel, nk=nk),
        grid=(b, hp, nq, nk),
        in_specs=in_specs,
        out_specs=pl.BlockSpec((None, tq, LANES), lambda bi, h, qi, ki: (bi, q_off + qi, h)),
        out_shape=jax.ShapeDtypeStruct((b, t, MLA_HEADS * MLA_V), BF16),
        scratch_shapes=[pltpu.VMEM((tq, 1), F32), pltpu.VMEM((tq, LANES), F32)] * 2,
        input_output_aliases=aliases,
        compiler_params=_cparams(("parallel", "parallel", "parallel", "arbitrary")),
    )(*args)


def _topk_rows(s, kk, payload=None):
    nrow = s.shape[0]
    rid = lax.broadcasted_iota(I32, s.shape, 0)
    vals, sel = [], []
    for _ in range(kk):
        m = jnp.max(s, axis=0, keepdims=True)
        pos = jnp.min(jnp.where(s == m, rid, nrow), axis=0, keepdims=True)
        hit = rid == pos
        vals.append(m)
        sel.append(pos if payload is None else jnp.max(jnp.where(hit, payload, -1), axis=0, keepdims=True))
        s = jnp.where(hit, -jnp.inf, s)
    return jnp.concatenate(vals, 0), jnp.concatenate(sel, 0)


def _peer_topk_kernel(q0_ref, q1_ref, keys_ref, idx_ref, gate_ref):
    nt = (((1,), (1,)), ((), ()))
    s0 = lax.dot_general(keys_ref[0].astype(BF16), q0_ref[...].astype(BF16), nt, preferred_element_type=F32)
    s1 = lax.dot_general(keys_ref[1].astype(BF16), q1_ref[...].astype(BF16), nt, preferred_element_type=F32)
    v1, i1 = _topk_rows(s0, PEER_TOPK)
    v2, i2 = _topk_rows(s1, PEER_TOPK)
    assert PEER_TOPK == 2 * SUBLANES
    rid8 = lax.broadcasted_iota(I32, (SUBLANES, s0.shape[1]), 0)
    cs = [v1[0:1, :] + v2]
    ci = [i1[0:1, :] * PEER_NKEYS + i2]
    for a in range(1, SUBLANES):
        nb = PEER_TOPK // (a + 1)
        s = v1[a:a + 1, :] + v2[0:SUBLANES, :]
        cs.append(s if nb >= SUBLANES else jnp.where(rid8 < nb, s, -jnp.inf))
        ci.append(i1[a:a + 1, :] * PEER_NKEYS + i2[0:SUBLANES, :])
    cs.append(v1[SUBLANES:, :] + v2[0:1, :])
    ci.append(i1[SUBLANES:, :] * PEER_NKEYS + i2[0:1, :])
    top_s, top_i = _topk_rows(jnp.concatenate(cs, 0), PEER_TOPK, payload=jnp.concatenate(ci, 0))
    e = jnp.exp(top_s - top_s[0:1, :])
    idx_ref[...] = top_i
    gate_ref[...] = e / jnp.sum(e, axis=0, keepdims=True)


def _peer_topk(q, keys, tm=256):
    m = q.shape[0]
    out_blk = pl.BlockSpec((None, PEER_TOPK, tm), lambda i, h: (h, 0, i))
    return pl.pallas_call(
        _peer_topk_kernel, grid=(m // tm, PEER_HEADS),
        in_specs=[pl.BlockSpec((tm, PEER_HALF), lambda i, h: (i, 2 * h)),
                  pl.BlockSpec((tm, PEER_HALF), lambda i, h: (i, 2 * h + 1)),
                  pl.BlockSpec((None, 2, PEER_NKEYS, PEER_HALF), lambda i, h: (h, 0, 0, 0))],
        out_specs=[out_blk, out_blk],
        out_shape=[jax.ShapeDtypeStruct((PEER_HEADS, PEER_TOPK, m), I32),
                   jax.ShapeDtypeStruct((PEER_HEADS, PEER_TOPK, m), F32)],
        compiler_params=_cparams(("parallel", "parallel")),
    )(q, q, keys)


def _gelu(x):
    return 0.5 * x * (1.0 + lax.erf(x * (2.0 ** -0.5)))


def _peer_gates_kernel(idx_ref, gate_ref, g_ref, *, tm, unroll):
    rid = lax.broadcasted_iota(I32, (PEER_NKEYS, PEER_SEL), 0)
    nt = (((1,), (1,)), ((), ()))

    def body(t, carry):
        for u in range(unroll):
            n = t * unroll + u
            idx = idx_ref[pl.ds(n, 1), :]
            gate = gate_ref[pl.ds(n, 1), :]
            i1 = idx // PEER_NKEYS
            i2 = idx - i1 * PEER_NKEYS
            a = jnp.where(rid == i1, gate, 0.0).astype(BF16)
            b = jnp.where(rid == i2, 1.0, 0.0).astype(BF16)
            g_ref[n] = lax.dot_general(a, b, nt, preferred_element_type=F32).astype(g_ref.dtype)
        return carry

    lax.fori_loop(0, tm // unroll, body, 0)


def _peer_gates(idx, gate, tm=128, unroll=8):
    m = idx.shape[0]
    sel = pl.BlockSpec((tm, PEER_SEL), lambda i: (i, 0))
    return pl.pallas_call(
        functools.partial(_peer_gates_kernel, tm=tm, unroll=unroll), grid=(m // tm,),
        in_specs=[sel, sel],
        out_specs=pl.BlockSpec((tm, PEER_NKEYS, PEER_NKEYS), lambda i: (i, 0, 0)),
        out_shape=jax.ShapeDtypeStruct((m, PEER_NKEYS, PEER_NKEYS), BF16),
        compiler_params=_cparams(("parallel",)),
    )(idx, gate)


def _peer_dense_kernel(h_ref, u_ref, v_ref, g_ref, oin_ref, o_ref):
    act = lax.dot_general(h_ref[...], u_ref[...], (((1,), (1,)), ((), ())), preferred_element_type=F32)
    g = g_ref[...].astype(F32).reshape(act.shape)
    wt = jnp.where(g != 0.0, _gelu(act) * g, 0.0).astype(BF16)
    o_ref[...] = oin_ref[...] + jnp.dot(wt, v_ref[...], preferred_element_type=F32)


def _peer_dense(h2, u_tab, v_tab, g, *, tm, te):
    m, d = h2.shape
    e = v_tab.shape[0]
    assert m % tm == 0 and e % te == 0
    tok = pl.BlockSpec((tm, d), lambda j, i: (i, 0))
    tab = pl.BlockSpec((te, d), lambda j, i: (j, 0), pipeline_mode=pl.Buffered(1))
    return pl.pallas_call(
        _peer_dense_kernel, grid=(e // te, m // tm),
        in_specs=[tok, tab, tab, pl.BlockSpec((tm, te // PEER_NKEYS, PEER_NKEYS), lambda j, i: (i, j, 0)), tok],
        out_specs=tok,
        out_shape=jax.ShapeDtypeStruct((m, d), F32),
        input_output_aliases={4: 0},
        compiler_params=_cparams(("arbitrary", "arbitrary")),
    )(h2, u_tab, v_tab, g, jnp.zeros((m, d), F32))


def _block_diag2(w):
    z = jnp.zeros_like(w[0])
    return jnp.concatenate([jnp.concatenate([w[0], z], 1), jnp.concatenate([z, w[1]], 1)], 0)


def _rope_swap_cols(w_rope):
    half = MLA_ROPE // 2
    return jnp.concatenate([-w_rope[..., half:], w_rope[..., :half]], -1)


def _prep_layer_weights(l, w_in, mla_w_uq, mla_w_ukv, rk_w_up, rk_a_up, rk_g_up):
    d = w_in.shape[1]
    wi = w_in[l]
    lo, hi = RW_COLS, RW_COLS + MLA_Q_RANK + MLA_KV_RANK + MLA_ROPE
    w_rw = jnp.pad(wi[:, :lo], ((0, 0), (0, RW_COLS_PAD - RW_COLS))).astype(BF16)
    w_kr = wi[:, hi - MLA_ROPE:hi]
    place = lambda w: jnp.pad(w, ((0, 0), (MLA_NOPE, LANES - MLA_NOPE - MLA_ROPE)))
    w_mla = jnp.concatenate([wi[:, lo:hi - MLA_ROPE], place(w_kr), place(_rope_swap_cols(w_kr))], 1).astype(BF16)
    w_gate = wi[:, hi:].astype(BF16)

    uq = mla_w_uq[l].reshape(MLA_Q_RANK, MLA_HEADS, MLA_QK)
    zeros = jnp.zeros((MLA_Q_RANK, MLA_HEADS, LANES - MLA_QK), F32)
    wqa = jnp.concatenate([uq, zeros], -1).reshape(MLA_Q_RANK, -1).astype(BF16)
    zn = jnp.zeros((MLA_Q_RANK, MLA_HEADS, MLA_NOPE), F32)
    wqs = jnp.concatenate([zn, _rope_swap_cols(uq[..., MLA_NOPE:]), zeros], -1).reshape(MLA_Q_RANK, -1).astype(BF16)
    ukv = mla_w_ukv[l].reshape(MLA_KV_RANK, MLA_HEADS, MLA_NOPE + MLA_V)
    wk = jnp.pad(ukv[..., :MLA_NOPE], ((0, 0), (0, 0), (0, LANES - MLA_NOPE))).reshape(MLA_KV_RANK, -1).astype(BF16)
    wv = ukv[..., MLA_NOPE:].reshape(MLA_KV_RANK, -1).astype(BF16)

    wup = _block_diag2(rk_w_up[l]).astype(BF16)
    aup = _block_diag2(rk_a_up[l]).astype(BF16)
    gup = jnp.pad(_block_diag2(rk_g_up[l]), ((0, LANES - 2 * RW_GATE_LORA), (0, 0))).astype(BF16)
    return dict(w_rw=w_rw, w_mla=w_mla, w_gate=w_gate, wqa=wqa, wqs=wqs, wk=wk, wv=wv, wup=wup, aup=aup, gup=gup)


def _rope_lane_tables(l, c):
    rows = l // GRID_W
    row = jnp.repeat(jnp.arange(rows, dtype=F32), GRID_W)
    col = jnp.tile(jnp.arange(GRID_W, dtype=F32), rows)
    n_freq = MLA_ROPE // 4
    freqs = ROPE_BASE ** (-jnp.arange(n_freq, dtype=F32) / n_freq)
    ang = jnp.concatenate([row[:, None] * freqs, col[:, None] * freqs], -1)
    cos, sin = jnp.cos(ang), jnp.sin(ang)
    tail = LANES - MLA_NOPE - MLA_ROPE
    cos_t = jnp.concatenate([jnp.ones((l, MLA_NOPE), F32), cos, cos, jnp.ones((l, tail), F32)], -1)
    sin_t = jnp.concatenate([jnp.zeros((l, MLA_NOPE), F32), sin, sin, jnp.zeros((l, tail), F32)], -1)
    cos_t = jnp.concatenate([cos_t, jnp.ones((c, LANES), F32)], 0)
    sin_t = jnp.concatenate([sin_t, jnp.zeros((c, LANES), F32)], 0)
    return cos_t, sin_t


def _peer(h2, w_q, keys, u_tab, v_tab, *, te, tm=256):
    m, d = h2.shape
    q = _mm(h2, w_q, tm=256, tn=_pick_tile(w_q.shape[1], 1024))
    idx_t, gate_t = _peer_topk(q, keys)
    idx = idx_t.transpose(2, 0, 1).reshape(m, PEER_SEL)
    gate = gate_t.transpose(2, 0, 1).reshape(m, PEER_SEL)
    g = _peer_gates(idx, gate)
    return _peer_dense(h2, u_tab, v_tab, g, tm=tm, te=te)


def _forward(x, c, ctx, c_ctx, ada_w, ada_b, norm_mix_g, w_in, gate_b, rk_mu, rk_w0, rk_w_up, rk_a0,
             rk_a_up, rk_g_up, rk_k_k, rk_k_a, rk_r_k, rk_ln_w, rk_ln_b, mla_q_norm, mla_w_uq, mla_kv_norm,
             mla_w_ukv, w_branch_a, w_branch_b, w_out, norm_ffn_g, peer_w_q, peer_keys, peer_u, peer_v,
             final_norm_g, *, flash_tq=512, flash_tk=2816, scan_tt=32, peer_te=2048):
    nb, l, d = x.shape
    cl = ctx.shape[1]
    t = l + cl
    m = nb * t
    depth = w_in.shape[0]
    dims = dict(B=nb, L=l, C=cl, T=t)
    assert nb == 2 and 4 * nb * RW_HEADS == LANES and l % 256 == 0 and cl % 256 == 0

    xa = jnp.concatenate([x, ctx], 1).reshape(m, d)
    cos_t, sin_t = _rope_lane_tables(l, cl)
    seg = jnp.arange(RW_WIDTH, dtype=I32) // RW_HEAD
    bd = (seg[:, None] == seg[None, :]).astype(BF16)
    cvec = jnp.zeros((SUBLANES, d), F32).at[:nb].set(jax.nn.silu(c)).at[nb].set(jax.nn.silu(c_ctx))

    delta, mods_prev = None, None
    out = None
    for li in range(depth):
        wl = _prep_layer_weights(li, w_in, mla_w_uq, mla_w_ukv, rk_w_up, rk_a_up, rk_g_up)
        tn_ada = _pick_tile(6 * d, 768)
        mods = _mm(cvec, ada_w[li], tm=SUBLANES, tn=tn_ada, epilogue=lambda acc, bias: acc + bias,
                   extras=[(ada_b[li][None, :], pl.BlockSpec((1, tn_ada), lambda j, i: (0, j)))])
        mods = mods[:nb + 1].reshape((nb + 1) * 6, 1, d)
        g_mix = norm_mix_g[li][None, :]
        if li == 0:
            h = _norm_mod(xa, g_mix, mods, 0, 1, dims)
        else:
            xa, h = _resid_norm_mod(xa, delta, mods_prev, 5, g_mix, mods, 0, 1, dims)

        p_rw = _mm(h, wl["w_rw"], tm=256, tn=_pick_tile(RW_COLS_PAD, 1152))
        p_mla = _mm(h, wl["w_mla"], tm=256, tn=MLA_P_COLS)
        p_gate = _mm(h, wl["w_gate"], tm=256, tn=1024)

        mu = jnp.pad(rk_mu[li], (0, RW_COLS_PAD - RW_COLS))[None, :]
        feats = _rwfeat(p_rw.reshape(nb, t, RW_COLS_PAD), mu, rk_w0[li], wl["wup"], rk_a0[li], wl["aup"], wl["gup"],
                        rk_k_k[li][None, :], rk_k_a[li][None, :], rk_r_k[li].reshape(1, RW_WIDTH), bd, dims)
        r, nkk, v, dec0, dec1, b0, b1, ke0, ke1, bon0, bon1, g0, g1 = feats
        ks = (_to_scan_k(dec0, dec1), _to_scan_k(b0, b1), _to_scan_k(ke0, ke1), _to_scan_k(r, r),
              _to_scan_k(*_next_step_rows(nkk, l)))
        ynf, ynr = _scan(ks, _to_scan_v(v), l, tt=scan_tt)
        yn0, yn1 = _from_scan_y(ynf, ynr, nb)
        flat = lambda z: z.reshape(m, RW_WIDTH)
        ya = _readout(flat(yn0), flat(yn1), flat(bon0), flat(bon1), flat(g0), flat(g1),
                      rk_ln_w[li][None, :], rk_ln_b[li][None, :])

        q, k, vv = _mla_prep(p_mla, cos_t, sin_t, mla_q_norm[li][None, :], mla_kv_norm[li][None, :],
                             wl["wqa"], wl["wqs"], wl["wk"], wl["wv"], dims)
        q3, k3, v3 = (z.reshape(nb, t, -1) for z in (q, k, vv))
        yb = _flash(q3, k3, v3, tq=flash_tq, tk=flash_tk, q_off=0, nq=l // flash_tq, k_off=0, nk=t // flash_tk)
        yb = _flash(q3, k3, v3, tq=cl, tk=cl, q_off=l // cl, nq=1, k_off=l // cl, nk=1, prev_out=yb)
        yb = yb.reshape(m, MLA_HEADS * MLA_V)

        tpb, lt = t // 256, l // 256
        br_a = _mm(ya, w_branch_a[li].astype(BF16), tm=256, tn=1024)
        gb = gate_b[li]
        merged = _mm(
            yb, w_branch_b[li].astype(BF16), tm=256, tn=1024, out_dtype=BF16,
            epilogue=lambda acc, ga, gbb, ba, bb, a: _sigmoid(ga + ba) * a + _sigmoid(gbb + bb) * acc,
            extras=[(p_gate, pl.BlockSpec((256, 1024), lambda j, i: (i, j))),
                    (p_gate, pl.BlockSpec((256, 1024), lambda j, i: (i, j + d // 1024))),
                    (gb[0][None, :], pl.BlockSpec((1, 1024), lambda j, i: (0, j))),
                    (gb[1][None, :], pl.BlockSpec((1, 1024), lambda j, i: (0, j))),
                    (br_a, pl.BlockSpec((256, 1024), lambda j, i: (i, j)))])
        x1 = _mm(merged, w_out[li].astype(BF16), tm=256, tn=1024,
                 epilogue=lambda acc, xr, gt: xr + gt * acc,
                 extras=[(xa, pl.BlockSpec((256, 1024), lambda j, i: (i, j))),
                         (mods, pl.BlockSpec((None, 1, 1024), lambda j, i: (_mod_row(i, tpb, lt, 2, nb), 0, j)))])

        h2 = _norm_mod(x1, norm_ffn_g[li][None, :], mods, 3, 4, dims)
        delta = _peer(h2, peer_w_q[li].astype(BF16), peer_keys[li], peer_u[li].astype(BF16),
                      peer_v[li].astype(BF16), te=peer_te)
        xa, mods_prev = x1, mods

    out = _resid_final_norm(xa.reshape(nb, t, d), delta.reshape(nb, t, d), mods_prev, 5,
                            final_norm_g[None, :], dims)
    return out


def kernel(x, c, ctx, c_ctx, ada_w, ada_b, norm_mix_g, w_in, gate_b, rk_mu, rk_w0, rk_w_up, rk_a0, rk_a_up, rk_g_up, rk_k_k, rk_k_a, rk_r_k, rk_ln_w, rk_ln_b, mla_q_norm, mla_w_uq, mla_kv_norm, mla_w_ukv, w_branch_a, w_branch_b, w_out, norm_ffn_g, peer_w_q, peer_keys, peer_u, peer_v, final_norm_g):
    return _forward(x, c, ctx, c_ctx, ada_w, ada_b, norm_mix_g, w_in, gate_b, rk_mu, rk_w0, rk_w_up, rk_a0,
                    rk_a_up, rk_g_up, rk_k_k, rk_k_a, rk_r_k, rk_ln_w, rk_ln_b, mla_q_norm, mla_w_uq,
                    mla_kv_norm, mla_w_ukv, w_branch_a, w_branch_b, w_out, norm_ffn_g, peer_w_q, peer_keys,
                    peer_u, peer_v, final_norm_g)
```

```python
import functools

import jax
import jax.numpy as jnp
from jax import lax
from jax.experimental import pallas as pl
from jax.experimental.pallas import tpu as pltpu

F32 = jnp.float32
BF16 = jnp.bfloat16
I32 = jnp.int32

NORM_EPS = 1e-6
GRID_W = 64
ROPE_BASE = 10000.0

RW_HEADS = 16
RW_HEAD = 64
RW_WIDTH = RW_HEADS * RW_HEAD
RW_DECAY_LORA = 64
RW_A_LORA = 64
RW_GATE_LORA = 32
RW_GN_EPS = 64e-5
RW_COLS = 3 * RW_WIDTH + 2 * (RW_DECAY_LORA + RW_A_LORA + RW_GATE_LORA)
RW_COLS_PAD = 3456

MLA_HEADS = 16
MLA_NOPE = 64
MLA_ROPE = 32
MLA_QK = MLA_NOPE + MLA_ROPE
MLA_V = 64
MLA_Q_RANK = 512
MLA_KV_RANK = 256
MLA_HEAD_PAD = 128
MLA_P_COLS = MLA_Q_RANK + MLA_KV_RANK + 2 * MLA_HEAD_PAD

PEER_HEADS = 8
PEER_NKEYS = 128
PEER_HALF = 128
PEER_TOPK = 16
PEER_SEL = PEER_HEADS * PEER_TOPK

LANES = 128
SUBLANES = 8
VMEM_LIMIT = 56 * 1024 * 1024


def _cparams(sem):
    return pltpu.CompilerParams(dimension_semantics=sem, vmem_limit_bytes=VMEM_LIMIT)


def _pick_tile(n, cap, mult=LANES):
    best = None
    t = mult
    while t <= min(n, cap):
        if n % t == 0:
            best = t
        t += mult
    assert best is not None, (n, cap)
    return best


def _mm_kernel(*refs, n_extra, epilogue):
    a_ref, w_ref = refs[0], refs[1]
    extra = refs[2:2 + n_extra]
    o_ref = refs[2 + n_extra]
    acc = jnp.dot(a_ref[...].astype(BF16), w_ref[...].astype(BF16), preferred_element_type=F32)
    if epilogue is not None:
        acc = epilogue(acc, *[e[...] for e in extra])
    o_ref[...] = acc.astype(o_ref.dtype)


def _mm(a, w, *, tm, tn, out_dtype=F32, epilogue=None, extras=()):
    m, k = a.shape
    k2, n = w.shape
    assert k == k2 and m % tm == 0 and n % tn == 0, (a.shape, w.shape, tm, tn)
    in_specs = [pl.BlockSpec((tm, k), lambda j, i: (i, 0)),
                pl.BlockSpec((k, tn), lambda j, i: (0, j))] + [s for _, s in extras]
    return pl.pallas_call(
        functools.partial(_mm_kernel, n_extra=len(extras), epilogue=epilogue),
        grid=(n // tn, m // tm),
        in_specs=in_specs,
        out_specs=pl.BlockSpec((tm, tn), lambda j, i: (i, j)),
        out_shape=jax.ShapeDtypeStruct((m, n), out_dtype),
        compiler_params=_cparams(("parallel", "parallel")),
    )(a, w, *[x for x, _ in extras])


def _sigmoid(x):
    return 1.0 / (1.0 + jnp.exp(-x))


def _mod_row(i, tiles_per_batch, lat_tiles, part, n_batch):
    which = jnp.where(i % tiles_per_batch >= lat_tiles, n_batch, i // tiles_per_batch)
    return which * 6 + part


def _rms(x, g):
    return x * lax.rsqrt(jnp.mean(x * x, -1, keepdims=True) + NORM_EPS) * g


def _norm_mod_kernel(x_ref, g_ref, sh_ref, sc_ref, h_ref):
    y = _rms(x_ref[...], g_ref[...])
    h_ref[...] = (y * (1.0 + sc_ref[...]) + sh_ref[...]).astype(h_ref.dtype)


def _resid_norm_mod_kernel(x_ref, d_ref, gt_ref, g_ref, sh_ref, sc_ref, xo_ref, h_ref):
    x = x_ref[...] + gt_ref[...] * d_ref[...]
    xo_ref[...] = x
    y = _rms(x, g_ref[...])
    h_ref[...] = (y * (1.0 + sc_ref[...]) + sh_ref[...]).astype(h_ref.dtype)


def _resid_final_norm_kernel(x_ref, d_ref, gt_ref, g_ref, o_ref):
    x = x_ref[...] + gt_ref[...] * d_ref[...]
    o_ref[...] = _rms(x, g_ref[...])


def _mod_spec(d, part, dims, tm):
    tpb, lt, nb = dims["T"] // tm, dims["L"] // tm, dims["B"]
    return pl.BlockSpec((None, 1, d), lambda i: (_mod_row(i, tpb, lt, part, nb), 0, 0))


def _norm_mod(x, g, mods, part_sh, part_sc, dims, tm=256):
    m, d = x.shape
    row = pl.BlockSpec((tm, d), lambda i: (i, 0))
    return pl.pallas_call(
        _norm_mod_kernel, grid=(m // tm,),
        in_specs=[row, pl.BlockSpec((1, d), lambda i: (0, 0)),
                  _mod_spec(d, part_sh, dims, tm), _mod_spec(d, part_sc, dims, tm)],
        out_specs=row, out_shape=jax.ShapeDtypeStruct((m, d), BF16),
        compiler_params=_cparams(("parallel",)),
    )(x, g, mods, mods)


def _resid_norm_mod(x, delta, mods_gt, part_gt, g, mods, part_sh, part_sc, dims, tm=256):
    m, d = x.shape
    row = pl.BlockSpec((tm, d), lambda i: (i, 0))
    return pl.pallas_call(
        _resid_norm_mod_kernel, grid=(m // tm,),
        in_specs=[row, row, _mod_spec(d, part_gt, dims, tm), pl.BlockSpec((1, d), lambda i: (0, 0)),
                  _mod_spec(d, part_sh, dims, tm), _mod_spec(d, part_sc, dims, tm)],
        out_specs=[row, row],
        out_shape=[jax.ShapeDtypeStruct((m, d), F32), jax.ShapeDtypeStruct((m, d), BF16)],
        compiler_params=_cparams(("parallel",)),
    )(x, delta, mods_gt, g, mods, mods)


def _resid_final_norm(x3, delta3, mods, part_gt, g, dims, tm=256):
    b, t, d = x3.shape
    lt = dims["L"] // tm
    blk = pl.BlockSpec((None, tm, d), lambda bi, i: (bi, i, 0))
    return pl.pallas_call(
        _resid_final_norm_kernel, grid=(b, lt),
        in_specs=[blk, blk, pl.BlockSpec((None, 1, d), lambda bi, i: (bi * 6 + part_gt, 0, 0)),
                  pl.BlockSpec((1, d), lambda bi, i: (0, 0))],
        out_specs=blk, out_shape=jax.ShapeDtypeStruct((b, dims["L"], d), F32),
        compiler_params=_cparams(("parallel", "parallel")),
    )(x3, delta3, mods, g)


def _segsum(x, bd):
    hi = x.astype(BF16)
    lo = (x - hi.astype(F32)).astype(BF16)
    return (jnp.dot(hi, bd, preferred_element_type=F32) + jnp.dot(lo, bd, preferred_element_type=F32))


def _rwfeat_kernel(p_ref, pp_ref, pn_ref, mu_ref, w0_ref, wup_ref, a0_ref, aup_ref, gup_ref, kk_ref, ka_ref,
                   rk_ref, bd_ref,
                   r_ref, nkk_ref, v_ref, dec0_ref, dec1_ref, b0_ref, b1_ref, ke0_ref, ke1_ref,
                   bon0_ref, bon1_ref, g0_ref, g1_ref, *, tm, lat_tiles, all_tiles):
    i = pl.program_id(1)
    p = p_ref[...]
    prev_ok = jnp.logical_and(i != 0, i != lat_tiles)
    next_ok = jnp.logical_and(i != lat_tiles - 1, i != all_tiles - 1)
    prow = jnp.where(prev_ok, pp_ref[SUBLANES - 1:SUBLANES, :], 0.0)
    nrow = jnp.where(next_ok, pn_ref[0:1, :], 0.0)
    rid = lax.broadcasted_iota(I32, (tm, 1), 0)
    prev = jnp.where(rid == 0, prow, pltpu.roll(p, 1, 0))
    nxt = jnp.where(rid == tm - 1, nrow, pltpu.roll(p, tm - 1, 0))
    ps = p + (0.5 * (prev + nxt) - p) * mu_ref[...]

    w_ = RW_WIDTH
    r = ps[:, 0:w_]
    k = ps[:, w_:2 * w_]
    v = ps[:, 2 * w_:3 * w_]
    wd = ps[:, 3 * w_:3 * w_ + LANES]
    ad = ps[:, 3 * w_ + LANES:3 * w_ + 2 * LANES]
    gd = ps[:, 3 * w_ + 2 * LANES:3 * w_ + 3 * LANES]
    bd = bd_ref[...]

    lw = jnp.dot(jnp.tanh(wd).astype(BF16), wup_ref[...], preferred_element_type=F32)
    la = jnp.dot(ad.astype(BF16), aup_ref[...], preferred_element_type=F32)
    lg = jnp.dot(_sigmoid(gd).astype(BF16), gup_ref[...], preferred_element_type=F32)

    kk = k * kk_ref[...]
    kk = kk * lax.rsqrt(_segsum(kk * kk, bd) + 1e-12)
    r_ref[...] = r
    nkk_ref[...] = -kk
    v_ref[...] = v
    rk = rk_ref[...]
    ka = ka_ref[...]
    outs = ((dec0_ref, b0_ref, ke0_ref, bon0_ref, g0_ref), (dec1_ref, b1_ref, ke1_ref, bon1_ref, g1_ref))
    for d in range(2):
        dec_ref, b_ref, ke_ref, bon_ref, g_ref = outs[d]
        z = w0_ref[d:d + 1, :] + lw[:, d * w_:(d + 1) * w_]
        nz = -z
        softplus = jnp.maximum(nz, 0.0) + jnp.log(1.0 + jnp.exp(-jnp.abs(nz)))
        wlog = -softplus - 0.5
        dec_ref[...] = jnp.exp(-jnp.exp(wlog))
        a = _sigmoid(a0_ref[d:d + 1, :] + la[:, d * w_:(d + 1) * w_])
        b_ref[...] = kk * a
        ke = k * (1.0 + (a - 1.0) * ka)
        ke_ref[...] = ke
        bon_ref[...] = _segsum(r * ke * rk, bd) * v
        g_ref[...] = lg[:, d * w_:(d + 1) * w_]


def _rwfeat(p3, mu, w0, wup, a0, aup, gup, k_k, k_a, r_k, bd, dims, tm=128):
    b, t, wc = p3.shape
    lt, at = dims["L"] // tm, t // tm
    hb = tm // SUBLANES
    nblk8 = t // SUBLANES
    w_ = RW_WIDTH
    full = lambda arr: pl.BlockSpec(arr.shape, lambda bi, i: (0,) * arr.ndim)
    out_blk = pl.BlockSpec((None, tm, w_), lambda bi, i: (bi, i, 0))
    n_out = 13
    return pl.pallas_call(
        functools.partial(_rwfeat_kernel, tm=tm, lat_tiles=lt, all_tiles=at),
        grid=(b, at),
        in_specs=[pl.BlockSpec((None, tm, wc), lambda bi, i: (bi, i, 0)),
                  pl.BlockSpec((None, SUBLANES, wc), lambda bi, i: (bi, jnp.maximum(i * hb - 1, 0), 0)),
                  pl.BlockSpec((None, SUBLANES, wc), lambda bi, i: (bi, jnp.minimum((i + 1) * hb, nblk8 - 1), 0)),
                  full(mu), full(w0), full(wup), full(a0), full(aup), full(gup), full(k_k), full(k_a),
                  full(r_k), full(bd)],
        out_specs=[out_blk] * n_out,
        out_shape=[jax.ShapeDtypeStruct((b, t, w_), F32)] * n_out,
        compiler_params=_cparams(("parallel", "parallel")),
    )(p3, p3, p3, mu, w0, wup, a0, aup, gup, k_k, k_a, r_k, bd)


SCAN_KH = RW_HEAD // 2


SCAN_NK = 5


def _scan_kernel(*refs, tt):
    kf_refs, kr_refs = refs[:SCAN_NK], refs[SCAN_NK:2 * SCAN_NK]
    vf_ref, vr_ref, yf_ref, yr_ref, s_ref, sa_ref, mrg_ref, acc_ref, yraw_ref = refs[2 * SCAN_NK:]

    @pl.when(pl.program_id(0) == 0)
    def _():
        s_ref[...] = jnp.zeros_like(s_ref)
        sa_ref[...] = jnp.zeros_like(sa_ref)

    lane = lax.broadcasted_iota(I32, (RW_HEAD, LANES), 1)
    lanek = lax.broadcasted_iota(I32, (SCAN_KH, LANES), 1)
    quarter = LANES // 4
    fwdk = (lanek // quarter) % 2 == 0
    laneh = lax.broadcasted_iota(I32, (RW_HEAD // 2, LANES), 1)
    fwdh = (laneh // quarter) % 2 == 0

    half = RW_HEAD // 2

    def merge_rows(t, slot):
        for q in range(SCAN_NK):
            mrg_ref[slot, q] = jnp.where(fwdk, kf_refs[q][t], kr_refs[q][tt - 1 - t])

    merge_rows(0, 0)

    def one_step(t):
        tr = tt - 1 - t
        slot = t % 2

        def half_pass(hv, carry):
            sl = pl.ds(pl.multiple_of(hv * half, half), half)
            vt = jnp.where(fwdh, vf_ref[t, sl, :], vr_ref[tr, sl, :])
            sah = sa_ref[sl, :]
            yacc = jnp.zeros((half, LANES), F32)
            sn = jnp.zeros((half, LANES), F32)
            for k in range(SCAN_KH):
                row = lambda q: mrg_ref[slot, q, pl.ds(k, 1), :]
                w, b, ke, r, an = row(0), row(1), row(2), row(3), row(4)
                s = s_ref[k, sl, :] * w + (sah * b + vt * ke)
                s_ref[k, sl, :] = s
                yacc = yacc + s * r
                sn = sn + s * an
            acc_ref[0, sl, :] = yacc
            acc_ref[1, sl, :] = sn
            return carry

        lax.fori_loop(0, 2, half_pass, 0)
        sn = acc_ref[1]
        sa_ref[...] = sn + pltpu.roll(sn, LANES // 2, 1)
        yraw_ref[t] = acc_ref[0]
        merge_rows(jnp.minimum(t + 1, tt - 1), 1 - slot)

    def step(t, carry):
        one_step(t)
        return carry

    lax.fori_loop(0, tt, step, 0)

    def normed(t):
        yacc = yraw_ref[t]
        y = yacc + pltpu.roll(yacc, LANES // 2, 1)
        mean = jnp.mean(y, axis=0, keepdims=True)
        dlt = y - mean
        var = jnp.mean(dlt * dlt, axis=0, keepdims=True)
        return dlt * lax.rsqrt(var + RW_GN_EPS)

    low = lane < LANES // 2
    for j in range(tt // 2):
        y0, y1 = normed(2 * j), normed(2 * j + 1)
        yf_ref[j] = jnp.where(low, y0, y1)
        yr_ref[tt // 2 - 1 - j] = jnp.where(low, y1, y0)


def _scan(ks, vv, l, tt=32):
    t = vv.shape[0]
    assert len(ks) == SCAN_NK and t % tt == 0 and tt % 2 == 0 and l % tt == 0
    nb, lb = t // tt, l // tt
    cb = nb - lb
    fwd = lambda i: jnp.where(i < cb, lb + i, i - cb)
    rev = lambda i: jnp.where(i < cb, lb + (cb - 1 - i), lb - 1 - (i - cb))
    kv_spec = lambda f: pl.BlockSpec((tt, SCAN_KH, LANES), lambda i: (f(i), 0, 0))
    v_spec = lambda f: pl.BlockSpec((tt, RW_HEAD, LANES), lambda i: (f(i), 0, 0))
    y_spec = lambda f: pl.BlockSpec((tt // 2, RW_HEAD, LANES), lambda i: (f(i), 0, 0))
    y_shape = jax.ShapeDtypeStruct((t // 2, RW_HEAD, LANES), F32)
    return pl.pallas_call(
        functools.partial(_scan_kernel, tt=tt),
        grid=(nb,),
        in_specs=[kv_spec(fwd)] * SCAN_NK + [kv_spec(rev)] * SCAN_NK + [v_spec(fwd), v_spec(rev)],
        out_specs=[y_spec(fwd), y_spec(rev)],
        out_shape=[y_shape, y_shape],
        scratch_shapes=[pltpu.VMEM((SCAN_KH, RW_HEAD, LANES), F32), pltpu.VMEM((RW_HEAD, LANES), F32),
                        pltpu.VMEM((2, SCAN_NK, SCAN_KH, LANES), F32), pltpu.VMEM((2, RW_HEAD, LANES), F32),
                        pltpu.VMEM((tt, RW_HEAD, LANES), F32)],
        compiler_params=_cparams(("arbitrary",)),
    )(*ks, *ks, vv, vv)


def _to_scan_k(x0, x1):
    b, t, _ = x0.shape
    s = jnp.stack([x0, x1], 0).reshape(2, b, t, RW_HEADS, 2, SCAN_KH)
    return s.transpose(2, 5, 4, 0, 1, 3).reshape(t, SCAN_KH, 4 * b * RW_HEADS)


def _to_scan_v(x):
    b, t, _ = x.shape
    s = x.reshape(b, t, RW_HEADS, RW_HEAD).transpose(1, 3, 0, 2).reshape(t, RW_HEAD, b * RW_HEADS)
    return jnp.concatenate([s, s, s, s], -1)


def _next_step_rows(x, l):
    lat, ctx = x[:, :l], x[:, l:]
    z = jnp.zeros_like(x[:, :1])
    nxt_f = jnp.concatenate([lat[:, 1:], z, ctx[:, 1:], lat[:, :1]], 1)
    nxt_r = jnp.concatenate([z, lat[:, :-1], lat[:, l - 1:l], ctx[:, :-1]], 1)
    return nxt_f, nxt_r


def _from_scan_y(yf, yr, b):
    t2 = yf.shape[0]
    q = b * RW_HEADS
    outs = []
    for d, y in enumerate((yf, yr)):
        s = y.reshape(t2, RW_HEAD, 2, 2 * q)[..., d * q:(d + 1) * q]
        outs.append(s.reshape(t2, RW_HEAD, 2, b, RW_HEADS).transpose(3, 0, 2, 4, 1).reshape(b, 2 * t2, RW_WIDTH))
    return outs[0], outs[1]


def _readout_kernel(y0_ref, y1_ref, bon0_ref, bon1_ref, g0_ref, g1_ref, lw_ref, lb_ref, o_ref):
    lw, lb = lw_ref[...], lb_ref[...]
    o = (y0_ref[...] * lw + lb + bon0_ref[...]) * g0_ref[...]
    o = o + (y1_ref[...] * lw + lb + bon1_ref[...]) * g1_ref[...]
    o_ref[...] = o.astype(o_ref.dtype)


def _readout(y0, y1, bon0, bon1, g0, g1, ln_w, ln_b, tm=256):
    m, w_ = y0.shape
    row = pl.BlockSpec((tm, w_), lambda i: (i, 0))
    vec = pl.BlockSpec((1, w_), lambda i: (0, 0))
    return pl.pallas_call(
        _readout_kernel, grid=(m // tm,), in_specs=[row] * 6 + [vec, vec], out_specs=row,
        out_shape=jax.ShapeDtypeStruct((m, w_), BF16), compiler_params=_cparams(("parallel",)),
    )(y0, y1, bon0, bon1, g0, g1, ln_w, ln_b)


def _mla_prep_kernel(p_ref, c_ref, s_ref, qn_ref, kn_ref, wqa_ref, wqs_ref, wk_ref, wv_ref, q_ref, k_ref, v_ref):
    p = p_ref[...]
    cos, sin = c_ref[...], s_ref[...]
    qc = _rms(p[:, :MLA_Q_RANK], qn_ref[...]).astype(BF16)
    kvc = _rms(p[:, MLA_Q_RANK:MLA_Q_RANK + MLA_KV_RANK], kn_ref[...]).astype(BF16)
    kr = p[:, MLA_Q_RANK + MLA_KV_RANK:MLA_Q_RANK + MLA_KV_RANK + LANES]
    krs = p[:, MLA_Q_RANK + MLA_KV_RANK + LANES:MLA_Q_RANK + MLA_KV_RANK + 2 * LANES]
    krope = kr * cos + krs * sin
    qa = jnp.dot(qc, wqa_ref[...], preferred_element_type=F32)
    qs = jnp.dot(qc, wqs_ref[...], preferred_element_type=F32)
    kk = jnp.dot(kvc, wk_ref[...], preferred_element_type=F32)
    v_ref[...] = jnp.dot(kvc, wv_ref[...], preferred_element_type=F32).astype(v_ref.dtype)
    scale = MLA_QK ** -0.5
    for h in range(MLA_HEADS):
        sl = slice(h * LANES, (h + 1) * LANES)
        q_ref[:, sl] = ((qa[:, sl] * cos + qs[:, sl] * sin) * scale).astype(q_ref.dtype)
        k_ref[:, sl] = (kk[:, sl] + krope).astype(k_ref.dtype)


def _mla_prep(p, cos_t, sin_t, q_norm, kv_norm, wqa, wqs, wk, wv, dims, tm=256):
    m, pc = p.shape
    tpb = dims["T"] // tm
    full = lambda arr: pl.BlockSpec(arr.shape, lambda i: (0,) * arr.ndim)
    tab = pl.BlockSpec((tm, LANES), lambda i: (i % tpb, 0))
    hw = MLA_HEADS * LANES
    return pl.pallas_call(
        _mla_prep_kernel, grid=(m // tm,),
        in_specs=[pl.BlockSpec((tm, pc), lambda i: (i, 0)), tab, tab, full(q_norm), full(kv_norm),
                  full(wqa), full(wqs), full(wk), full(wv)],
        out_specs=[pl.BlockSpec((tm, hw), lambda i: (i, 0)), pl.BlockSpec((tm, hw), lambda i: (i, 0)),
                   pl.BlockSpec((tm, MLA_HEADS * MLA_V), lambda i: (i, 0))],
        out_shape=[jax.ShapeDtypeStruct((m, hw), BF16), jax.ShapeDtypeStruct((m, hw), BF16),
                   jax.ShapeDtypeStruct((m, MLA_HEADS * MLA_V), BF16)],
        compiler_params=_cparams(("parallel",)),
    )(p, cos_t, sin_t, q_norm, kv_norm, wqa, wqs, wk, wv)


def _flash_kernel(q_ref, k_ref, v_ref, *rest, nk):
    o_ref = rest[-5]
    state = (rest[-4:-2], rest[-2:])
    ki = pl.program_id(3)

    @pl.when(ki == 0)
    def _():
        for m_ref, acc_ref in state:
            m_ref[...] = jnp.full_like(m_ref, -jnp.inf)
            acc_ref[...] = jnp.zeros_like(acc_ref)

    v = v_ref[...]
    vlane = lax.broadcasted_iota(I32, v.shape, 1)
    ones = jnp.ones_like(v)
    for hh, (m_ref, acc_ref) in enumerate(state):
        own = (vlane < MLA_V) if hh == 0 else (vlane >= MLA_V)
        q = q_ref[:, hh * LANES:(hh + 1) * LANES]
        k = k_ref[:, hh * LANES:(hh + 1) * LANES]
        s = lax.dot_general(q, k, (((1,), (1,)), ((), ())), preferred_element_type=F32)
        m_prev = m_ref[...]
        m_new = jnp.maximum(m_prev, jnp.max(s, -1, keepdims=True))
        alpha = jnp.exp(m_prev - m_new)
        pr = jnp.exp(s - m_new).astype(BF16)
        acc_ref[...] = alpha * acc_ref[...] + jnp.dot(pr, jnp.where(own, v, ones), preferred_element_type=F32)
        m_ref[...] = m_new

    @pl.when(ki == nk - 1)
    def _():
        (_, acc0), (_, acc1) = state
        a0, a1 = acc0[...], acc1[...]
        lane = lax.broadcasted_iota(I32, a0.shape, 1)
        o = jnp.where(lane < MLA_V, a0 / pltpu.roll(a0, MLA_V, 1), a1 / pltpu.roll(a1, MLA_V, 1))
        o_ref[...] = o.astype(o_ref.dtype)


def _flash(q3, k3, v3, *, tq, tk, q_off, nq, k_off, nk, prev_out=None):
    b, t, _ = q3.shape
    hp = MLA_HEADS // 2
    in_specs = [pl.BlockSpec((None, tq, 2 * LANES), lambda bi, h, qi, ki: (bi, q_off + qi, h)),
                pl.BlockSpec((None, tk, 2 * LANES), lambda bi, h, qi, ki: (bi, k_off + ki, h)),
                pl.BlockSpec((None, tk, LANES), lambda bi, h, qi, ki: (bi, k_off + ki, h))]
    args = [q3, k3, v3]
    aliases = {}
    if prev_out is not None:
        in_specs.append(pl.BlockSpec(memory_space=pl.ANY))
        args.append(prev_out)
        aliases = {3: 0}
    return pl.pallas_call(
        functools.partial(_flash_kernel, nk=nk),
        grid=(b, hp, nq, nk),
        in_specs=in_specs,
        out_specs=pl.BlockSpec((None, tq, LANES), lambda bi, h, qi, ki: (bi, q_off + qi, h)),
        out_shape=jax.ShapeDtypeStruct((b, t, MLA_HEADS * MLA_V), BF16),
        scratch_shapes=[pltpu.VMEM((tq, 1), F32), pltpu.VMEM((tq, LANES), F32)] * 2,
        input_output_aliases=aliases,
        compiler_params=_cparams(("parallel", "parallel", "parallel", "arbitrary")),
    )(*args)


def _topk_rows(s, kk, payload=None):
    nrow = s.shape[0]
    rid = lax.broadcasted_iota(I32, s.shape, 0)
    vals, sel = [], []
    for _ in range(kk):
        m = jnp.max(s, axis=0, keepdims=True)
        pos = jnp.min(jnp.where(s == m, rid, nrow), axis=0, keepdims=True)
        hit = rid == pos
        vals.append(m)
        sel.append(pos if payload is None else jnp.max(jnp.where(hit, payload, -1), axis=0, keepdims=True))
        s = jnp.where(hit, -jnp.inf, s)
    return jnp.concatenate(vals, 0), jnp.concatenate(sel, 0)


def _peer_topk_kernel(q0_ref, q1_ref, keys_ref, idx_ref, gate_ref):
    nt = (((1,), (1,)), ((), ()))
    s0 = lax.dot_general(keys_ref[0].astype(BF16), q0_ref[...].astype(BF16), nt, preferred_element_type=F32)
    s1 = lax.dot_general(keys_ref[1].astype(BF16), q1_ref[...].astype(BF16), nt, preferred_element_type=F32)
    v1, i1 = _topk_rows(s0, PEER_TOPK)
    v2, i2 = _topk_rows(s1, PEER_TOPK)
    assert PEER_TOPK == 2 * SUBLANES
    rid8 = lax.broadcasted_iota(I32, (SUBLANES, s0.shape[1]), 0)
    cs = [v1[0:1, :] + v2]
    ci = [i1[0:1, :] * PEER_NKEYS + i2]
    for a in range(1, SUBLANES):
        nb = PEER_TOPK // (a + 1)
        s = v1[a:a + 1, :] + v2[0:SUBLANES, :]
        cs.append(s if nb >= SUBLANES else jnp.where(rid8 < nb, s, -jnp.inf))
        ci.append(i1[a:a + 1, :] * PEER_NKEYS + i2[0:SUBLANES, :])
    cs.append(v1[SUBLANES:, :] + v2[0:1, :])
    ci.append(i1[SUBLANES:, :] * PEER_NKEYS + i2[0:1, :])
    top_s, top_i = _topk_rows(jnp.concatenate(cs, 0), PEER_TOPK, payload=jnp.concatenate(ci, 0))
    e = jnp.exp(top_s - top_s[0:1, :])
    idx_ref[...] = top_i
    gate_ref[...] = e / jnp.sum(e, axis=0, keepdims=True)


def _peer_topk(q, keys, tm=256):
    m = q.shape[0]
    out_blk = pl.BlockSpec((None, PEER_TOPK, tm), lambda i, h: (h, 0, i))
    return pl.pallas_call(
        _peer_topk_kernel, grid=(m // tm, PEER_HEADS),
        in_specs=[pl.BlockSpec((tm, PEER_HALF), lambda i, h: (i, 2 * h)),
                  pl.BlockSpec((tm, PEER_HALF), lambda i, h: (i, 2 * h + 1)),
                  pl.BlockSpec((None, 2, PEER_NKEYS, PEER_HALF), lambda i, h: (h, 0, 0, 0))],
        out_specs=[out_blk, out_blk],
        out_shape=[jax.ShapeDtypeStruct((PEER_HEADS, PEER_TOPK, m), I32),
                   jax.ShapeDtypeStruct((PEER_HEADS, PEER_TOPK, m), F32)],
        compiler_params=_cparams(("parallel", "parallel")),
    )(q, q, keys)


def _gelu(x):
    return 0.5 * x * (1.0 + lax.erf(x * (2.0 ** -0.5)))


def _peer_gates_kernel(idx_ref, gate_ref, g_ref, *, tm, unroll):
    rid = lax.broadcasted_iota(I32, (PEER_NKEYS, PEER_SEL), 0)
    nt = (((1,), (1,)), ((), ()))

    def body(t, carry):
        for u in range(unroll):
            n = t * unroll + u
            idx = idx_ref[pl.ds(n, 1), :]
            gate = gate_ref[pl.ds(n, 1), :]
            i1 = idx // PEER_NKEYS
            i2 = idx - i1 * PEER_NKEYS
            a = jnp.where(rid == i1, gate, 0.0).astype(BF16)
            b = jnp.where(rid == i2, 1.0, 0.0).astype(BF16)
            g_ref[n] = lax.dot_general(a, b, nt, preferred_element_type=F32).astype(g_ref.dtype)
        return carry

    lax.fori_loop(0, tm // unroll, body, 0)


def _peer_gates(idx, gate, tm=128, unroll=16):
    m = idx.shape[0]
    sel = pl.BlockSpec((tm, PEER_SEL), lambda i: (i, 0))
    return pl.pallas_call(
        functools.partial(_peer_gates_kernel, tm=tm, unroll=unroll), grid=(m // tm,),
        in_specs=[sel, sel],
        out_specs=pl.BlockSpec((tm, PEER_NKEYS, PEER_NKEYS), lambda i: (i, 0, 0)),
        out_shape=jax.ShapeDtypeStruct((m, PEER_NKEYS, PEER_NKEYS), BF16),
        compiler_params=_cparams(("parallel",)),
    )(idx, gate)


def _peer_dense_kernel(h_ref, u_ref, v_ref, g_ref, oin_ref, o_ref):
    act = lax.dot_general(h_ref[...], u_ref[...], (((1,), (1,)), ((), ())), preferred_element_type=F32)
    g = g_ref[...].astype(F32).reshape(act.shape)
    wt = jnp.where(g != 0.0, _gelu(act) * g, 0.0).astype(BF16)
    o_ref[...] = oin_ref[...] + jnp.dot(wt, v_ref[...], preferred_element_type=F32)


def _peer_dense(h2, u_tab, v_tab, g, *, tm, te):
    m, d = h2.shape
    e = v_tab.shape[0]
    assert m % tm == 0 and e % te == 0
    tok = pl.BlockSpec((tm, d), lambda j, i: (i, 0))
    tab = pl.BlockSpec((te, d), lambda j, i: (j, 0), pipeline_mode=pl.Buffered(1))
    return pl.pallas_call(
        _peer_dense_kernel, grid=(e // te, m // tm),
        in_specs=[tok, tab, tab, pl.BlockSpec((tm, te // PEER_NKEYS, PEER_NKEYS), lambda j, i: (i, j, 0)), tok],
        out_specs=tok,
        out_shape=jax.ShapeDtypeStruct((m, d), F32),
        input_output_aliases={4: 0},
        compiler_params=_cparams(("arbitrary", "arbitrary")),
    )(h2, u_tab, v_tab, g, jnp.zeros((m, d), F32))


def _block_diag2(w):
    z = jnp.zeros_like(w[0])
    return jnp.concatenate([jnp.concatenate([w[0], z], 1), jnp.concatenate([z, w[1]], 1)], 0)


def _rope_swap_cols(w_rope):
    half = MLA_ROPE // 2
    return jnp.concatenate([-w_rope[..., half:], w_rope[..., :half]], -1)


def _prep_layer_weights(l, w_in, mla_w_uq, mla_w_ukv, rk_w_up, rk_a_up, rk_g_up):
    d = w_in.shape[1]
    wi = w_in[l]
    lo, hi = RW_COLS, RW_COLS + MLA_Q_RANK + MLA_KV_RANK + MLA_ROPE
    w_rw = jnp.pad(wi[:, :lo], ((0, 0), (0, RW_COLS_PAD - RW_COLS))).astype(BF16)
    w_kr = wi[:, hi - MLA_ROPE:hi]
    place = lambda w: jnp.pad(w, ((0, 0), (MLA_NOPE, LANES - MLA_NOPE - MLA_ROPE)))
    w_mla = jnp.concatenate([wi[:, lo:hi - MLA_ROPE], place(w_kr), place(_rope_swap_cols(w_kr))], 1).astype(BF16)
    w_gate = wi[:, hi:].astype(BF16)

    uq = mla_w_uq[l].reshape(MLA_Q_RANK, MLA_HEADS, MLA_QK)
    zeros = jnp.zeros((MLA_Q_RANK, MLA_HEADS, LANES - MLA_QK), F32)
    wqa = jnp.concatenate([uq, zeros], -1).reshape(MLA_Q_RANK, -1).astype(BF16)
    zn = jnp.zeros((MLA_Q_RANK, MLA_HEADS, MLA_NOPE), F32)
    wqs = jnp.concatenate([zn, _rope_swap_cols(uq[..., MLA_NOPE:]), zeros], -1).reshape(MLA_Q_RANK, -1).astype(BF16)
    ukv = mla_w_ukv[l].reshape(MLA_KV_RANK, MLA_HEADS, MLA_NOPE + MLA_V)
    wk = jnp.pad(ukv[..., :MLA_NOPE], ((0, 0), (0, 0), (0, LANES - MLA_NOPE))).reshape(MLA_KV_RANK, -1).astype(BF16)
    wv = ukv[..., MLA_NOPE:].reshape(MLA_KV_RANK, -1).astype(BF16)

    wup = _block_diag2(rk_w_up[l]).astype(BF16)
    aup = _block_diag2(rk_a_up[l]).astype(BF16)
    gup = jnp.pad(_block_diag2(rk_g_up[l]), ((0, LANES - 2 * RW_GATE_LORA), (0, 0))).astype(BF16)
    return dict(w_rw=w_rw, w_mla=w_mla, w_gate=w_gate, wqa=wqa, wqs=wqs, wk=wk, wv=wv, wup=wup, aup=aup, gup=gup)


def _rope_lane_tables(l, c):
    rows = l // GRID_W
    row = jnp.repeat(jnp.arange(rows, dtype=F32), GRID_W)
    col = jnp.tile(jnp.arange(GRID_W, dtype=F32), rows)
    n_freq = MLA_ROPE // 4
    freqs = ROPE_BASE ** (-jnp.arange(n_freq, dtype=F32) / n_freq)
    ang = jnp.concatenate([row[:, None] * freqs, col[:, None] * freqs], -1)
    cos, sin = jnp.cos(ang), jnp.sin(ang)
    tail = LANES - MLA_NOPE - MLA_ROPE
    cos_t = jnp.concatenate([jnp.ones((l, MLA_NOPE), F32), cos, cos, jnp.ones((l, tail), F32)], -1)
    sin_t = jnp.concatenate([jnp.zeros((l, MLA_NOPE), F32), sin, sin, jnp.zeros((l, tail), F32)], -1)
    cos_t = jnp.concatenate([cos_t, jnp.ones((c, LANES), F32)], 0)
    sin_t = jnp.concatenate([sin_t, jnp.zeros((c, LANES), F32)], 0)
    return cos_t, sin_t


def _peer(h2, w_q, keys, u_tab, v_tab, *, te, tm=256):
    m, d = h2.shape
    q = _mm(h2, w_q, tm=256, tn=_pick_tile(w_q.shape[1], 1024))
    idx_t, gate_t = _peer_topk(q, keys)
    idx = idx_t.transpose(2, 0, 1).reshape(m, PEER_SEL)
    gate = gate_t.transpose(2, 0, 1).reshape(m, PEER_SEL)
    g = _peer_gates(idx, gate)
    return _peer_dense(h2, u_tab, v_tab, g, tm=tm, te=te)


def _forward(x, c, ctx, c_ctx, ada_w, ada_b, norm_mix_g, w_in, gate_b, rk_mu, rk_w0, rk_w_up, rk_a0,
             rk_a_up, rk_g_up, rk_k_k, rk_k_a, rk_r_k, rk_ln_w, rk_ln_b, mla_q_norm, mla_w_uq, mla_kv_norm,
             mla_w_ukv, w_branch_a, w_branch_b, w_out, norm_ffn_g, peer_w_q, peer_keys, peer_u, peer_v,
             final_norm_g, *, flash_tq=512, flash_tk=2816, scan_tt=32, peer_te=2048):
    nb, l, d = x.shape
    cl = ctx.shape[1]
    t = l + cl
    m = nb * t
    depth = w_in.shape[0]
    dims = dict(B=nb, L=l, C=cl, T=t)
    assert nb == 2 and 4 * nb * RW_HEADS == LANES and l % 256 == 0 and cl % 256 == 0

    xa = jnp.concatenate([x, ctx], 1).reshape(m, d)
    cos_t, sin_t = _rope_lane_tables(l, cl)
    seg = jnp.arange(RW_WIDTH, dtype=I32) // RW_HEAD
    bd = (seg[:, None] == seg[None, :]).astype(BF16)
    cvec = jnp.zeros((SUBLANES, d), F32).at[:nb].set(jax.nn.silu(c)).at[nb].set(jax.nn.silu(c_ctx))

    delta, mods_prev = None, None
    out = None
    for li in range(depth):
        wl = _prep_layer_weights(li, w_in, mla_w_uq, mla_w_ukv, rk_w_up, rk_a_up, rk_g_up)
        tn_ada = _pick_tile(6 * d, 768)
        mods = _mm(cvec, ada_w[li], tm=SUBLANES, tn=tn_ada, epilogue=lambda acc, bias: acc + bias,
                   extras=[(ada_b[li][None, :], pl.BlockSpec((1, tn_ada), lambda j, i: (0, j)))])
        mods = mods[:nb + 1].reshape((nb + 1) * 6, 1, d)
        g_mix = norm_mix_g[li][None, :]
        if li == 0:
            h = _norm_mod(xa, g_mix, mods, 0, 1, dims)
        else:
            xa, h = _resid_norm_mod(xa, delta, mods_prev, 5, g_mix, mods, 0, 1, dims)

        p_rw = _mm(h, wl["w_rw"], tm=256, tn=_pick_tile(RW_COLS_PAD, 1152))
        p_mla = _mm(h, wl["w_mla"], tm=256, tn=MLA_P_COLS)
        p_gate = _mm(h, wl["w_gate"], tm=256, tn=1024)

        mu = jnp.pad(rk_mu[li], (0, RW_COLS_PAD - RW_COLS))[None, :]
        feats = _rwfeat(p_rw.reshape(nb, t, RW_COLS_PAD), mu, rk_w0[li], wl["wup"], rk_a0[li], wl["aup"], wl["gup"],
                        rk_k_k[li][None, :], rk_k_a[li][None, :], rk_r_k[li].reshape(1, RW_WIDTH), bd, dims)
        r, nkk, v, dec0, dec1, b0, b1, ke0, ke1, bon0, bon1, g0, g1 = feats
        ks = (_to_scan_k(dec0, dec1), _to_scan_k(b0, b1), _to_scan_k(ke0, ke1), _to_scan_k(r, r),
              _to_scan_k(*_next_step_rows(nkk, l)))
        ynf, ynr = _scan(ks, _to_scan_v(v), l, tt=scan_tt)
        yn0, yn1 = _from_scan_y(ynf, ynr, nb)
        flat = lambda z: z.reshape(m, RW_WIDTH)
        ya = _readout(flat(yn0), flat(yn1), flat(bon0), flat(bon1), flat(g0), flat(g1),
                      rk_ln_w[li][None, :], rk_ln_b[li][None, :])

        q, k, vv = _mla_prep(p_mla, cos_t, sin_t, mla_q_norm[li][None, :], mla_kv_norm[li][None, :],
                             wl["wqa"], wl["wqs"], wl["wk"], wl["wv"], dims)
        q3, k3, v3 = (z.reshape(nb, t, -1) for z in (q, k, vv))
        yb = _flash(q3, k3, v3, tq=flash_tq, tk=flash_tk, q_off=0, nq=l // flash_tq, k_off=0, nk=t // flash_tk)
        yb = _flash(q3, k3, v3, tq=cl, tk=cl, q_off=l // cl, nq=1, k_off=l // cl, nk=1, prev_out=yb)
        yb = yb.reshape(m, MLA_HEADS * MLA_V)

        tpb, lt = t // 256, l // 256
        br_a = _mm(ya, w_branch_a[li].astype(BF16), tm=256, tn=1024)
        gb = gate_b[li]
        merged = _mm(
            yb, w_branch_b[li].astype(BF16), tm=256, tn=1024, out_dtype=BF16,
            epilogue=lambda acc, ga, gbb, ba, bb, a: _sigmoid(ga + ba) * a + _sigmoid(gbb + bb) * acc,
            extras=[(p_gate, pl.BlockSpec((256, 1024), lambda j, i: (i, j))),
                    (p_gate, pl.BlockSpec((256, 1024), lambda j, i: (i, j + d // 1024))),
                    (gb[0][None, :], pl.BlockSpec((1, 1024), lambda j, i: (0, j))),
                    (gb[1][None, :], pl.BlockSpec((1, 1024), lambda j, i: (0, j))),
                    (br_a, pl.BlockSpec((256, 1024), lambda j, i: (i, j)))])
        x1 = _mm(merged, w_out[li].astype(BF16), tm=256, tn=1024,
                 epilogue=lambda acc, xr, gt: xr + gt * acc,
                 extras=[(xa, pl.BlockSpec((256, 1024), lambda j, i: (i, j))),
                         (mods, pl.BlockSpec((None, 1, 1024), lambda j, i: (_mod_row(i, tpb, lt, 2, nb), 0, j)))])

        h2 = _norm_mod(x1, norm_ffn_g[li][None, :], mods, 3, 4, dims)
        delta = _peer(h2, peer_w_q[li].astype(BF16), peer_keys[li], peer_u[li].astype(BF16),
                      peer_v[li].astype(BF16), te=peer_te)
        xa, mods_prev = x1, mods

    out = _resid_final_norm(xa.reshape(nb, t, d), delta.reshape(nb, t, d), mods_prev, 5,
                            final_norm_g[None, :], dims)
    return out


def kernel(x, c, ctx, c_ctx, ada_w, ada_b, norm_mix_g, w_in, gate_b, rk_mu, rk_w0, rk_w_up, rk_a0, rk_a_up, rk_g_up, rk_k_k, rk_k_a, rk_r_k, rk_ln_w, rk_ln_b, mla_q_norm, mla_w_uq, mla_kv_norm, mla_w_ukv, w_branch_a, w_branch_b, w_out, norm_ffn_g, peer_w_q, peer_keys, peer_u, peer_v, final_norm_g):
    return _forward(x, c, ctx, c_ctx, ada_w, ada_b, norm_mix_g, w_in, gate_b, rk_mu, rk_w0, rk_w_up, rk_a0,
                    rk_a_up, rk_g_up, rk_k_k, rk_k_a, rk_r_k, rk_ln_w, rk_ln_b, mla_q_norm, mla_w_uq,
                    mla_kv_norm, mla_w_ukv, w_branch_a, w_branch_b, w_out, norm_ffn_g, peer_w_q, peer_keys,
                    peer_u, peer_v, final_norm_g)
```
